```python
import jax, jax.numpy as jnp
from jax import lax
import numpy as np

D_MODEL = 2048
BATCH = 4
SEQ = 4096
DEPTH = 2
DEC_BATCH = 8
DEC_SEQ = 2048
PAST_LEN = 128

N_MIXERS = 2
N_ATTN_LAYERS = (DEPTH + 1) // 2
N_FOURIER_LAYERS = DEPTH // 2
HEAD_DIM = 128
N_HEADS = D_MODEL // HEAD_DIM
N_KV_HEADS = N_HEADS // 2
Q_PER_KV = N_HEADS // N_KV_HEADS
QKV_DIM = (N_HEADS + 2 * N_KV_HEADS) * HEAD_DIM
ROPE_AXIS_DIM = HEAD_DIM // 2
ROPE_THETA = 10000.0
GRID_W = 64
Q_BLOCK = 128
N_FOURIER_GROUPS = 8
FOURIER_GROUP_DIM = D_MODEL // N_FOURIER_GROUPS
N_EXPERT_GROUPS = 4
EXPERTS_PER_GROUP = 8
N_EXPERTS = N_EXPERT_GROUPS * EXPERTS_PER_GROUP
TOP_K = 2
D_EXPERT = D_MODEL // 4
EXPERT_BLOCK = 128
PLE_DIM = 256
NORM_EPS = 1e-6

kernel_name = "hybrid_attn_fnet_hmoe_encoder"


def rms_norm(x, g):
    xf = x.astype(jnp.float32)
    y = xf * lax.rsqrt(jnp.mean(xf * xf, axis=-1, keepdims=True) + NORM_EPS)
    return (y * g.astype(jnp.float32)).astype(x.dtype)


def axial_rope_tables(seq_len):
    rows = seq_len // GRID_W
    row_idx = jnp.broadcast_to(jnp.arange(rows, dtype=jnp.float32)[:, None], (rows, GRID_W)).reshape(-1)
    col_idx = jnp.broadcast_to(jnp.arange(GRID_W, dtype=jnp.float32)[None, :], (rows, GRID_W)).reshape(-1)
    inv_freq = ROPE_THETA ** (-jnp.arange(0, ROPE_AXIS_DIM, 2, dtype=jnp.float32) / ROPE_AXIS_DIM)
    ang_r = row_idx[:, None] * inv_freq[None, :]
    ang_c = col_idx[:, None] * inv_freq[None, :]
    return jnp.cos(ang_r), jnp.sin(ang_r), jnp.cos(ang_c), jnp.sin(ang_c)


def rotate(x, c, s):
    half = x.shape[-1] // 2
    x1, x2 = x[..., :half], x[..., half:]
    c = c[None, :, None, :]
    s = s[None, :, None, :]
    return jnp.concatenate([x1 * c - x2 * s, x1 * s + x2 * c], axis=-1)


def apply_axial_rope(x, tables):
    cr, sr, cc, sc = tables
    xf = x.astype(jnp.float32)
    out = jnp.concatenate([rotate(xf[..., :ROPE_AXIS_DIM], cr, sr),
                           rotate(xf[..., ROPE_AXIS_DIM:], cc, sc)], axis=-1)
    return out.astype(x.dtype)


def attention(a, w_qkv, g_q, g_k, w_o):
    B, S, _ = a.shape
    qkv = a @ w_qkv
    q = qkv[..., :N_HEADS * HEAD_DIM].reshape(B, S, N_HEADS, HEAD_DIM)
    k = qkv[..., N_HEADS * HEAD_DIM:(N_HEADS + N_KV_HEADS) * HEAD_DIM].reshape(B, S, N_KV_HEADS, HEAD_DIM)
    v = qkv[..., (N_HEADS + N_KV_HEADS) * HEAD_DIM:].reshape(B, S, N_KV_HEADS, HEAD_DIM)
    q = rms_norm(q, g_q)
    k = rms_norm(k, g_k)
    tables = axial_rope_tables(S)
    q = apply_axial_rope(q, tables)
    k = apply_axial_rope(k, tables)
    n_qb = S // Q_BLOCK
    qb = q.reshape(B, n_qb, Q_BLOCK, N_KV_HEADS, Q_PER_KV, HEAD_DIM).transpose(1, 0, 2, 3, 4, 5)
    scale = 1.0 / float(np.sqrt(HEAD_DIM))

    def block(q_blk):
        s = jnp.einsum('bqkgd,bskd->bkgqs', q_blk, k).astype(jnp.float32) * scale
        p = jax.nn.softmax(s, axis=-1).astype(v.dtype)
        return jnp.einsum('bkgqs,bskd->bqkgd', p, v)

    o = lax.map(block, qb)
    o = o.transpose(1, 0, 2, 3, 4, 5).reshape(B, S, N_HEADS * HEAD_DIM)
    return o @ w_o


def fourier_mix(a, w_out):
    B, S, D = a.shape
    af = a.astype(jnp.float32).reshape(B, S, N_FOURIER_GROUPS, FOURIER_GROUP_DIM)
    f = jnp.fft.fft2(af, axes=(1, 3), norm='ortho').real
    return f.reshape(B, S, D).astype(a.dtype) @ w_out


def hier_moe(m, w_rg, b_rg, w_re, b_re, w_gate, w_up, w_down):
    B, S, D = m.shape
    T = B * S
    mt = m.reshape(T, D)
    lg = (mt @ w_rg).astype(jnp.float32) + b_rg.astype(jnp.float32)
    pg = jax.nn.softmax(lg, axis=-1)
    g_sel = jnp.argmax(lg, axis=-1).astype(jnp.int32)
    pg_sel = jnp.take_along_axis(pg, g_sel[:, None], axis=1)
    le = ((mt @ w_re).astype(jnp.float32) + b_re.astype(jnp.float32)).reshape(T, N_EXPERT_GROUPS, EXPERTS_PER_GROUP)
    le_sel = jnp.take_along_axis(le, g_sel[:, None, None], axis=1)[:, 0]
    pe = jax.nn.softmax(le_sel, axis=-1)
    top_w, top_i = lax.top_k(pe, TOP_K)
    w_tok = pg_sel * top_w / jnp.sum(top_w, axis=-1, keepdims=True)
    e_tok = g_sel[:, None] * EXPERTS_PER_GROUP + top_i.astype(jnp.int32)
    n_assign = T * TOP_K
    flat_e = e_tok.reshape(-1)
    flat_w = w_tok.reshape(-1)
    flat_t = jnp.arange(n_assign, dtype=jnp.int32) // TOP_K
    order = jnp.argsort(flat_e)
    se = flat_e[order]
    counts = jnp.bincount(flat_e, length=N_EXPERTS).astype(jnp.int32)
    pad_counts = (counts + EXPERT_BLOCK - 1) // EXPERT_BLOCK * EXPERT_BLOCK
    pad_end = jnp.cumsum(pad_counts)
    pad_start = pad_end - pad_counts
    start = jnp.cumsum(counts) - counts
    dest = pad_start[se] + jnp.arange(n_assign, dtype=jnp.int32) - start[se]
    n_blocks = -(-n_assign // EXPERT_BLOCK) + N_EXPERTS
    n_rows = n_blocks * EXPERT_BLOCK
    rows_t = jnp.zeros((n_rows,), jnp.int32).at[dest].set(flat_t[order])
    rows_w = jnp.zeros((n_rows,), jnp.float32).at[dest].set(flat_w[order])
    blk_e = jnp.minimum(jnp.searchsorted(pad_end, jnp.arange(n_blocks, dtype=jnp.int32) * EXPERT_BLOCK, side='right'),
                        N_EXPERTS - 1).astype(jnp.int32)
    xb = mt[rows_t].reshape(n_blocks, EXPERT_BLOCK, D)

    def expert_block(args):
        x_blk, e = args
        h = jax.nn.silu(x_blk @ w_gate[e]) * (x_blk @ w_up[e])
        return h @ w_down[e]

    yb = lax.map(expert_block, (xb, blk_e)).reshape(n_rows, D)
    y = jnp.zeros((T, D), jnp.float32).at[rows_t].add(yb.astype(jnp.float32) * rows_w[:, None])
    return y.astype(m.dtype).reshape(B, S, D)


def trunk(x, p, g_mix, w_qkv, g_q, g_k, w_attn_out, w_fourier_out, g_ffn,
          w_route_group, b_route_group, w_route_expert, b_route_expert,
          w_exp_gate, w_exp_up, w_exp_down, g_ple, w_ple_gate, w_ple_proj, g_final):
    h = x
    for i in range(DEPTH):
        a = rms_norm(h, g_mix[i])
        j = i // N_MIXERS
        if i % N_MIXERS == 0:
            h = h + attention(a, w_qkv[j], g_q[j], g_k[j], w_attn_out[j])
        else:
            h = h + fourier_mix(a, w_fourier_out[j])
        h = h + hier_moe(rms_norm(h, g_ffn[i]), w_route_group[i], b_route_group[i],
                         w_route_expert[i], b_route_expert[i],
                         w_exp_gate[i], w_exp_up[i], w_exp_down[i])
        gate = jax.nn.sigmoid((rms_norm(h, g_ple[i]) @ w_ple_gate[i]).astype(jnp.float32))
        emb = (p[i] @ w_ple_proj[i]).astype(jnp.float32)
        h = h + (gate * emb).astype(h.dtype)
    return rms_norm(h, g_final)


def setup_inputs(seed: int = 0) -> dict:
    key = jax.random.key(seed)
    ks = jax.random.split(key, 24)
    f32 = jnp.float32

    def nrm(k, shape, scale):
        return jax.random.normal(k, shape, f32) * scale

    def gain(k, shape):
        return 1.0 + 0.01 * jax.random.normal(k, shape, f32)

    D = D_MODEL
    return {
        "x_prompt": nrm(ks[0], (BATCH, SEQ, D), 1.0),
        "x_sample": nrm(ks[1], (DEC_BATCH, DEC_SEQ, D), 1.0),
        "p_prompt": nrm(ks[2], (DEPTH, BATCH, SEQ, PLE_DIM), 1.0),
        "p_sample": nrm(ks[3], (DEPTH, DEC_BATCH, DEC_SEQ, PLE_DIM), 1.0),
        "g_mix": gain(ks[4], (DEPTH, D)),
        "w_qkv": nrm(ks[5], (N_ATTN_LAYERS, D, QKV_DIM), D ** -0.5),
        "g_q": gain(ks[6], (N_ATTN_LAYERS, HEAD_DIM)),
        "g_k": gain(ks[7], (N_ATTN_LAYERS, HEAD_DIM)),
        "w_attn_out": nrm(ks[8], (N_ATTN_LAYERS, N_HEADS * HEAD_DIM, D), (N_HEADS * HEAD_DIM) ** -0.5),
        "w_fourier_out": nrm(ks[9], (N_FOURIER_LAYERS, D, D), D ** -0.5),
        "g_ffn": gain(ks[10], (DEPTH, D)),
        "w_route_group": nrm(ks[11], (DEPTH, D, N_EXPERT_GROUPS), D ** -0.5),
        "b_route_group": nrm(ks[12], (DEPTH, N_EXPERT_GROUPS), 0.01),
        "w_route_expert": nrm(ks[13], (DEPTH, D, N_EXPERTS), D ** -0.5),
        "b_route_expert": nrm(ks[14], (DEPTH, N_EXPERTS), 0.01),
        "w_exp_gate": nrm(ks[15], (DEPTH, N_EXPERTS, D, D_EXPERT), D ** -0.5),
        "w_exp_up": nrm(ks[16], (DEPTH, N_EXPERTS, D, D_EXPERT), D ** -0.5),
        "w_exp_down": nrm(ks[17], (DEPTH, N_EXPERTS, D_EXPERT, D), D_EXPERT ** -0.5),
        "g_ple": gain(ks[18], (DEPTH, D)),
        "w_ple_gate": nrm(ks[19], (DEPTH, D, D), D ** -0.5),
        "w_ple_proj": nrm(ks[20], (DEPTH, PLE_DIM, D), PLE_DIM ** -0.5),
        "g_final": gain(ks[21], (D,)),
    }


def reference(x_prompt, x_sample, p_prompt, p_sample, g_mix, w_qkv, g_q, g_k, w_attn_out,
              w_fourier_out, g_ffn, w_route_group, b_route_group, w_route_expert, b_route_expert,
              w_exp_gate, w_exp_up, w_exp_down, g_ple, w_ple_gate, w_ple_proj, g_final):
    y_prompt = trunk(x_prompt, p_prompt, g_mix, w_qkv, g_q, g_k, w_attn_out, w_fourier_out, g_ffn,
                     w_route_group, b_route_group, w_route_expert, b_route_expert,
                     w_exp_gate, w_exp_up, w_exp_down, g_ple, w_ple_gate, w_ple_proj, g_final)
    y_sample = trunk(x_sample, p_sample, g_mix, w_qkv, g_q, g_k, w_attn_out, w_fourier_out, g_ffn,
                     w_route_group, b_route_group, w_route_expert, b_route_expert,
                     w_exp_gate, w_exp_up, w_exp_down, g_ple, w_ple_gate, w_ple_proj, g_final)
    return (y_prompt, y_sample)
```

```python
import functools

import jax
import jax.numpy as jnp
import numpy as np
from jax import lax
from jax.experimental import pallas as pl
from jax.experimental.pallas import tpu as pltpu

F32 = jnp.float32
BF16 = jnp.bfloat16

HEAD_DIM = 128
N_HEADS = 16
N_KV_HEADS = 8
Q_PER_KV = N_HEADS // N_KV_HEADS
ROPE_AXIS_DIM = HEAD_DIM // 2
ROPE_THETA = 10000.0
GRID_W = 64
N_FOURIER_GROUPS = 8
N_EXPERT_GROUPS = 4
EXPERTS_PER_GROUP = 8
N_EXPERTS = N_EXPERT_GROUPS * EXPERTS_PER_GROUP
TOP_K = 2
NORM_EPS = 1e-6

LANES = 128
VMEM_LIMIT = 48 * 1024 * 1024

TM = 512
TN = 512
TQ = 256
TKV = 512
EXPERT_BLOCK = 256
GATHER_ROWS = 256
DFT_TM, DFT_TN, DFT_TK = 1024, 1024, 512


def _params(*sem):
    return pltpu.CompilerParams(dimension_semantics=sem, vmem_limit_bytes=VMEM_LIMIT)


def _rms(x, g):
    ms = jnp.mean(x * x, axis=-1, keepdims=True)
    return x * lax.rsqrt(ms + NORM_EPS) * g


def _qkv_kernel(x_ref, g_ref, w_ref, gq_ref, gk_ref, cos_ref, sin_ref, o_ref, a_ref, *, tn, d_q, d_qk):
    j = pl.program_id(1)

    @pl.when(j == 0)
    def _():
        a_ref[...] = _rms(x_ref[...], g_ref[...]).astype(BF16)

    acc = jnp.dot(a_ref[...], w_ref[...], preferred_element_type=F32)

    def norm_rope(gain, scale):
        cos = cos_ref[...]
        sin = sin_ref[...]
        lane = lax.broadcasted_iota(jnp.int32, cos.shape, 1)
        first_half = (lane % ROPE_AXIS_DIM) < (ROPE_AXIS_DIM // 2)
        for c in range(tn // HEAD_DIM):
            sl = slice(c * HEAD_DIM, (c + 1) * HEAD_DIM)
            y = _rms(acc[:, sl], gain)
            partner = jnp.where(first_half,
                                pltpu.roll(y, HEAD_DIM - ROPE_AXIS_DIM // 2, 1),
                                pltpu.roll(y, ROPE_AXIS_DIM // 2, 1))
            o_ref[:, sl] = ((y * cos + partner * sin) * scale).astype(BF16)

    @pl.when(j < d_q // tn)
    def _():
        norm_rope(gq_ref[...], 1.0 / float(np.sqrt(HEAD_DIM)))

    @pl.when((j >= d_q // tn) & (j < d_qk // tn))
    def _():
        norm_rope(gk_ref[...], 1.0)

    @pl.when(j >= d_qk // tn)
    def _():
        o_ref[...] = acc.astype(BF16)


def _rope_tables(n_pos):
    pos = jnp.arange(n_pos, dtype=jnp.int32)
    row = (pos // GRID_W).astype(F32)
    col = (pos % GRID_W).astype(F32)
    inv_freq = ROPE_THETA ** (-jnp.arange(0, ROPE_AXIS_DIM, 2, dtype=F32) / ROPE_AXIS_DIM)
    ang_r = row[:, None] * inv_freq[None, :]
    ang_c = col[:, None] * inv_freq[None, :]
    cos = jnp.concatenate([jnp.cos(ang_r), jnp.cos(ang_r), jnp.cos(ang_c), jnp.cos(ang_c)], axis=-1)
    sin = jnp.concatenate([-jnp.sin(ang_r), jnp.sin(ang_r), -jnp.sin(ang_c), jnp.sin(ang_c)], axis=-1)
    return cos, sin


def _qkv_proj(h, g, w, gq, gk, cos, sin, groups):
    T, D = h.shape
    N = w.shape[1]
    (t0, s0), (t1, s1) = groups
    assert s0 % TM == 0 and s1 % TM == 0 and T % TM == 0 and N % TN == 0

    def pos_map(i, j):
        r = i * TM
        p = jnp.where(r < t0, r % s0, (r - t0) % s1)
        return (p // TM, 0)

    d_q = N_HEADS * HEAD_DIM
    d_qk = d_q + N_KV_HEADS * HEAD_DIM
    kern = functools.partial(_qkv_kernel, tn=TN, d_q=d_q, d_qk=d_qk)
    return pl.pallas_call(
        kern,
        out_shape=jax.ShapeDtypeStruct((T, N), BF16),
        grid=(T // TM, N // TN),
        in_specs=[
            pl.BlockSpec((TM, D), lambda i, j: (i, 0)),
            pl.BlockSpec((1, D), lambda i, j: (0, 0)),
            pl.BlockSpec((D, TN), lambda i, j: (0, j)),
            pl.BlockSpec((1, HEAD_DIM), lambda i, j: (0, 0)),
            pl.BlockSpec((1, HEAD_DIM), lambda i, j: (0, 0)),
            pl.BlockSpec((TM, HEAD_DIM), pos_map),
            pl.BlockSpec((TM, HEAD_DIM), pos_map),
        ],
        out_specs=pl.BlockSpec((TM, TN), lambda i, j: (i, j)),
        scratch_shapes=[pltpu.VMEM((TM, D), BF16)],
        compiler_params=_params("parallel", "arbitrary"),
        name="qkv_proj",
    )(h, g, w, gq, gk, cos, sin)


def _attn_kernel(q_ref, k_ref, v_ref, o_ref, *, tq, tk, seq):
    q = q_ref[...]
    q2 = jnp.concatenate([q[:, :HEAD_DIM], q[:, HEAD_DIM:]], axis=0)

    def body(c, carry):
        m, l, acc = carry
        off = pl.multiple_of(c * tk, tk)
        k = k_ref[pl.ds(off, tk), :]
        v = v_ref[pl.ds(off, tk), :]
        s = lax.dot_general(q2, k, (((1,), (1,)), ((), ())), preferred_element_type=F32)
        m_new = jnp.maximum(m, jnp.max(s, axis=-1, keepdims=True))
        alpha = jnp.exp(m - m_new)
        p = jnp.exp(s - m_new)
        l = alpha * l + jnp.sum(p, axis=-1, keepdims=True)
        acc = alpha * acc + jnp.dot(p.astype(BF16), v, preferred_element_type=F32)
        return m_new, l, acc

    m0 = jnp.full((2 * tq, 1), -jnp.inf, F32)
    l0 = jnp.zeros((2 * tq, 1), F32)
    acc0 = jnp.zeros((2 * tq, HEAD_DIM), F32)
    _, l, acc = lax.fori_loop(0, seq // tk, body, (m0, l0, acc0))
    o = acc / l
    o_ref[:, :HEAD_DIM] = o[:tq].astype(BF16)
    o_ref[:, HEAD_DIM:] = o[tq:].astype(BF16)


def _attention(qkv, row0, batch, seq):
    assert row0 % seq == 0 and seq % TQ == 0 and seq % TKV == 0
    qw = Q_PER_KV * HEAD_DIM
    kern = functools.partial(_attn_kernel, tq=TQ, tk=TKV, seq=seq)
    q_blk0 = row0 // TQ
    s_blk0 = row0 // seq
    k_col0 = N_HEADS
    v_col0 = N_HEADS + N_KV_HEADS
    return pl.pallas_call(
        kern,
        out_shape=jax.ShapeDtypeStruct((batch * seq, N_HEADS * HEAD_DIM), BF16),
        grid=(batch, N_KV_HEADS, seq // TQ),
        in_specs=[
            pl.BlockSpec((TQ, qw), lambda b, h, i: (q_blk0 + b * (seq // TQ) + i, h)),
            pl.BlockSpec((seq, HEAD_DIM), lambda b, h, i: (s_blk0 + b, k_col0 + h)),
            pl.BlockSpec((seq, HEAD_DIM), lambda b, h, i: (s_blk0 + b, v_col0 + h)),
        ],
        out_specs=pl.BlockSpec((TQ, qw), lambda b, h, i: (b * (seq // TQ) + i, h)),
        compiler_params=_params("parallel", "parallel", "arbitrary"),
        name="attention",
    )(qkv, qkv, qkv)


def _mm_res_kernel(x_ref, w_ref, r_ref, o_ref):
    o_ref[...] = r_ref[...] + jnp.dot(x_ref[...], w_ref[...], preferred_element_type=F32)


def _matmul_residual(x, w, res):
    T, K = x.shape
    N = w.shape[1]
    return pl.pallas_call(
        _mm_res_kernel,
        out_shape=jax.ShapeDtypeStruct((T, N), F32),
        grid=(T // TM, N // TN),
        in_specs=[
            pl.BlockSpec((TM, K), lambda i, j: (i, 0)),
            pl.BlockSpec((K, TN), lambda i, j: (0, j)),
            pl.BlockSpec((TM, TN), lambda i, j: (i, j)),
        ],
        out_specs=pl.BlockSpec((TM, TN), lambda i, j: (i, j)),
        compiler_params=_params("parallel", "arbitrary"),
        name="matmul_residual",
    )(x, w, res)


R_E1, R_E2, R_W1, R_W2, R_RANK1, R_RANK2 = range(6)
ROUTE_LOGIT_E0 = N_EXPERT_GROUPS


def _route_kernel(x_ref, g_ref, w_ref, b_ref, tri_ref, r_ref, cnt_ref, carry_ref):
    @pl.when(pl.program_id(0) == 0)
    def _():
        carry_ref[...] = jnp.zeros_like(carry_ref)

    a = _rms(x_ref[...], g_ref[...])
    logits = jnp.dot(a, w_ref[...], preferred_element_type=F32,
                     precision=lax.Precision.HIGHEST) + b_ref[...]
    lane = lax.broadcasted_iota(jnp.int32, logits.shape, 1)
    neg = -jnp.inf
    far = LANES

    def first_max(vals):
        top = jnp.max(vals, axis=-1, keepdims=True)
        idx = jnp.min(jnp.where(vals == top, lane, far), axis=-1, keepdims=True)
        return top, idx

    gmask = lane < N_EXPERT_GROUPS
    gtop, g_sel = first_max(jnp.where(gmask, logits, neg))
    pg = 1.0 / jnp.sum(jnp.where(gmask, jnp.exp(logits - gtop), 0.0), axis=-1, keepdims=True)
    lo = ROUTE_LOGIT_E0 + g_sel * EXPERTS_PER_GROUP
    le = jnp.where((lane >= lo) & (lane < lo + EXPERTS_PER_GROUP), logits, neg)
    t1, i1 = first_max(le)
    t2, i2 = first_max(jnp.where(lane == i1, neg, le))
    r21 = jnp.exp(t2 - t1)
    w1 = pg / (1.0 + r21)
    w2 = pg * r21 / (1.0 + r21)
    e1 = i1 - ROUTE_LOGIT_E0
    e2 = i2 - ROUTE_LOGIT_E0

    hit1 = lane == e1
    hit2 = lane == e2
    onehot = (hit1 | hit2).astype(F32)
    before = jnp.dot(tri_ref[...], onehot.astype(BF16), preferred_element_type=F32) + carry_ref[0:1, :]
    rank1 = jnp.sum(jnp.where(hit1, before, 0.0), axis=-1, keepdims=True)
    rank2 = jnp.sum(jnp.where(hit2, before, 0.0), axis=-1, keepdims=True)
    carry_ref[...] = carry_ref[...] + jnp.sum(onehot, axis=0, keepdims=True)
    cnt_ref[...] = carry_ref[...]

    rec = jnp.zeros(logits.shape, F32)
    for pos, val in ((R_E1, e1.astype(F32)), (R_E2, e2.astype(F32)), (R_W1, w1), (R_W2, w2),
                     (R_RANK1, rank1), (R_RANK2, rank2)):
        rec = jnp.where(lane == pos, val, rec)
    r_ref[...] = rec


def _route(h, g, w_r, b_r):
    T, D = h.shape
    tri = (jnp.arange(TM)[:, None] > jnp.arange(TM)[None, :]).astype(BF16)
    return pl.pallas_call(
        _route_kernel,
        out_shape=(jax.ShapeDtypeStruct((T, LANES), F32), jax.ShapeDtypeStruct((8, LANES), F32)),
        grid=(T // TM,),
        in_specs=[
            pl.BlockSpec((TM, D), lambda i: (i, 0)),
            pl.BlockSpec((1, D), lambda i: (0, 0)),
            pl.BlockSpec((D, LANES), lambda i: (0, 0)),
            pl.BlockSpec((1, LANES), lambda i: (0, 0)),
            pl.BlockSpec((TM, TM), lambda i: (0, 0)),
        ],
        out_specs=(pl.BlockSpec((TM, LANES), lambda i: (i, 0)),
                   pl.BlockSpec((8, LANES), lambda i: (0, 0))),
        scratch_shapes=[pltpu.VMEM((8, LANES), F32)],
        compiler_params=_params("arbitrary"),
        name="moe_route",
    )(h, g, w_r, b_r, tri)


def _row_copy(src_hbm, dst_ref, sem, src_row, dst_row):
    return pltpu.make_async_copy(src_hbm.at[pl.ds(src_row, 1), :], dst_ref.at[pl.ds(dst_row, 1), :], sem)


def _gather_kernel(idx_ref, src_hbm, o_ref, sem, *, rows):
    base = pl.program_id(0) * rows

    def start(r, c):
        _row_copy(src_hbm, o_ref, sem, idx_ref[base + r], r).start()
        return c

    def wait(r, c):
        _row_copy(src_hbm, o_ref, sem, 0, r).wait()
        return c

    lax.fori_loop(0, rows, start, 0)
    lax.fori_loop(0, rows, wait, 0)


def _gather_rows(src, idx):
    n = idx.shape[0]
    D = src.shape[1]
    assert n % GATHER_ROWS == 0
    kern = functools.partial(_gather_kernel, rows=GATHER_ROWS)
    return pl.pallas_call(
        kern,
        out_shape=jax.ShapeDtypeStruct((n, D), src.dtype),
        grid_spec=pltpu.PrefetchScalarGridSpec(
            num_scalar_prefetch=1,
            grid=(n // GATHER_ROWS,),
            in_specs=[pl.BlockSpec(memory_space=pl.ANY)],
            out_specs=pl.BlockSpec((GATHER_ROWS, D), lambda i, idx: (i, 0)),
            scratch_shapes=[pltpu.SemaphoreType.DMA(())],
        ),
        compiler_params=_params("arbitrary"),
        name="moe_dispatch_gather",
    )(idx, src)


def _expert_kernel(be_ref, nu_ref, x_ref, g_ref, wg_ref, wu_ref, wd_ref, o_ref):
    b = pl.program_id(0)

    @pl.when(b < nu_ref[0])
    def _():
        a = _rms(x_ref[...], g_ref[...]).astype(BF16)
        hg = jnp.dot(a, wg_ref[0], preferred_element_type=F32)
        hu = jnp.dot(a, wu_ref[0], preferred_element_type=F32)
        h = hg * jax.nn.sigmoid(hg) * hu
        o_ref[...] = jnp.dot(h.astype(BF16), wd_ref[0], preferred_element_type=F32)

    @pl.when(b >= nu_ref[0])
    def _():
        o_ref[...] = jnp.zeros_like(o_ref)


def _experts(xb, g, w_gate, w_up, w_down, blk_e, n_used):
    n_rows, D = xb.shape
    De = w_gate.shape[2]
    bm = EXPERT_BLOCK
    return pl.pallas_call(
        _expert_kernel,
        out_shape=jax.ShapeDtypeStruct((n_rows, D), F32),
        grid_spec=pltpu.PrefetchScalarGridSpec(
            num_scalar_prefetch=2,
            grid=(n_rows // bm,),
            in_specs=[
                pl.BlockSpec((bm, D), lambda b, be, nu: (b, 0)),
                pl.BlockSpec((1, D), lambda b, be, nu: (0, 0)),
                pl.BlockSpec((1, D, De), lambda b, be, nu: (be[b], 0, 0)),
                pl.BlockSpec((1, D, De), lambda b, be, nu: (be[b], 0, 0)),
                pl.BlockSpec((1, De, D), lambda b, be, nu: (be[b], 0, 0)),
            ],
            out_specs=pl.BlockSpec((bm, D), lambda b, be, nu: (b, 0)),
        ),
        compiler_params=_params("arbitrary"),
        name="moe_experts",
    )(blk_e, n_used, xb, g, w_gate, w_up, w_down)


def _combine_kernel(pos_ref, h_ref, r_ref, yb_hbm, o_ref, y1_ref, y2_ref, sem, *, rows):
    base = pl.program_id(0) * rows

    def start(r, c):
        _row_copy(yb_hbm, y1_ref, sem, pos_ref[TOP_K * (base + r)], r).start()
        _row_copy(yb_hbm, y2_ref, sem, pos_ref[TOP_K * (base + r) + 1], r).start()
        return c

    def wait(r, c):
        _row_copy(yb_hbm, y1_ref, sem, 0, r).wait()
        _row_copy(yb_hbm, y2_ref, sem, 0, r).wait()
        return c

    lax.fori_loop(0, rows, start, 0)
    lax.fori_loop(0, rows, wait, 0)
    rec = r_ref[...]
    w1 = rec[:, R_W1:R_W1 + 1]
    w2 = rec[:, R_W2:R_W2 + 1]
    o_ref[...] = h_ref[...] + (w1 * y1_ref[...] + w2 * y2_ref[...])


def _combine(h, rec, yb, pos):
    T, D = h.shape
    rows = GATHER_ROWS
    kern = functools.partial(_combine_kernel, rows=rows)
    return pl.pallas_call(
        kern,
        out_shape=jax.ShapeDtypeStruct((T, D), F32),
        grid_spec=pltpu.PrefetchScalarGridSpec(
            num_scalar_prefetch=1,
            grid=(T // rows,),
            in_specs=[
                pl.BlockSpec((rows, D), lambda i, pos: (i, 0)),
                pl.BlockSpec((rows, LANES), lambda i, pos: (i, 0)),
                pl.BlockSpec(memory_space=pl.ANY),
            ],
            out_specs=pl.BlockSpec((rows, D), lambda i, pos: (i, 0)),
            scratch_shapes=[pltpu.VMEM((rows, D), F32), pltpu.VMEM((rows, D), F32),
                            pltpu.SemaphoreType.DMA(())],
        ),
        compiler_params=_params("arbitrary"),
        name="moe_combine",
    )(pos, h, rec, yb)


def _hier_moe(h, g, w_rg, b_rg, w_re, b_re, w_gate, w_up, w_down):
    T, D = h.shape
    pad = LANES - N_EXPERT_GROUPS - N_EXPERTS
    w_r = jnp.concatenate([w_rg, w_re, jnp.zeros((D, pad), F32)], axis=1)
    b_r = jnp.concatenate([b_rg, b_re, jnp.zeros((pad,), F32)])[None, :]
    rec, cnt = _route(h, g, w_r, b_r)

    bm = EXPERT_BLOCK
    e_tok = rec[:, R_E1:R_E2 + 1].astype(jnp.int32)
    rank = rec[:, R_RANK1:R_RANK2 + 1].astype(jnp.int32)
    counts = cnt[0, :N_EXPERTS].astype(jnp.int32)
    pad_counts = (counts + bm - 1) // bm * bm
    pad_end = jnp.cumsum(pad_counts)
    pad_start = pad_end - pad_counts
    dest = (pad_start[e_tok] + rank).reshape(-1)
    n_assign = T * TOP_K
    n_blocks = n_assign // bm + N_EXPERTS
    n_rows = n_blocks * bm
    flat_t = jnp.arange(n_assign, dtype=jnp.int32) // TOP_K
    rows_t = jnp.zeros((n_rows,), jnp.int32).at[dest].set(flat_t)
    blk_e = jnp.minimum(jnp.searchsorted(pad_end, jnp.arange(n_blocks, dtype=jnp.int32) * bm, side='right'),
                        N_EXPERTS - 1).astype(jnp.int32)
    n_used = (pad_end[-1:] // bm).astype(jnp.int32)

    xb = _gather_rows(h, rows_t)
    yb = _experts(xb, g, w_gate, w_up, w_down, blk_e, n_used)
    return _combine(h, rec, yb, dest)


def _ple_kernel(h_ref, g_ref, wg_ref, p_ref, wp_ref, o_ref, a_ref, *, tn):
    j = pl.program_id(1)

    @pl.when(j == 0)
    def _():
        a_ref[...] = _rms(h_ref[...], g_ref[...]).astype(BF16)

    gate = jax.nn.sigmoid(jnp.dot(a_ref[...], wg_ref[...], preferred_element_type=F32))
    emb = jnp.dot(p_ref[...].astype(BF16), wp_ref[...], preferred_element_type=F32)
    res = h_ref[:, pl.ds(pl.multiple_of(j * tn, tn), tn)]
    o_ref[...] = res + gate * emb


def _ple(h, g, w_gate, p, w_proj):
    T, D = h.shape
    P = p.shape[1]
    kern = functools.partial(_ple_kernel, tn=TN)
    return pl.pallas_call(
        kern,
        out_shape=jax.ShapeDtypeStruct((T, D), F32),
        grid=(T // TM, D // TN),
        in_specs=[
            pl.BlockSpec((TM, D), lambda i, j: (i, 0)),
            pl.BlockSpec((1, D), lambda i, j: (0, 0)),
            pl.BlockSpec((D, TN), lambda i, j: (0, j)),
            pl.BlockSpec((TM, P), lambda i, j: (i, 0)),
            pl.BlockSpec((P, TN), lambda i, j: (0, j)),
        ],
        out_specs=pl.BlockSpec((TM, TN), lambda i, j: (i, j)),
        scratch_shapes=[pltpu.VMEM((TM, D), BF16)],
        compiler_params=_params("parallel", "arbitrary"),
        name="ple",
    )(h, g, w_gate, p, w_proj)


def _dft_tables(n, scale):
    j = jnp.arange(n, dtype=jnp.int32)
    ang = ((j[:, None] * j[None, :]) % n).astype(F32) * (2.0 * np.pi / n)
    return (jnp.cos(ang) * scale).astype(BF16), (-jnp.sin(ang) * scale).astype(BF16)


def _chan_dft_kernel(x_ref, g_ref, c_ref, s_ref, u_ref, v_ref, *, gd):
    a = _rms(x_ref[...], g_ref[...]).astype(BF16)
    c = c_ref[...]
    s = s_ref[...]
    for grp in range(a.shape[1] // gd):
        sl = slice(grp * gd, (grp + 1) * gd)
        u_ref[:, sl] = jnp.dot(a[:, sl], c, preferred_element_type=F32).astype(BF16)
        v_ref[:, sl] = jnp.dot(a[:, sl], s, preferred_element_type=F32).astype(BF16)


def _chan_dft(h, g, c, s):
    T, D = h.shape
    gd = c.shape[0]
    kern = functools.partial(_chan_dft_kernel, gd=gd)
    row = pl.BlockSpec((TM, D), lambda i: (i, 0))
    return pl.pallas_call(
        kern,
        out_shape=(jax.ShapeDtypeStruct((T, D), BF16), jax.ShapeDtypeStruct((T, D), BF16)),
        grid=(T // TM,),
        in_specs=[row, pl.BlockSpec((1, D), lambda i: (0, 0)),
                  pl.BlockSpec((gd, gd), lambda i: (0, 0)), pl.BlockSpec((gd, gd), lambda i: (0, 0))],
        out_specs=(row, row),
        compiler_params=_params("parallel"),
        name="fourier_channel_dft",
    )(h, g, c, s)


def _seq_dft_kernel(c_ref, s_ref, u_ref, v_ref, o_ref, acc_ref):
    k = pl.program_id(3)

    @pl.when(k == 0)
    def _():
        acc_ref[...] = jnp.zeros_like(acc_ref)

    acc_ref[...] += (jnp.dot(c_ref[...], u_ref[...], preferred_element_type=F32)
                     + jnp.dot(s_ref[...], v_ref[...], preferred_element_type=F32))

    @pl.when(k == pl.num_programs(3) - 1)
    def _():
        o_ref[...] = acc_ref[...].astype(BF16)


def _seq_dft(u, v, c, s, row0, batch, seq):
    D = u.shape[1]
    tm, tn, tk = min(DFT_TM, seq), DFT_TN, DFT_TK
    rb0 = row0 // tk
    return pl.pallas_call(
        _seq_dft_kernel,
        out_shape=jax.ShapeDtypeStruct((batch * seq, D), BF16),
        grid=(batch, seq // tm, D // tn, seq // tk),
        in_specs=[
            pl.BlockSpec((tm, tk), lambda b, i, j, k: (i, k)),
            pl.BlockSpec((tm, tk), lambda b, i, j, k: (i, k)),
            pl.BlockSpec((tk, tn), lambda b, i, j, k: (rb0 + b * (seq // tk) + k, j)),
            pl.BlockSpec((tk, tn), lambda b, i, j, k: (rb0 + b * (seq // tk) + k, j)),
        ],
        out_specs=pl.BlockSpec((tm, tn), lambda b, i, j, k: (b * (seq // tm) + i, j)),
        scratch_shapes=[pltpu.VMEM((tm, tn), F32)],
        compiler_params=_params("parallel", "parallel", "parallel", "arbitrary"),
        name="fourier_seq_dft",
    )(c, s, u, v)


def _norm_kernel(x_ref, g_ref, o_ref):
    o_ref[...] = _rms(x_ref[...], g_ref[...])


def _final_norm(h, g, row0, n_rows):
    D = h.shape[1]
    return pl.pallas_call(
        _norm_kernel,
        out_shape=jax.ShapeDtypeStruct((n_rows, D), F32),
        grid=(n_rows // TM,),
        in_specs=[pl.BlockSpec((TM, D), lambda i: (row0 // TM + i, 0)),
                  pl.BlockSpec((1, D), lambda i: (0, 0))],
        out_specs=pl.BlockSpec((TM, D), lambda i: (i, 0)),
        compiler_params=_params("parallel"),
        name="final_norm",
    )(h, g)


def kernel(x_prompt, x_sample, p_prompt, p_sample, g_mix, w_qkv, g_q, g_k, w_attn_out, w_fourier_out, g_ffn, w_route_group, b_route_group, w_route_expert, b_route_expert, w_exp_gate, w_exp_up, w_exp_down, g_ple, w_ple_gate, w_ple_proj, g_final):
    depth = g_mix.shape[0]
    bp, sp, D = x_prompt.shape
    bs, ss, _ = x_sample.shape
    tp, ts = bp * sp, bs * ss
    groups = ((tp, sp), (ts, ss))
    gd = D // N_FOURIER_GROUPS

    h = jnp.concatenate([x_prompt.reshape(tp, D), x_sample.reshape(ts, D)], axis=0)
    n_mixers = 2
    for i in range(depth):
        jm = i // n_mixers
        g_i = g_mix[i][None, :]
        if i % n_mixers == 0:
            cos, sin = _rope_tables(max(sp, ss))
            qkv = _qkv_proj(h, g_i, w_qkv[jm].astype(BF16), g_q[jm][None, :], g_k[jm][None, :], cos, sin, groups)
            o = jnp.concatenate([_attention(qkv, 0, bp, sp), _attention(qkv, tp, bs, ss)], axis=0)
            h = _matmul_residual(o, w_attn_out[jm].astype(BF16), h)
        else:
            cc, sc = _dft_tables(gd, 1.0 / float(np.sqrt(gd)))
            u, v = _chan_dft(h, g_i, cc, sc)
            parts = []
            for row0, batch, seq in ((0, bp, sp), (tp, bs, ss)):
                cs, sn = _dft_tables(seq, 1.0 / float(np.sqrt(seq)))
                parts.append(_seq_dft(u, v, cs, -sn, row0, batch, seq))
            f = jnp.concatenate(parts, axis=0)
            h = _matmul_residual(f, w_fourier_out[jm].astype(BF16), h)
        h = _hier_moe(h, g_ffn[i][None, :], w_route_group[i], b_route_group[i], w_route_expert[i],
                      b_route_expert[i], w_exp_gate[i].astype(BF16), w_exp_up[i].astype(BF16),
                      w_exp_down[i].astype(BF16))
        p = jnp.concatenate([p_prompt[i].reshape(tp, -1), p_sample[i].reshape(ts, -1)], axis=0)
        h = _ple(h, g_ple[i][None, :], w_ple_gate[i].astype(BF16), p, w_ple_proj[i].astype(BF16))
    g_f = g_final[None, :]
    y_prompt = _final_norm(h, g_f, 0, tp).reshape(bp, sp, D)
    y_sample = _final_norm(h, g_f, tp, ts).reshape(bs, ss, D)
    return (y_prompt, y_sample)
```

```python
import functools

import jax
import jax.numpy as jnp
import numpy as np
from jax import lax
from jax.experimental import pallas as pl
from jax.experimental.pallas import tpu as pltpu

F32 = jnp.float32
BF16 = jnp.bfloat16
U32 = jnp.uint32

HEAD_DIM = 128
N_HEADS = 16
N_KV_HEADS = 8
Q_PER_KV = N_HEADS // N_KV_HEADS
ROPE_AXIS_DIM = HEAD_DIM // 2
ROPE_THETA = 10000.0
GRID_W = 64
N_FOURIER_GROUPS = 8
N_EXPERT_GROUPS = 4
EXPERTS_PER_GROUP = 8
N_EXPERTS = N_EXPERT_GROUPS * EXPERTS_PER_GROUP
TOP_K = 2
NORM_EPS = 1e-6

LANES = 128
SUBLANES = 8
MXU_DIM = 256
VMEM_LIMIT = 56 * 1024 * 1024

TM = 512
TQ = 256
TKV = 1024
EXPERT_BLOCK = 256
MOVE_ROWS = 256
DFT_TM, DFT_TN, DFT_TK = 1024, 1024, 512


def _params(*sem):
    return pltpu.CompilerParams(dimension_semantics=sem, vmem_limit_bytes=VMEM_LIMIT)


def _resident(shape):
    return pl.BlockSpec(shape, lambda *_: (0,) * len(shape), pipeline_mode=pl.Buffered(1))


def _rms(x, g):
    ms = jnp.mean(x * x, axis=-1, keepdims=True)
    return x * lax.rsqrt(ms + NORM_EPS) * g


def _pack_rows(x, ref, rows):
    half = SUBLANES * LANES
    for c in range(SUBLANES):
        lo = x[:, c * LANES:(c + 1) * LANES].astype(BF16).astype(F32)
        hi = x[:, half + c * LANES:half + (c + 1) * LANES].astype(BF16).astype(F32)
        word = (pltpu.bitcast(lo, U32) >> 16) | (pltpu.bitcast(hi, U32) & jnp.uint32(0xFFFF0000))
        ref[pl.ds(c, rows, stride=SUBLANES), :] = word


def _unpack_rows(ref, rows, c):
    word = ref[pl.ds(c, rows, stride=SUBLANES), :]
    lo = pltpu.bitcast(word << 16, F32)
    hi = pltpu.bitcast(word & jnp.uint32(0xFFFF0000), F32)
    return lo, hi


def _qkv_kernel(x_ref, g_ref, w_ref, gq_ref, gk_ref, cos_ref, sin_ref, o_ref, *, d_q, d_qk):
    a = _rms(x_ref[...], g_ref[...]).astype(BF16)
    cos = cos_ref[...]
    sin = sin_ref[...]
    q_scale = float(np.log2(np.e) / np.sqrt(HEAD_DIM))
    tabs = {
        "q": (gq_ref[0:1, :] * cos * q_scale, gq_ref[1:2, :] * sin * q_scale),
        "k": (gk_ref[0:1, :] * cos, gk_ref[1:2, :] * sin),
    }
    n_out = o_ref.shape[1]
    for c0 in range(0, n_out, MXU_DIM):
        acc = jnp.dot(a, w_ref[:, c0:c0 + MXU_DIM], preferred_element_type=F32)
        if c0 >= d_qk:
            o_ref[:, c0:c0 + MXU_DIM] = acc.astype(BF16)
            continue
        t1, t2 = tabs["q" if c0 < d_q else "k"]
        for hh in range(MXU_DIM // HEAD_DIM):
            y = acc[:, hh * HEAD_DIM:(hh + 1) * HEAD_DIM]
            r = lax.rsqrt(jnp.mean(y * y, axis=-1, keepdims=True) + NORM_EPS)
            rot = (y * t1 + pltpu.roll(y, HEAD_DIM // 2, 1) * t2) * r
            o_ref[:, c0 + hh * HEAD_DIM:c0 + (hh + 1) * HEAD_DIM] = rot.astype(BF16)


_HEAD_PERM = np.concatenate([np.arange(0, 32), np.arange(64, 96), np.arange(32, 64), np.arange(96, 128)])


def _rope_tables(n_pos):
    pos = jnp.arange(n_pos, dtype=jnp.int32)
    row = (pos // GRID_W).astype(F32)
    col = (pos % GRID_W).astype(F32)
    inv_freq = ROPE_THETA ** (-jnp.arange(0, ROPE_AXIS_DIM, 2, dtype=F32) / ROPE_AXIS_DIM)
    ang_r = row[:, None] * inv_freq[None, :]
    ang_c = col[:, None] * inv_freq[None, :]
    cos = jnp.concatenate([jnp.cos(ang_r), jnp.cos(ang_c), jnp.cos(ang_r), jnp.cos(ang_c)], axis=-1)
    sin = jnp.concatenate([-jnp.sin(ang_r), -jnp.sin(ang_c), jnp.sin(ang_r), jnp.sin(ang_c)], axis=-1)
    return cos, sin


def _qkv_proj(h, g, w_qkv, g_q, g_k, groups):
    T, D = h.shape
    N = w_qkv.shape[1]
    (t0, s0), (t1, s1) = groups
    assert s0 % TM == 0 and s1 % TM == 0 and T % TM == 0
    d_q = N_HEADS * HEAD_DIM
    d_qk = d_q + N_KV_HEADS * HEAD_DIM
    n_qk = d_qk // HEAD_DIM
    wqk = w_qkv[:, :d_qk].reshape(D, n_qk, HEAD_DIM)[:, :, _HEAD_PERM].reshape(D, d_qk)
    w = jnp.concatenate([wqk, w_qkv[:, d_qk:]], axis=1).astype(BF16)
    gq = g_q[_HEAD_PERM]
    gk = g_k[_HEAD_PERM]
    gq2 = jnp.stack([gq, jnp.roll(gq, HEAD_DIM // 2)])
    gk2 = jnp.stack([gk, jnp.roll(gk, HEAD_DIM // 2)])
    cos, sin = _rope_tables(max(s0, s1))

    def pos_map(i):
        r = i * TM
        p = jnp.where(r < t0, r % s0, (r - t0) % s1)
        return (p // TM, 0)

    kern = functools.partial(_qkv_kernel, d_q=d_q, d_qk=d_qk)
    return pl.pallas_call(
        kern,
        out_shape=jax.ShapeDtypeStruct((T, N), BF16),
        grid=(T // TM,),
        in_specs=[
            pl.BlockSpec((TM, D), lambda i: (i, 0)),
            _resident((1, D)),
            _resident((D, N)),
            _resident((2, HEAD_DIM)),
            _resident((2, HEAD_DIM)),
            pl.BlockSpec((TM, HEAD_DIM), pos_map),
            pl.BlockSpec((TM, HEAD_DIM), pos_map),
        ],
        out_specs=pl.BlockSpec((TM, N), lambda i: (i, 0)),
        compiler_params=_params("parallel"),
        name="qkv_proj",
    )(h, g, w, gq2, gk2, cos, sin)


def _attn_kernel(q_ref, k_ref, v_ref, o_ref, *, tq, tk, seq):
    q = q_ref[...]
    q2 = jnp.concatenate([q[:, :HEAD_DIM], q[:, HEAD_DIM:]], axis=0)
    m = jnp.full((2 * tq, 1), -jnp.inf, F32)
    l = jnp.zeros((2 * tq, 1), F32)
    acc = jnp.zeros((2 * tq, HEAD_DIM), F32)
    for c in range(seq // tk):
        k = k_ref[c * tk:(c + 1) * tk, :]
        v = v_ref[c * tk:(c + 1) * tk, :]
        s = lax.dot_general(q2, k, (((1,), (1,)), ((), ())), preferred_element_type=F32)
        m_new = jnp.maximum(m, jnp.max(s, axis=-1, keepdims=True))
        alpha = jnp.exp2(m - m_new)
        p = jnp.exp2(s - m_new)
        l = alpha * l + jnp.sum(p, axis=-1, keepdims=True)
        acc = alpha * acc + jnp.dot(p.astype(BF16), v, preferred_element_type=F32)
        m = m_new
    o = acc / l
    o_ref[:, :HEAD_DIM] = o[:tq].astype(BF16)
    o_ref[:, HEAD_DIM:] = o[tq:].astype(BF16)


def _attention(qkv, row0, batch, seq):
    assert row0 % seq == 0 and seq % TQ == 0 and seq % TKV == 0
    qw = Q_PER_KV * HEAD_DIM
    kern = functools.partial(_attn_kernel, tq=TQ, tk=TKV, seq=seq)
    q_blk0 = row0 // TQ
    s_blk0 = row0 // seq
    k_col0 = N_HEADS
    v_col0 = N_HEADS + N_KV_HEADS
    return pl.pallas_call(
        kern,
        out_shape=jax.ShapeDtypeStruct((batch * seq, N_HEADS * HEAD_DIM), BF16),
        grid=(batch, N_KV_HEADS, seq // TQ),
        in_specs=[
            pl.BlockSpec((TQ, qw), lambda b, h, i: (q_blk0 + b * (seq // TQ) + i, h)),
            pl.BlockSpec((seq, HEAD_DIM), lambda b, h, i: (s_blk0 + b, k_col0 + h)),
            pl.BlockSpec((seq, HEAD_DIM), lambda b, h, i: (s_blk0 + b, v_col0 + h)),
        ],
        out_specs=pl.BlockSpec((TQ, qw), lambda b, h, i: (b * (seq // TQ) + i, h)),
        compiler_params=_params("parallel", "parallel", "arbitrary"),
        name="attention",
    )(qkv, qkv, qkv)


def _mm_res_kernel(x_ref, w_ref, r_ref, o_ref):
    o_ref[...] = r_ref[...] + jnp.dot(x_ref[...], w_ref[...], preferred_element_type=F32)


def _matmul_residual(x, w, res):
    T, K = x.shape
    N = w.shape[1]
    return pl.pallas_call(
        _mm_res_kernel,
        out_shape=jax.ShapeDtypeStruct((T, N), F32),
        grid=(T // TM,),
        in_specs=[
            pl.BlockSpec((TM, K), lambda i: (i, 0)),
            _resident((K, N)),
            pl.BlockSpec((TM, N), lambda i: (i, 0)),
        ],
        out_specs=pl.BlockSpec((TM, N), lambda i: (i, 0)),
        compiler_params=_params("parallel"),
        name="matmul_residual",
    )(x, w, res)


R_E1, R_E2, R_W1, R_W2, R_RANK1, R_RANK2 = range(6)
ROUTE_LOGIT_E0 = N_EXPERT_GROUPS


def _route_kernel(x_ref, g_ref, whi_ref, wlo_ref, b_ref, tri_ref, r_ref, cnt_ref, a_ref, carry_ref):
    @pl.when(pl.program_id(0) == 0)
    def _():
        carry_ref[...] = jnp.zeros_like(carry_ref)

    a = _rms(x_ref[...], g_ref[...])
    _pack_rows(a, a_ref, a.shape[0])
    a_hi = a.astype(BF16)
    a_lo = (a - a_hi.astype(F32)).astype(BF16)
    w_hi = whi_ref[...]
    logits = (jnp.dot(a_hi, w_hi, preferred_element_type=F32)
              + (jnp.dot(a_lo, w_hi, preferred_element_type=F32)
                 + jnp.dot(a_hi, wlo_ref[...], preferred_element_type=F32))) + b_ref[...]
    lane = lax.broadcasted_iota(jnp.int32, logits.shape, 1)
    neg = -jnp.inf
    far = LANES

    def first_max(vals):
        top = jnp.max(vals, axis=-1, keepdims=True)
        idx = jnp.min(jnp.where(vals == top, lane, far), axis=-1, keepdims=True)
        return top, idx

    gmask = lane < N_EXPERT_GROUPS
    gtop, g_sel = first_max(jnp.where(gmask, logits, neg))
    pg = 1.0 / jnp.sum(jnp.where(gmask, jnp.exp(logits - gtop), 0.0), axis=-1, keepdims=True)
    lo = ROUTE_LOGIT_E0 + g_sel * EXPERTS_PER_GROUP
    le = jnp.where((lane >= lo) & (lane < lo + EXPERTS_PER_GROUP), logits, neg)
    t1, i1 = first_max(le)
    t2, i2 = first_max(jnp.where(lane == i1, neg, le))
    r21 = jnp.exp(t2 - t1)
    w1 = pg / (1.0 + r21)
    w2 = pg * r21 / (1.0 + r21)
    e1 = i1 - ROUTE_LOGIT_E0
    e2 = i2 - ROUTE_LOGIT_E0

    hit1 = lane == e1
    hit2 = lane == e2
    onehot = (hit1 | hit2).astype(F32)
    before = jnp.dot(tri_ref[...], onehot.astype(BF16), preferred_element_type=F32) + carry_ref[0:1, :]
    rank1 = jnp.sum(jnp.where(hit1, before, 0.0), axis=-1, keepdims=True)
    rank2 = jnp.sum(jnp.where(hit2, before, 0.0), axis=-1, keepdims=True)
    carry_ref[...] = carry_ref[...] + jnp.sum(onehot, axis=0, keepdims=True)
    cnt_ref[...] = carry_ref[...]

    rec = jnp.zeros(logits.shape, F32)
    for pos, val in ((R_E1, e1.astype(F32)), (R_E2, e2.astype(F32)), (R_W1, w1), (R_W2, w2),
                     (R_RANK1, rank1), (R_RANK2, rank2)):
        rec = jnp.where(lane == pos, val, rec)
    r_ref[...] = rec


def _route(h, g, w_r, b_r):
    T, D = h.shape
    assert D == 2 * SUBLANES * LANES
    tri = (jnp.arange(TM)[:, None] > jnp.arange(TM)[None, :]).astype(BF16)
    w_hi = w_r.astype(BF16)
    w_lo = (w_r - w_hi.astype(F32)).astype(BF16)
    return pl.pallas_call(
        _route_kernel,
        out_shape=(jax.ShapeDtypeStruct((T, LANES), F32), jax.ShapeDtypeStruct((SUBLANES, LANES), F32),
                   jax.ShapeDtypeStruct((T * SUBLANES, LANES), U32)),
        grid=(T // TM,),
        in_specs=[
            pl.BlockSpec((TM, D), lambda i: (i, 0)),
            _resident((1, D)),
            _resident((D, LANES)),
            _resident((D, LANES)),
            _resident((1, LANES)),
            _resident((TM, TM)),
        ],
        out_specs=(pl.BlockSpec((TM, LANES), lambda i: (i, 0)),
                   pl.BlockSpec((SUBLANES, LANES), lambda i: (0, 0)),
                   pl.BlockSpec((TM * SUBLANES, LANES), lambda i: (i, 0))),
        scratch_shapes=[pltpu.VMEM((SUBLANES, LANES), F32)],
        compiler_params=_params("arbitrary"),
        name="moe_route",
    )(h, g, w_hi, w_lo, b_r, tri)


def _tile_copy(src, src_row, dst, dst_row, sem):
    return pltpu.make_async_copy(src.at[pl.ds(pl.multiple_of(src_row * SUBLANES, SUBLANES), SUBLANES), :],
                                 dst.at[pl.ds(pl.multiple_of(dst_row * SUBLANES, SUBLANES), SUBLANES), :], sem)


def _dispatch_kernel(dest_ref, fill_ref, a_ref, xb_hbm, zero_ref, sem, *, rows, n_blocks):
    i = pl.program_id(0)
    base = i * rows

    def start(r, c):
        for k in range(TOP_K):
            _tile_copy(a_ref, r, xb_hbm, dest_ref[TOP_K * (base + r) + k], sem).start()
        return c

    def wait(r, c):
        for k in range(TOP_K):
            _tile_copy(a_ref, r, xb_hbm, 0, sem).wait()
        return c

    lax.fori_loop(0, rows, start, 0, unroll=8)
    lax.fori_loop(0, rows, wait, 0, unroll=8)

    @pl.when(i == pl.num_programs(0) - 1)
    def _():
        zero_ref[...] = jnp.zeros_like(zero_ref)

        def per_expert(e, c):
            lo = fill_ref[2 * e]
            hi = fill_ref[2 * e + 1]

            def zstart(r, cc):
                _tile_copy(zero_ref, 0, xb_hbm, r, sem).start()
                return cc

            def zwait(r, cc):
                _tile_copy(zero_ref, 0, xb_hbm, r, sem).wait()
                return cc

            lax.fori_loop(lo, hi, zstart, 0)
            lax.fori_loop(lo, hi, zwait, 0)
            return c

        lax.fori_loop(0, N_EXPERTS, per_expert, 0)

        blk = zero_ref.shape[0]

        def block_copy(b):
            return pltpu.make_async_copy(zero_ref, xb_hbm.at[pl.ds(pl.multiple_of(b * blk, blk), blk), :], sem)

        def bstart(b, c):
            block_copy(b).start()
            return c

        def bwait(b, c):
            block_copy(b).wait()
            return c

        first_unused = fill_ref[2 * N_EXPERTS]
        lax.fori_loop(first_unused, n_blocks, bstart, 0)
        lax.fori_loop(first_unused, n_blocks, bwait, 0)


def _dispatch(a_slab, dest, fill, n_blocks, bm):
    T = a_slab.shape[0] // SUBLANES
    rows = MOVE_ROWS
    kern = functools.partial(_dispatch_kernel, rows=rows, n_blocks=n_blocks)
    return pl.pallas_call(
        kern,
        out_shape=jax.ShapeDtypeStruct((n_blocks * bm * SUBLANES, LANES), U32),
        grid_spec=pltpu.PrefetchScalarGridSpec(
            num_scalar_prefetch=2,
            grid=(T // rows,),
            in_specs=[pl.BlockSpec((rows * SUBLANES, LANES), lambda i, d, f: (i, 0))],
            out_specs=pl.BlockSpec(memory_space=pl.ANY),
            scratch_shapes=[pltpu.VMEM((bm * SUBLANES, LANES), U32), pltpu.SemaphoreType.DMA(())],
        ),
        compiler_params=_params("arbitrary"),
        name="moe_dispatch",
    )(dest, fill, a_slab)


def _expert_kernel(be_ref, nu_ref, x_ref, wg_ref, wu_ref, wd_ref, o_ref, x_s, wg_s, wu_s, wd_s, *, bm):
    b = pl.program_id(0)
    prev = be_ref[jnp.maximum(b - 1, 0)]

    @pl.when((b == 0) | (be_ref[b] != prev))
    def _():
        wg_s[...] = wg_ref[0].astype(BF16)
        wu_s[...] = wu_ref[0].astype(BF16)
        wd_s[...] = wd_ref[0].astype(BF16)

    @pl.when(b < nu_ref[0])
    def _():
        half = SUBLANES * LANES
        for c in range(SUBLANES):
            lo, hi = _unpack_rows(x_ref, bm, c)
            x_s[:, c * LANES:(c + 1) * LANES] = lo.astype(BF16)
            x_s[:, half + c * LANES:half + (c + 1) * LANES] = hi.astype(BF16)
        a = x_s[...]
        hg = jnp.dot(a, wg_s[...], preferred_element_type=F32)
        hu = jnp.dot(a, wu_s[...], preferred_element_type=F32)
        h = hg * jax.nn.sigmoid(hg) * hu
        y = jnp.dot(h.astype(BF16), wd_s[...], preferred_element_type=F32)
        _pack_rows(y, o_ref, bm)

    @pl.when(b >= nu_ref[0])
    def _():
        o_ref[...] = jnp.zeros_like(o_ref)


def _experts(xb, w_gate, w_up, w_down, blk_e, n_used):
    bm = EXPERT_BLOCK
    n_blocks = xb.shape[0] // (bm * SUBLANES)
    _, D, De = w_gate.shape
    kern = functools.partial(_expert_kernel, bm=bm)
    return pl.pallas_call(
        kern,
        out_shape=jax.ShapeDtypeStruct(xb.shape, U32),
        grid_spec=pltpu.PrefetchScalarGridSpec(
            num_scalar_prefetch=2,
            grid=(n_blocks,),
            in_specs=[
                pl.BlockSpec((bm * SUBLANES, LANES), lambda b, be, nu: (jnp.minimum(b, nu[0] - 1), 0)),
                pl.BlockSpec((1, D, De), lambda b, be, nu: (be[b], 0, 0)),
                pl.BlockSpec((1, D, De), lambda b, be, nu: (be[b], 0, 0)),
                pl.BlockSpec((1, De, D), lambda b, be, nu: (be[b], 0, 0)),
            ],
            out_specs=pl.BlockSpec((bm * SUBLANES, LANES), lambda b, be, nu: (b, 0)),
            scratch_shapes=[pltpu.VMEM((bm, D), BF16), pltpu.VMEM((D, De), BF16),
                            pltpu.VMEM((D, De), BF16), pltpu.VMEM((De, D), BF16)],
        ),
        compiler_params=_params("arbitrary"),
        name="moe_experts",
    )(blk_e, n_used, xb, w_gate, w_up, w_down)


def _combine_kernel(dest_ref, h_ref, r_ref, yb_hbm, o_ref, y1_ref, y2_ref, sem, *, rows):
    base = pl.program_id(0) * rows
    bufs = (y1_ref, y2_ref)

    def start(r, c):
        for k in range(TOP_K):
            _tile_copy(yb_hbm, dest_ref[TOP_K * (base + r) + k], bufs[k], r, sem).start()
        return c

    def wait(r, c):
        for k in range(TOP_K):
            _tile_copy(yb_hbm, 0, bufs[k], r, sem).wait()
        return c

    lax.fori_loop(0, rows, start, 0, unroll=8)
    lax.fori_loop(0, rows, wait, 0, unroll=8)
    rec = r_ref[...]
    w1 = rec[:, R_W1:R_W1 + 1]
    w2 = rec[:, R_W2:R_W2 + 1]
    half = SUBLANES * LANES
    for c in range(SUBLANES):
        lo1, hi1 = _unpack_rows(y1_ref, rows, c)
        lo2, hi2 = _unpack_rows(y2_ref, rows, c)
        sl = slice(c * LANES, (c + 1) * LANES)
        sh = slice(half + c * LANES, half + (c + 1) * LANES)
        o_ref[:, sl] = h_ref[:, sl] + (w1 * lo1 + w2 * lo2)
        o_ref[:, sh] = h_ref[:, sh] + (w1 * hi1 + w2 * hi2)


def _combine(h, rec, yb, dest):
    T, D = h.shape
    rows = MOVE_ROWS
    kern = functools.partial(_combine_kernel, rows=rows)
    return pl.pallas_call(
        kern,
        out_shape=jax.ShapeDtypeStruct((T, D), F32),
        grid_spec=pltpu.PrefetchScalarGridSpec(
            num_scalar_prefetch=1,
            grid=(T // rows,),
            in_specs=[
                pl.BlockSpec((rows, D), lambda i, d: (i, 0)),
                pl.BlockSpec((rows, LANES), lambda i, d: (i, 0)),
                pl.BlockSpec(memory_space=pl.ANY),
            ],
            out_specs=pl.BlockSpec((rows, D), lambda i, d: (i, 0)),
            scratch_shapes=[pltpu.VMEM((rows * SUBLANES, LANES), U32), pltpu.VMEM((rows * SUBLANES, LANES), U32),
                            pltpu.SemaphoreType.DMA(())],
        ),
        compiler_params=_params("arbitrary"),
        name="moe_combine",
    )(dest, h, rec, yb)


def _hier_moe(h, g, w_rg, b_rg, w_re, b_re, w_gate, w_up, w_down):
    T, D = h.shape
    pad = LANES - N_EXPERT_GROUPS - N_EXPERTS
    w_r = jnp.concatenate([w_rg, w_re, jnp.zeros((D, pad), F32)], axis=1)
    b_r = jnp.concatenate([b_rg, b_re, jnp.zeros((pad,), F32)])[None, :]
    rec, cnt, a_slab = _route(h, g, w_r, b_r)

    bm = EXPERT_BLOCK
    e_tok = rec[:, R_E1:R_E2 + 1].astype(jnp.int32)
    rank = rec[:, R_RANK1:R_RANK2 + 1].astype(jnp.int32)
    counts = cnt[0, :N_EXPERTS].astype(jnp.int32)
    pad_counts = (counts + bm - 1) // bm * bm
    pad_end = jnp.cumsum(pad_counts)
    pad_start = pad_end - pad_counts
    dest = (pad_start[e_tok] + rank).reshape(-1)
    n_blocks = (T * TOP_K) // bm + N_EXPERTS
    blk_start = jnp.arange(n_blocks, dtype=jnp.int32) * bm
    blk_e = jnp.minimum(jnp.sum(blk_start[:, None] >= pad_end[None, :], axis=1), N_EXPERTS - 1).astype(jnp.int32)
    n_used = (pad_end[-1:] // bm).astype(jnp.int32)
    fill = jnp.concatenate([jnp.stack([pad_start + counts, pad_end], axis=1).reshape(-1), n_used]).astype(jnp.int32)

    xb = _dispatch(a_slab, dest, fill, n_blocks, bm)
    yb = _experts(xb, w_gate, w_up, w_down, blk_e, n_used)
    return _combine(h, rec, yb, dest)


def _ple_math(h_ref, g_ref, wg_ref, p_ref, wp_ref):
    h = h_ref[...]
    a = _rms(h, g_ref[...]).astype(BF16)
    gate = jax.nn.sigmoid(jnp.dot(a, wg_ref[...], preferred_element_type=F32))
    emb = jnp.dot(p_ref[...].astype(BF16), wp_ref[...], preferred_element_type=F32)
    return h + gate * emb


def _ple_kernel(h_ref, g_ref, wg_ref, p_ref, wp_ref, o_ref):
    o_ref[...] = _ple_math(h_ref, g_ref, wg_ref, p_ref, wp_ref)


def _ple_final_kernel(h_ref, g_ref, wg_ref, p_ref, wp_ref, gf_ref, o0_ref, o1_ref, *, n0):
    out = _rms(_ple_math(h_ref, g_ref, wg_ref, p_ref, wp_ref), gf_ref[...])
    i = pl.program_id(0)

    @pl.when(i < n0)
    def _():
        o0_ref[...] = out

    @pl.when(i >= n0)
    def _():
        o1_ref[...] = out


def _ple(h, g, w_gate, p, w_proj, final=None):
    T, D = h.shape
    P = p.shape[1]
    in_specs = [
        pl.BlockSpec((TM, D), lambda i: (i, 0)),
        _resident((1, D)),
        _resident((D, D)),
        pl.BlockSpec((TM, P), lambda i: (i, 0)),
        _resident((P, D)),
    ]
    if final is None:
        return pl.pallas_call(
            _ple_kernel,
            out_shape=jax.ShapeDtypeStruct((T, D), F32),
            grid=(T // TM,),
            in_specs=in_specs,
            out_specs=pl.BlockSpec((TM, D), lambda i: (i, 0)),
            compiler_params=_params("parallel"),
            name="ple",
        )(h, g, w_gate, p, w_proj)
    g_final, t0 = final
    assert t0 % TM == 0
    n0 = t0 // TM
    kern = functools.partial(_ple_final_kernel, n0=n0)
    return pl.pallas_call(
        kern,
        out_shape=(jax.ShapeDtypeStruct((t0, D), F32), jax.ShapeDtypeStruct((T - t0, D), F32)),
        grid=(T // TM,),
        in_specs=in_specs + [_resident((1, D))],
        out_specs=(pl.BlockSpec((TM, D), lambda i: (jnp.minimum(i, n0 - 1), 0)),
                   pl.BlockSpec((TM, D), lambda i: (jnp.maximum(i - n0, 0), 0))),
        compiler_params=_params("arbitrary"),
        name="ple_final",
    )(h, g, w_gate, p, w_proj, g_final)


def _dft_tables(n, scale):
    j = jnp.arange(n, dtype=jnp.int32)
    ang = ((j[:, None] * j[None, :]) % n).astype(F32) * (2.0 * np.pi / n)
    return (jnp.cos(ang) * scale).astype(BF16), (-jnp.sin(ang) * scale).astype(BF16)


def _chan_dft_kernel(x_ref, g_ref, c_ref, s_ref, u_ref, v_ref, *, gd):
    a = _rms(x_ref[...], g_ref[...]).astype(BF16)
    c = c_ref[...]
    s = s_ref[...]
    for grp in range(a.shape[1] // gd):
        sl = slice(grp * gd, (grp + 1) * gd)
        u_ref[:, sl] = jnp.dot(a[:, sl], c, preferred_element_type=F32).astype(BF16)
        v_ref[:, sl] = jnp.dot(a[:, sl], s, preferred_element_type=F32).astype(BF16)


def _chan_dft(h, g, c, s):
    T, D = h.shape
    gd = c.shape[0]
    kern = functools.partial(_chan_dft_kernel, gd=gd)
    row = pl.BlockSpec((TM, D), lambda i: (i, 0))
    return pl.pallas_call(
        kern,
        out_shape=(jax.ShapeDtypeStruct((T, D), BF16), jax.ShapeDtypeStruct((T, D), BF16)),
        grid=(T // TM,),
        in_specs=[row, _resident((1, D)), _resident((gd, gd)), _resident((gd, gd))],
        out_specs=(row, row),
        compiler_params=_params("parallel"),
        name="fourier_channel_dft",
    )(h, g, c, s)


def _seq_dft_kernel(c_ref, s_ref, u_ref, v_ref, o_ref, acc_ref):
    k = pl.program_id(3)

    @pl.when(k == 0)
    def _():
        acc_ref[...] = jnp.zeros_like(acc_ref)

    acc_ref[...] += (jnp.dot(c_ref[...], u_ref[...], preferred_element_type=F32)
                     + jnp.dot(s_ref[...], v_ref[...], preferred_element_type=F32))

    @pl.when(k == pl.num_programs(3) - 1)
    def _():
        o_ref[...] = acc_ref[...].astype(BF16)


def _seq_dft(u, v, c, s, row0, batch, seq):
    D = u.shape[1]
    tm, tn, tk = min(DFT_TM, seq), DFT_TN, DFT_TK
    rb0 = row0 // tk
    return pl.pallas_call(
        _seq_dft_kernel,
        out_shape=jax.ShapeDtypeStruct((batch * seq, D), BF16),
        grid=(batch, seq // tm, D // tn, seq // tk),
        in_specs=[
            pl.BlockSpec((tm, tk), lambda b, i, j, k: (i, k)),
            pl.BlockSpec((tm, tk), lambda b, i, j, k: (i, k)),
            pl.BlockSpec((tk, tn), lambda b, i, j, k: (rb0 + b * (seq // tk) + k, j)),
            pl.BlockSpec((tk, tn), lambda b, i, j, k: (rb0 + b * (seq // tk) + k, j)),
        ],
        out_specs=pl.BlockSpec((tm, tn), lambda b, i, j, k: (b * (seq // tm) + i, j)),
        scratch_shapes=[pltpu.VMEM((tm, tn), F32)],
        compiler_params=_params("parallel", "parallel", "parallel", "arbitrary"),
        name="fourier_seq_dft",
    )(c, s, u, v)


def kernel(x_prompt, x_sample, p_prompt, p_sample, g_mix, w_qkv, g_q, g_k, w_attn_out, w_fourier_out, g_ffn, w_route_group, b_route_group, w_route_expert, b_route_expert, w_exp_gate, w_exp_up, w_exp_down, g_ple, w_ple_gate, w_ple_proj, g_final):
    depth = g_mix.shape[0]
    bp, sp, D = x_prompt.shape
    bs, ss, _ = x_sample.shape
    tp, ts = bp * sp, bs * ss
    groups = ((tp, sp), (ts, ss))
    gd = D // N_FOURIER_GROUPS

    h = jnp.concatenate([x_prompt.reshape(tp, D), x_sample.reshape(ts, D)], axis=0)
    n_mixers = 2
    for i in range(depth):
        jm = i // n_mixers
        g_i = g_mix[i][None, :]
        if i % n_mixers == 0:
            qkv = _qkv_proj(h, g_i, w_qkv[jm], g_q[jm], g_k[jm], groups)
            o = jnp.concatenate([_attention(qkv, 0, bp, sp), _attention(qkv, tp, bs, ss)], axis=0)
            h = _matmul_residual(o, w_attn_out[jm].astype(BF16), h)
        else:
            cc, sc = _dft_tables(gd, 1.0 / float(np.sqrt(gd)))
            u, v = _chan_dft(h, g_i, cc, sc)
            parts = []
            for row0, batch, seq in ((tp, bs, ss), (0, bp, sp)):
                cs, sn = _dft_tables(seq, 1.0 / float(np.sqrt(seq)))
                parts.append(_seq_dft(u, v, cs, -sn, row0, batch, seq))
            f = jnp.concatenate(parts[::-1], axis=0)
            h = _matmul_residual(f, w_fourier_out[jm].astype(BF16), h)
        h = _hier_moe(h, g_ffn[i][None, :], w_route_group[i], b_route_group[i], w_route_expert[i],
                      b_route_expert[i], w_exp_gate[i], w_exp_up[i], w_exp_down[i])
        p = jnp.concatenate([p_prompt[i].reshape(tp, -1), p_sample[i].reshape(ts, -1)], axis=0)
        h = _ple(h, g_ple[i][None, :], w_ple_gate[i].astype(BF16), p, w_ple_proj[i].astype(BF16),
                 (g_final[None, :], tp) if i == depth - 1 else None)
    y_prompt, y_sample = h
    return (y_prompt.reshape(bp, sp, D), y_sample.reshape(bs, ss, D))
```

```python
import functools

import jax
import jax.numpy as jnp
import numpy as np
from jax import lax
from jax.experimental import pallas as pl
from jax.experimental.pallas import tpu as pltpu

F32 = jnp.float32
BF16 = jnp.bfloat16
U32 = jnp.uint32

HEAD_DIM = 128
N_HEADS = 16
N_KV_HEADS = 8
Q_PER_KV = N_HEADS // N_KV_HEADS
ROPE_AXIS_DIM = HEAD_DIM // 2
ROPE_THETA = 10000.0
GRID_W = 64
N_FOURIER_GROUPS = 8
N_EXPERT_GROUPS = 4
EXPERTS_PER_GROUP = 8
N_EXPERTS = N_EXPERT_GROUPS * EXPERTS_PER_GROUP
TOP_K = 2
NORM_EPS = 1e-6

LANES = 128
SUBLANES = 8
MXU_DIM = 256
VMEM_LIMIT = 56 * 1024 * 1024

TM = 512
TQ = 256
TKV = 1024
EXPERT_BLOCK = 256
MOVE_ROWS = 256
DFT_TM, DFT_TN, DFT_TK = 1024, 1024, 512


def _params(*sem):
    return pltpu.CompilerParams(dimension_semantics=sem, vmem_limit_bytes=VMEM_LIMIT)


def _resident(shape):
    return pl.BlockSpec(shape, lambda *_: (0,) * len(shape), pipeline_mode=pl.Buffered(1))


def _rms(x, g):
    ms = jnp.mean(x * x, axis=-1, keepdims=True)
    return x * lax.rsqrt(ms + NORM_EPS) * g


def _pack_rows(x, ref, rows):
    half = SUBLANES * LANES
    for c in range(SUBLANES):
        lo = x[:, c * LANES:(c + 1) * LANES].astype(BF16).astype(F32)
        hi = x[:, half + c * LANES:half + (c + 1) * LANES].astype(BF16).astype(F32)
        word = (pltpu.bitcast(lo, U32) >> 16) | (pltpu.bitcast(hi, U32) & jnp.uint32(0xFFFF0000))
        ref[pl.ds(c, rows, stride=SUBLANES), :] = word


def _unpack_rows(ref, rows, c):
    word = ref[pl.ds(c, rows, stride=SUBLANES), :]
    lo = pltpu.bitcast(word << 16, F32)
    hi = pltpu.bitcast(word & jnp.uint32(0xFFFF0000), F32)
    return lo, hi


def _qkv_kernel(x_ref, g_ref, w_ref, gq_ref, gk_ref, cos_ref, sin_ref, o_ref, *, d_q, d_qk):
    a = _rms(x_ref[...], g_ref[...]).astype(BF16)
    cos = cos_ref[...]
    sin = sin_ref[...]
    q_scale = float(np.log2(np.e) / np.sqrt(HEAD_DIM))
    tabs = {
        "q": (gq_ref[0:1, :] * cos * q_scale, gq_ref[1:2, :] * sin * q_scale),
        "k": (gk_ref[0:1, :] * cos, gk_ref[1:2, :] * sin),
    }
    n_out = o_ref.shape[1]
    for c0 in range(0, n_out, MXU_DIM):
        acc = jnp.dot(a, w_ref[:, c0:c0 + MXU_DIM], preferred_element_type=F32)
        if c0 >= d_qk:
            o_ref[:, c0:c0 + MXU_DIM] = acc.astype(BF16)
            continue
        t1, t2 = tabs["q" if c0 < d_q else "k"]
        for hh in range(MXU_DIM // HEAD_DIM):
            y = acc[:, hh * HEAD_DIM:(hh + 1) * HEAD_DIM]
            r = lax.rsqrt(jnp.mean(y * y, axis=-1, keepdims=True) + NORM_EPS)
            rot = (y * t1 + pltpu.roll(y, HEAD_DIM // 2, 1) * t2) * r
            o_ref[:, c0 + hh * HEAD_DIM:c0 + (hh + 1) * HEAD_DIM] = rot.astype(BF16)


def _permute_heads(x):
    lead = x.shape[:-1]
    x = x.reshape(lead + (-1, 2, 2, HEAD_DIM // 4))
    return jnp.swapaxes(x, -3, -2).reshape(lead + (-1,))


def _rope_tables(n_pos):
    pos = jnp.arange(n_pos, dtype=jnp.int32)
    row = (pos // GRID_W).astype(F32)
    col = (pos % GRID_W).astype(F32)
    inv_freq = ROPE_THETA ** (-jnp.arange(0, ROPE_AXIS_DIM, 2, dtype=F32) / ROPE_AXIS_DIM)
    ang_r = row[:, None] * inv_freq[None, :]
    ang_c = col[:, None] * inv_freq[None, :]
    cos = jnp.concatenate([jnp.cos(ang_r), jnp.cos(ang_c), jnp.cos(ang_r), jnp.cos(ang_c)], axis=-1)
    sin = jnp.concatenate([-jnp.sin(ang_r), -jnp.sin(ang_c), jnp.sin(ang_r), jnp.sin(ang_c)], axis=-1)
    return cos, sin


def _qkv_weights(w_qkv, g_q, g_k):
    d_qk = (N_HEADS + N_KV_HEADS) * HEAD_DIM
    w = jnp.concatenate([_permute_heads(w_qkv[:, :d_qk]).astype(BF16), w_qkv[:, d_qk:].astype(BF16)], axis=1)
    gq = _permute_heads(g_q)
    gk = _permute_heads(g_k)
    gq2 = jnp.stack([gq, jnp.roll(gq, HEAD_DIM // 2)])
    gk2 = jnp.stack([gk, jnp.roll(gk, HEAD_DIM // 2)])
    return w, gq2, gk2


def _qkv_proj(h, g, w, gq2, gk2, cos, sin, seq):
    T, D = h.shape
    N = w.shape[1]
    assert seq % TM == 0 and T % seq == 0
    d_q = N_HEADS * HEAD_DIM
    d_qk = d_q + N_KV_HEADS * HEAD_DIM

    def pos_map(i):
        return (i % (seq // TM), 0)

    kern = functools.partial(_qkv_kernel, d_q=d_q, d_qk=d_qk)
    return pl.pallas_call(
        kern,
        out_shape=jax.ShapeDtypeStruct((T, N), BF16),
        grid=(T // TM,),
        in_specs=[
            pl.BlockSpec((TM, D), lambda i: (i, 0)),
            _resident((1, D)),
            _resident((D, N)),
            _resident((2, HEAD_DIM)),
            _resident((2, HEAD_DIM)),
            pl.BlockSpec((TM, HEAD_DIM), pos_map),
            pl.BlockSpec((TM, HEAD_DIM), pos_map),
        ],
        out_specs=pl.BlockSpec((TM, N), lambda i: (i, 0)),
        compiler_params=_params("parallel"),
        name="qkv_proj",
    )(h, g, w, gq2, gk2, cos, sin)


def _attn_kernel(q_ref, k_ref, v_ref, o_ref, *, tq, tk, seq):
    q = q_ref[...]
    q2 = jnp.concatenate([q[:, :HEAD_DIM], q[:, HEAD_DIM:]], axis=0)
    m = jnp.full((2 * tq, 1), -jnp.inf, F32)
    l = jnp.zeros((2 * tq, 1), F32)
    acc = jnp.zeros((2 * tq, HEAD_DIM), F32)
    for c in range(seq // tk):
        k = k_ref[c * tk:(c + 1) * tk, :]
        v = v_ref[c * tk:(c + 1) * tk, :]
        s = lax.dot_general(q2, k, (((1,), (1,)), ((), ())), preferred_element_type=F32)
        m_new = jnp.maximum(m, jnp.max(s, axis=-1, keepdims=True))
        alpha = jnp.exp2(m - m_new)
        p = jnp.exp2(s - m_new)
        l = alpha * l + jnp.sum(p, axis=-1, keepdims=True)
        acc = alpha * acc + jnp.dot(p.astype(BF16), v, preferred_element_type=F32)
        m = m_new
    o = acc / l
    o_ref[:, :HEAD_DIM] = o[:tq].astype(BF16)
    o_ref[:, HEAD_DIM:] = o[tq:].astype(BF16)


def _attention(qkv, row0, batch, seq):
    assert row0 % seq == 0 and seq % TQ == 0 and seq % TKV == 0
    qw = Q_PER_KV * HEAD_DIM
    kern = functools.partial(_attn_kernel, tq=TQ, tk=TKV, seq=seq)
    q_blk0 = row0 // TQ
    s_blk0 = row0 // seq
    k_col0 = N_HEADS
    v_col0 = N_HEADS + N_KV_HEADS
    return pl.pallas_call(
        kern,
        out_shape=jax.ShapeDtypeStruct((batch * seq, N_HEADS * HEAD_DIM), BF16),
        grid=(batch, N_KV_HEADS, seq // TQ),
        in_specs=[
            pl.BlockSpec((TQ, qw), lambda b, h, i: (q_blk0 + b * (seq // TQ) + i, h)),
            pl.BlockSpec((seq, HEAD_DIM), lambda b, h, i: (s_blk0 + b, k_col0 + h)),
            pl.BlockSpec((seq, HEAD_DIM), lambda b, h, i: (s_blk0 + b, v_col0 + h)),
        ],
        out_specs=pl.BlockSpec((TQ, qw), lambda b, h, i: (b * (seq // TQ) + i, h)),
        compiler_params=_params("parallel", "parallel", "arbitrary"),
        name="attention",
    )(qkv, qkv, qkv)


def _mm_res_kernel(x_ref, w_ref, r_ref, o_ref):
    o_ref[...] = r_ref[...] + jnp.dot(x_ref[...], w_ref[...], preferred_element_type=F32)


def _matmul_residual(x, w, res):
    T, K = x.shape
    N = w.shape[1]
    return pl.pallas_call(
        _mm_res_kernel,
        out_shape=jax.ShapeDtypeStruct((T, N), F32),
        grid=(T // TM,),
        in_specs=[
            pl.BlockSpec((TM, K), lambda i: (i, 0)),
            _resident((K, N)),
            pl.BlockSpec((TM, N), lambda i: (i, 0)),
        ],
        out_specs=pl.BlockSpec((TM, N), lambda i: (i, 0)),
        compiler_params=_params("parallel"),
        name="matmul_residual",
    )(x, w, res)


R_E1, R_E2, R_W1, R_W2, R_RANK1, R_RANK2 = range(6)
ROUTE_LOGIT_E0 = N_EXPERT_GROUPS


def _route_kernel(x0_ref, x1_ref, g_ref, w_ref, b_ref, tri_ref, r_ref, cnt_ref, a_ref, carry_ref, *, n0):
    i = pl.program_id(0)

    @pl.when(i == 0)
    def _():
        carry_ref[...] = jnp.zeros_like(carry_ref)

    a = _rms(jnp.where(i < n0, x0_ref[...], x1_ref[...]), g_ref[...])
    _pack_rows(a, a_ref, a.shape[0])
    a_hi = a.astype(BF16)
    a_lo = (a - a_hi.astype(F32)).astype(BF16)
    hi_both = jnp.dot(a_hi, w_ref[...], preferred_element_type=F32)
    logits = (hi_both[:, :LANES]
              + (jnp.dot(a_lo, w_ref[:, :LANES], preferred_element_type=F32) + hi_both[:, LANES:])) + b_ref[...]
    lane = lax.broadcasted_iota(jnp.int32, logits.shape, 1)
    lane_f = lane.astype(F32)
    neg = -jnp.inf
    far = float(LANES)

    def first_max(vals):
        top = jnp.max(vals, axis=-1, keepdims=True)
        idx = jnp.min(jnp.where(vals == top, lane_f, far), axis=-1, keepdims=True)
        return top, idx.astype(jnp.int32)

    gmask = lane < N_EXPERT_GROUPS
    gtop, g_sel = first_max(jnp.where(gmask, logits, neg))
    pg = 1.0 / jnp.sum(jnp.where(gmask, jnp.exp(logits - gtop), 0.0), axis=-1, keepdims=True)
    lo = ROUTE_LOGIT_E0 + g_sel * EXPERTS_PER_GROUP
    le = jnp.where((lane >= lo) & (lane < lo + EXPERTS_PER_GROUP), logits, neg)
    t1, i1 = first_max(le)
    t2, i2 = first_max(jnp.where(lane == i1, neg, le))
    r21 = jnp.exp(t2 - t1)
    w1 = pg / (1.0 + r21)
    w2 = pg * r21 / (1.0 + r21)
    e1 = i1 - ROUTE_LOGIT_E0
    e2 = i2 - ROUTE_LOGIT_E0

    hit1 = lane == e1
    hit2 = lane == e2
    onehot = (hit1 | hit2).astype(F32)
    before = jnp.dot(tri_ref[...], onehot.astype(BF16), preferred_element_type=F32) + carry_ref[0:1, :]
    rank1 = jnp.sum(jnp.where(hit1, before, 0.0), axis=-1, keepdims=True)
    rank2 = jnp.sum(jnp.where(hit2, before, 0.0), axis=-1, keepdims=True)
    carry_ref[...] = carry_ref[...] + jnp.sum(onehot, axis=0, keepdims=True)
    cnt_ref[...] = carry_ref[...]

    rec = jnp.zeros(logits.shape, F32)
    for pos, val in ((R_E1, e1.astype(F32)), (R_E2, e2.astype(F32)), (R_W1, w1), (R_W2, w2),
                     (R_RANK1, rank1), (R_RANK2, rank2)):
        rec = jnp.where(lane == pos, val, rec)
    r_ref[...] = rec


def _route(hs, g, w_r, b_r):
    h0, h1 = hs
    D = h0.shape[1]
    T = h0.shape[0] + h1.shape[0]
    assert D == 2 * SUBLANES * LANES and h0.shape[0] % TM == 0 and h1.shape[0] % TM == 0
    n0 = h0.shape[0] // TM
    tri = (jnp.arange(TM)[:, None] > jnp.arange(TM)[None, :]).astype(BF16)
    w_hi = w_r.astype(BF16)
    w_lo = (w_r - w_hi.astype(F32)).astype(BF16)
    kern = functools.partial(_route_kernel, n0=n0)
    return pl.pallas_call(
        kern,
        out_shape=(jax.ShapeDtypeStruct((T, LANES), F32), jax.ShapeDtypeStruct((SUBLANES, LANES), F32),
                   jax.ShapeDtypeStruct((T * SUBLANES, LANES), U32)),
        grid=(T // TM,),
        in_specs=[
            pl.BlockSpec((TM, D), lambda i: (jnp.minimum(i, n0 - 1), 0)),
            pl.BlockSpec((TM, D), lambda i: (jnp.maximum(i - n0, 0), 0)),
            _resident((1, D)),
            _resident((D, 2 * LANES)),
            _resident((1, LANES)),
            _resident((TM, TM)),
        ],
        out_specs=(pl.BlockSpec((TM, LANES), lambda i: (i, 0)),
                   pl.BlockSpec((SUBLANES, LANES), lambda i: (0, 0)),
                   pl.BlockSpec((TM * SUBLANES, LANES), lambda i: (i, 0))),
        scratch_shapes=[pltpu.VMEM((SUBLANES, LANES), F32)],
        compiler_params=_params("arbitrary"),
        name="moe_route",
    )(h0, h1, g, jnp.concatenate([w_hi, w_lo], axis=1), b_r, tri)


def _tile_copy(src, src_row, dst, dst_row, sem):
    return pltpu.make_async_copy(src.at[pl.ds(pl.multiple_of(src_row * SUBLANES, SUBLANES), SUBLANES), :],
                                 dst.at[pl.ds(pl.multiple_of(dst_row * SUBLANES, SUBLANES), SUBLANES), :], sem)


def _dispatch_kernel(dest_ref, fill_ref, a_ref, xb_hbm, zero_ref, sem, *, rows, n_blocks):
    i = pl.program_id(0)
    base = i * rows

    def start(r, c):
        for k in range(TOP_K):
            _tile_copy(a_ref, r, xb_hbm, dest_ref[TOP_K * (base + r) + k], sem).start()
        return c

    lax.fori_loop(0, rows, start, 0, unroll=8)
    for k in range(TOP_K):
        pltpu.make_async_copy(a_ref, xb_hbm.at[pl.ds(0, rows * SUBLANES), :], sem).wait()

    @pl.when(i == pl.num_programs(0) - 1)
    def _():
        zero_ref[...] = jnp.zeros_like(zero_ref)

        def per_expert(e, c):
            lo = fill_ref[2 * e]
            hi = fill_ref[2 * e + 1]

            def zstart(r, cc):
                _tile_copy(zero_ref, 0, xb_hbm, r, sem).start()
                return cc

            def zwait(r, cc):
                _tile_copy(zero_ref, 0, xb_hbm, r, sem).wait()
                return cc

            lax.fori_loop(lo, hi, zstart, 0)
            lax.fori_loop(lo, hi, zwait, 0)
            return c

        lax.fori_loop(0, N_EXPERTS, per_expert, 0)

        blk = zero_ref.shape[0]

        def block_copy(b):
            return pltpu.make_async_copy(zero_ref, xb_hbm.at[pl.ds(pl.multiple_of(b * blk, blk), blk), :], sem)

        def bstart(b, c):
            block_copy(b).start()
            return c

        def bwait(b, c):
            block_copy(b).wait()
            return c

        first_unused = fill_ref[2 * N_EXPERTS]
        lax.fori_loop(first_unused, n_blocks, bstart, 0)
        lax.fori_loop(first_unused, n_blocks, bwait, 0)


def _dispatch(a_slab, dest, fill, n_blocks, bm):
    T = a_slab.shape[0] // SUBLANES
    rows = MOVE_ROWS
    kern = functools.partial(_dispatch_kernel, rows=rows, n_blocks=n_blocks)
    return pl.pallas_call(
        kern,
        out_shape=jax.ShapeDtypeStruct((n_blocks * bm * SUBLANES, LANES), U32),
        grid_spec=pltpu.PrefetchScalarGridSpec(
            num_scalar_prefetch=2,
            grid=(T // rows,),
            in_specs=[pl.BlockSpec((rows * SUBLANES, LANES), lambda i, d, f: (i, 0))],
            out_specs=pl.BlockSpec(memory_space=pl.ANY),
            scratch_shapes=[pltpu.VMEM((bm * SUBLANES, LANES), U32), pltpu.SemaphoreType.DMA(())],
        ),
        compiler_params=_params("arbitrary"),
        name="moe_dispatch",
    )(dest, fill, a_slab)


def _expert_kernel(be_ref, nu_ref, x_ref, wg_ref, wu_ref, wd_ref, o_ref, x_s, wg_s, wu_s, wd_s, *, bm):
    b = pl.program_id(0)
    prev = be_ref[jnp.maximum(b - 1, 0)]

    @pl.when((b == 0) | (be_ref[b] != prev))
    def _():
        wg_s[...] = wg_ref[0].astype(BF16)
        wu_s[...] = wu_ref[0].astype(BF16)
        wd_s[...] = wd_ref[0].astype(BF16)

    @pl.when(b < nu_ref[0])
    def _():
        half = SUBLANES * LANES
        for c in range(SUBLANES):
            lo, hi = _unpack_rows(x_ref, bm, c)
            x_s[:, c * LANES:(c + 1) * LANES] = lo.astype(BF16)
            x_s[:, half + c * LANES:half + (c + 1) * LANES] = hi.astype(BF16)
        a = x_s[...]
        hg = jnp.dot(a, wg_s[...], preferred_element_type=F32)
        hu = jnp.dot(a, wu_s[...], preferred_element_type=F32)
        h = hg * jax.nn.sigmoid(hg) * hu
        y = jnp.dot(h.astype(BF16), wd_s[...], preferred_element_type=F32)
        _pack_rows(y, o_ref, bm)

    @pl.when(b >= nu_ref[0])
    def _():
        o_ref[...] = jnp.zeros_like(o_ref)


def _experts(xb, w_gate, w_up, w_down, blk_e, n_used):
    bm = EXPERT_BLOCK
    n_blocks = xb.shape[0] // (bm * SUBLANES)
    _, D, De = w_gate.shape
    kern = functools.partial(_expert_kernel, bm=bm)
    return pl.pallas_call(
        kern,
        out_shape=jax.ShapeDtypeStruct(xb.shape, U32),
        grid_spec=pltpu.PrefetchScalarGridSpec(
            num_scalar_prefetch=2,
            grid=(n_blocks,),
            in_specs=[
                pl.BlockSpec((bm * SUBLANES, LANES), lambda b, be, nu: (jnp.minimum(b, nu[0] - 1), 0)),
                pl.BlockSpec((1, D, De), lambda b, be, nu: (be[b], 0, 0)),
                pl.BlockSpec((1, D, De), lambda b, be, nu: (be[b], 0, 0)),
                pl.BlockSpec((1, De, D), lambda b, be, nu: (be[b], 0, 0)),
            ],
            out_specs=pl.BlockSpec((bm * SUBLANES, LANES), lambda b, be, nu: (b, 0)),
            scratch_shapes=[pltpu.VMEM((bm, D), BF16), pltpu.VMEM((D, De), BF16),
                            pltpu.VMEM((D, De), BF16), pltpu.VMEM((De, D), BF16)],
        ),
        compiler_params=_params("arbitrary"),
        name="moe_experts",
    )(blk_e, n_used, xb, w_gate, w_up, w_down)


def _combine_kernel(dest_ref, h_ref, r_ref, yb_hbm, o_ref, y_ref, sem, *, rows, tok0):
    i = pl.program_id(0)

    def issue(step, slot):
        base = tok0 + step * rows

        def start(r, c):
            for k in range(TOP_K):
                _tile_copy(yb_hbm, dest_ref[TOP_K * (base + r) + k], y_ref.at[slot, k], r, sem.at[slot]).start()
            return c

        lax.fori_loop(0, rows, start, 0, unroll=8)

    def finish(slot):
        for k in range(TOP_K):
            pltpu.make_async_copy(yb_hbm.at[pl.ds(0, rows * SUBLANES), :], y_ref.at[slot, k], sem.at[slot]).wait()
        rec = r_ref[...]
        w1 = rec[:, R_W1:R_W1 + 1]
        w2 = rec[:, R_W2:R_W2 + 1]
        half = SUBLANES * LANES
        for c in range(SUBLANES):
            lo1, hi1 = _unpack_rows(y_ref.at[slot, 0], rows, c)
            lo2, hi2 = _unpack_rows(y_ref.at[slot, 1], rows, c)
            sl = slice(c * LANES, (c + 1) * LANES)
            sh = slice(half + c * LANES, half + (c + 1) * LANES)
            o_ref[:, sl] = h_ref[:, sl] + (w1 * lo1 + w2 * lo2)
            o_ref[:, sh] = h_ref[:, sh] + (w1 * hi1 + w2 * hi2)

    @pl.when(i == 0)
    def _():
        issue(0, 0)

    for slot in range(2):
        @pl.when(i % 2 == slot)
        def _():
            @pl.when(i + 1 < pl.num_programs(0))
            def _():
                issue(i + 1, 1 - slot)

            finish(slot)


def _combine(h, rec, yb, dest, tok0):
    Tg, D = h.shape
    rows = MOVE_ROWS
    assert tok0 % rows == 0 and Tg % rows == 0
    blk0 = tok0 // rows
    kern = functools.partial(_combine_kernel, rows=rows, tok0=tok0)
    return pl.pallas_call(
        kern,
        out_shape=jax.ShapeDtypeStruct((Tg, D), F32),
        grid_spec=pltpu.PrefetchScalarGridSpec(
            num_scalar_prefetch=1,
            grid=(Tg // rows,),
            in_specs=[
                pl.BlockSpec((rows, D), lambda i, d: (i, 0)),
                pl.BlockSpec((rows, LANES), lambda i, d: (blk0 + i, 0)),
                pl.BlockSpec(memory_space=pl.ANY),
            ],
            out_specs=pl.BlockSpec((rows, D), lambda i, d: (i, 0)),
            scratch_shapes=[pltpu.VMEM((2, TOP_K, rows * SUBLANES, LANES), U32), pltpu.SemaphoreType.DMA((2,))],
        ),
        compiler_params=_params("arbitrary"),
        name="moe_combine",
    )(dest, h, rec, yb)


def _hier_moe(hs, g, w_rg, b_rg, w_re, b_re, w_gate, w_up, w_down):
    D = hs[0].shape[1]
    T = hs[0].shape[0] + hs[1].shape[0]
    pad = LANES - N_EXPERT_GROUPS - N_EXPERTS
    w_r = jnp.concatenate([w_rg, w_re, jnp.zeros((D, pad), F32)], axis=1)
    b_r = jnp.concatenate([b_rg, b_re, jnp.zeros((pad,), F32)])[None, :]
    rec, cnt, a_slab = _route(hs, g, w_r, b_r)

    bm = EXPERT_BLOCK
    e_tok = rec[:, R_E1:R_E2 + 1].astype(jnp.int32)
    rank = rec[:, R_RANK1:R_RANK2 + 1].astype(jnp.int32)
    counts = cnt[0, :N_EXPERTS].astype(jnp.int32)
    pad_counts = (counts + bm - 1) // bm * bm
    pad_end = jnp.cumsum(pad_counts)
    pad_start = pad_end - pad_counts
    dest = (pad_start[e_tok] + rank).reshape(-1)
    n_blocks = (T * TOP_K) // bm + N_EXPERTS
    blk_start = jnp.arange(n_blocks, dtype=jnp.int32) * bm
    blk_e = jnp.minimum(jnp.sum(blk_start[:, None] >= pad_end[None, :], axis=1), N_EXPERTS - 1).astype(jnp.int32)
    n_used = (pad_end[-1:] // bm).astype(jnp.int32)
    fill = jnp.concatenate([jnp.stack([pad_start + counts, pad_end], axis=1).reshape(-1), n_used]).astype(jnp.int32)

    xb = _dispatch(a_slab, dest, fill, n_blocks, bm)
    yb = _experts(xb, w_gate, w_up, w_down, blk_e, n_used)
    return (_combine(hs[0], rec, yb, dest, 0), _combine(hs[1], rec, yb, dest, hs[0].shape[0]))


def _ple_math(h_ref, g_ref, wg_ref, p_ref, wp_ref):
    h = h_ref[...]
    a = _rms(h, g_ref[...]).astype(BF16)
    gate = jax.nn.sigmoid(jnp.dot(a, wg_ref[...], preferred_element_type=F32))
    emb = jnp.dot(p_ref[...].astype(BF16), wp_ref[...], preferred_element_type=F32)
    return h + gate * emb


def _ple_kernel(h_ref, g_ref, wg_ref, p_ref, wp_ref, o_ref):
    o_ref[...] = _ple_math(h_ref, g_ref, wg_ref, p_ref, wp_ref)


def _ple_final_kernel(h_ref, g_ref, wg_ref, p_ref, wp_ref, gf_ref, o_ref):
    o_ref[...] = _rms(_ple_math(h_ref, g_ref, wg_ref, p_ref, wp_ref), gf_ref[...])


def _ple(h, g, w_gate, p, w_proj, g_final=None):
    T, D = h.shape
    P = p.shape[1]
    in_specs = [
        pl.BlockSpec((TM, D), lambda i: (i, 0)),
        _resident((1, D)),
        _resident((D, D)),
        pl.BlockSpec((TM, P), lambda i: (i, 0)),
        _resident((P, D)),
    ]
    args = (h, g, w_gate, p, w_proj)
    if g_final is not None:
        in_specs.append(_resident((1, D)))
        args += (g_final,)
    return pl.pallas_call(
        _ple_kernel if g_final is None else _ple_final_kernel,
        out_shape=jax.ShapeDtypeStruct((T, D), F32),
        grid=(T // TM,),
        in_specs=in_specs,
        out_specs=pl.BlockSpec((TM, D), lambda i: (i, 0)),
        compiler_params=_params("parallel"),
        name="ple" if g_final is None else "ple_final",
    )(*args)


def _dft_tables(n, scale, sin_sign):
    blk = 64
    assert n % blk == 0
    j = jnp.arange(n, dtype=jnp.int32)

    def cos_sin(k):
        ang = ((j[:, None] * k[None, :]) % n).astype(F32) * (2.0 * np.pi / n)
        return jnp.cos(ang), jnp.sin(ang)

    ca, sa = cos_sin(jnp.arange(n // blk, dtype=jnp.int32) * blk)
    cb, sb = cos_sin(jnp.arange(blk, dtype=jnp.int32))
    ca, sa, cb, sb = ca[:, :, None], sa[:, :, None], cb[:, None, :], sb[:, None, :]
    c = (ca * cb - sa * sb) * scale
    s = (sa * cb + ca * sb) * (sin_sign * scale)
    return c.reshape(n, n).astype(BF16), s.reshape(n, n).astype(BF16)


def _chan_dft_kernel(x_ref, g_ref, c_ref, s_ref, u_ref, v_ref, *, gd):
    a = _rms(x_ref[...], g_ref[...]).astype(BF16)
    c = c_ref[...]
    s = s_ref[...]
    for grp in range(a.shape[1] // gd):
        sl = slice(grp * gd, (grp + 1) * gd)
        u_ref[:, sl] = jnp.dot(a[:, sl], c, preferred_element_type=F32).astype(BF16)
        v_ref[:, sl] = jnp.dot(a[:, sl], s, preferred_element_type=F32).astype(BF16)


def _chan_dft(h, g, c, s):
    T, D = h.shape
    gd = c.shape[0]
    kern = functools.partial(_chan_dft_kernel, gd=gd)
    row = pl.BlockSpec((TM, D), lambda i: (i, 0))
    return pl.pallas_call(
        kern,
        out_shape=(jax.ShapeDtypeStruct((T, D), BF16), jax.ShapeDtypeStruct((T, D), BF16)),
        grid=(T // TM,),
        in_specs=[row, _resident((1, D)), _resident((gd, gd)), _resident((gd, gd))],
        out_specs=(row, row),
        compiler_params=_params("parallel"),
        name="fourier_channel_dft",
    )(h, g, c, s)


def _seq_dft_kernel(c_ref, s_ref, u_ref, v_ref, o_ref, acc_ref):
    k = pl.program_id(3)

    @pl.when(k == 0)
    def _():
        acc_ref[...] = jnp.zeros_like(acc_ref)

    acc_ref[...] += (jnp.dot(c_ref[...], u_ref[...], preferred_element_type=F32)
                     + jnp.dot(s_ref[...], v_ref[...], preferred_element_type=F32))

    @pl.when(k == pl.num_programs(3) - 1)
    def _():
        o_ref[...] = acc_ref[...].astype(BF16)


def _seq_dft(u, v, c, s, row0, batch, seq):
    D = u.shape[1]
    tm, tn, tk = min(DFT_TM, seq), DFT_TN, DFT_TK
    rb0 = row0 // tk
    return pl.pallas_call(
        _seq_dft_kernel,
        out_shape=jax.ShapeDtypeStruct((batch * seq, D), BF16),
        grid=(batch, seq // tm, D // tn, seq // tk),
        in_specs=[
            pl.BlockSpec((tm, tk), lambda b, i, j, k: (i, k)),
            pl.BlockSpec((tm, tk), lambda b, i, j, k: (i, k)),
            pl.BlockSpec((tk, tn), lambda b, i, j, k: (rb0 + b * (seq // tk) + k, j)),
            pl.BlockSpec((tk, tn), lambda b, i, j, k: (rb0 + b * (seq // tk) + k, j)),
        ],
        out_specs=pl.BlockSpec((tm, tn), lambda b, i, j, k: (b * (seq // tm) + i, j)),
        scratch_shapes=[pltpu.VMEM((tm, tn), F32)],
        compiler_params=_params("parallel", "parallel", "parallel", "arbitrary"),
        name="fourier_seq_dft",
    )(c, s, u, v)


def kernel(x_prompt, x_sample, p_prompt, p_sample, g_mix, w_qkv, g_q, g_k, w_attn_out, w_fourier_out, g_ffn, w_route_group, b_route_group, w_route_expert, b_route_expert, w_exp_gate, w_exp_up, w_exp_down, g_ple, w_ple_gate, w_ple_proj, g_final):
    depth = g_mix.shape[0]
    bp, sp, D = x_prompt.shape
    bs, ss, _ = x_sample.shape
    tp, ts = bp * sp, bs * ss
    gd = D // N_FOURIER_GROUPS
    geom = ((bs, ss), (bp, sp))
    hs = (x_sample.reshape(ts, D), x_prompt.reshape(tp, D))
    ps = (p_sample, p_prompt)

    n_mixers = 2
    for i in range(depth):
        jm = i // n_mixers
        g_i = g_mix[i][None, :]
        if i % n_mixers == 0:
            w, gq2, gk2 = _qkv_weights(w_qkv[jm], g_q[jm], g_k[jm])
            cos, sin = _rope_tables(max(sp, ss))
            w_o = w_attn_out[jm].astype(BF16)
            mixed = []
            for h, (batch, seq) in zip(hs, geom):
                qkv = _qkv_proj(h, g_i, w, gq2, gk2, cos, sin, seq)
                mixed.append(_matmul_residual(_attention(qkv, 0, batch, seq), w_o, h))
        else:
            cc, sc = _dft_tables(gd, 1.0 / float(np.sqrt(gd)), -1.0)
            w_o = w_fourier_out[jm].astype(BF16)
            mixed = []
            for h, (batch, seq) in zip(hs, geom):
                u, v = _chan_dft(h, g_i, cc, sc)
                cs, sn = _dft_tables(seq, 1.0 / float(np.sqrt(seq)), 1.0)
                mixed.append(_matmul_residual(_seq_dft(u, v, cs, sn, 0, batch, seq), w_o, h))
        hs = _hier_moe(tuple(mixed), g_ffn[i][None, :], w_route_group[i], b_route_group[i], w_route_expert[i],
                       b_route_expert[i], w_exp_gate[i], w_exp_up[i], w_exp_down[i])
        w_pg = w_ple_gate[i].astype(BF16)
        w_pp = w_ple_proj[i].astype(BF16)
        g_f = g_final[None, :] if i == depth - 1 else None
        hs = tuple(_ple(h, g_ple[i][None, :], w_pg, p[i].reshape(h.shape[0], -1), w_pp, g_f)
                   for h, p in zip(hs, ps))
    y_sample, y_prompt = hs
    return (y_prompt.reshape(bp, sp, D), y_sample.reshape(bs, ss, D))
```

```python
import functools

import jax
import jax.numpy as jnp
import numpy as np
from jax import lax
from jax.experimental import pallas as pl
from jax.experimental.pallas import tpu as pltpu

F32 = jnp.float32
BF16 = jnp.bfloat16
U32 = jnp.uint32

HEAD_DIM = 128
N_HEADS = 16
N_KV_HEADS = 8
Q_PER_KV = N_HEADS // N_KV_HEADS
ROPE_AXIS_DIM = HEAD_DIM // 2
ROPE_THETA = 10000.0
GRID_W = 64
N_FOURIER_GROUPS = 8
N_EXPERT_GROUPS = 4
EXPERTS_PER_GROUP = 8
N_EXPERTS = N_EXPERT_GROUPS * EXPERTS_PER_GROUP
TOP_K = 2
NORM_EPS = 1e-6

LANES = 128
SUBLANES = 8
MXU_DIM = 256
VMEM_LIMIT = 56 * 1024 * 1024

TM = 512
TQ = 256
TKV = 1024
EXPERT_BLOCK = 256
MOVE_ROWS = 256
DFT_TM, DFT_TN, DFT_TK = 1024, 1024, 512


def _params(*sem):
    return pltpu.CompilerParams(dimension_semantics=sem, vmem_limit_bytes=VMEM_LIMIT)


def _resident(shape):
    return pl.BlockSpec(shape, lambda *_: (0,) * len(shape), pipeline_mode=pl.Buffered(1))


def _rms(x, g):
    ms = jnp.mean(x * x, axis=-1, keepdims=True)
    return x * lax.rsqrt(ms + NORM_EPS) * g


def _pack_rows(x, ref, rows):
    half = SUBLANES * LANES
    for c in range(SUBLANES):
        lo = x[:, c * LANES:(c + 1) * LANES].astype(BF16).astype(F32)
        hi = x[:, half + c * LANES:half + (c + 1) * LANES].astype(BF16).astype(F32)
        word = (pltpu.bitcast(lo, U32) >> 16) | (pltpu.bitcast(hi, U32) & jnp.uint32(0xFFFF0000))
        ref[pl.ds(c, rows, stride=SUBLANES), :] = word


def _unpack_rows(ref, rows, c):
    word = ref[pl.ds(c, rows, stride=SUBLANES), :]
    lo = pltpu.bitcast(word << 16, F32)
    hi = pltpu.bitcast(word & jnp.uint32(0xFFFF0000), F32)
    return lo, hi


def _qkv_kernel(x_ref, g_ref, w_ref, gq_ref, gk_ref, cos_ref, sin_ref, o_ref, *, d_q, d_qk):
    a = _rms(x_ref[...], g_ref[...]).astype(BF16)
    cos = cos_ref[...]
    sin = sin_ref[...]
    q_scale = float(np.log2(np.e) / np.sqrt(HEAD_DIM))
    tabs = {
        "q": (gq_ref[0:1, :] * cos * q_scale, gq_ref[1:2, :] * sin * q_scale),
        "k": (gk_ref[0:1, :] * cos, gk_ref[1:2, :] * sin),
    }
    n_out = o_ref.shape[1]
    for c0 in range(0, n_out, MXU_DIM):
        acc = jnp.dot(a, w_ref[:, c0:c0 + MXU_DIM], preferred_element_type=F32)
        if c0 >= d_qk:
            o_ref[:, c0:c0 + MXU_DIM] = acc.astype(BF16)
            continue
        t1, t2 = tabs["q" if c0 < d_q else "k"]
        for hh in range(MXU_DIM // HEAD_DIM):
            y = acc[:, hh * HEAD_DIM:(hh + 1) * HEAD_DIM]
            r = lax.rsqrt(jnp.mean(y * y, axis=-1, keepdims=True) + NORM_EPS)
            rot = (y * t1 + pltpu.roll(y, HEAD_DIM // 2, 1) * t2) * r
            o_ref[:, c0 + hh * HEAD_DIM:c0 + (hh + 1) * HEAD_DIM] = rot.astype(BF16)


def _permute_heads(x):
    lead = x.shape[:-1]
    x = x.reshape(lead + (-1, 2, 2, HEAD_DIM // 4))
    return jnp.swapaxes(x, -3, -2).reshape(lead + (-1,))


def _rope_tables(n_pos):
    pos = jnp.arange(n_pos, dtype=jnp.int32)
    row = (pos // GRID_W).astype(F32)
    col = (pos % GRID_W).astype(F32)
    inv_freq = ROPE_THETA ** (-jnp.arange(0, ROPE_AXIS_DIM, 2, dtype=F32) / ROPE_AXIS_DIM)
    ang_r = row[:, None] * inv_freq[None, :]
    ang_c = col[:, None] * inv_freq[None, :]
    cos = jnp.concatenate([jnp.cos(ang_r), jnp.cos(ang_c), jnp.cos(ang_r), jnp.cos(ang_c)], axis=-1)
    sin = jnp.concatenate([-jnp.sin(ang_r), -jnp.sin(ang_c), jnp.sin(ang_r), jnp.sin(ang_c)], axis=-1)
    return cos, sin


def _qkv_weights(w_qkv, g_q, g_k):
    d_qk = (N_HEADS + N_KV_HEADS) * HEAD_DIM
    w = jnp.concatenate([_permute_heads(w_qkv[:, :d_qk]).astype(BF16), w_qkv[:, d_qk:].astype(BF16)], axis=1)
    gq = _permute_heads(g_q)
    gk = _permute_heads(g_k)
    gq2 = jnp.stack([gq, jnp.roll(gq, HEAD_DIM // 2)])
    gk2 = jnp.stack([gk, jnp.roll(gk, HEAD_DIM // 2)])
    return w, gq2, gk2


def _qkv_proj(h, g, w, gq2, gk2, cos, sin, seq):
    T, D = h.shape
    N = w.shape[1]
    assert seq % TM == 0 and T % seq == 0
    d_q = N_HEADS * HEAD_DIM
    d_qk = d_q + N_KV_HEADS * HEAD_DIM

    def pos_map(i):
        return (i % (seq // TM), 0)

    kern = functools.partial(_qkv_kernel, d_q=d_q, d_qk=d_qk)
    return pl.pallas_call(
        kern,
        out_shape=jax.ShapeDtypeStruct((T, N), BF16),
        grid=(T // TM,),
        in_specs=[
            pl.BlockSpec((TM, D), lambda i: (i, 0)),
            _resident((1, D)),
            _resident((D, N)),
            _resident((2, HEAD_DIM)),
            _resident((2, HEAD_DIM)),
            pl.BlockSpec((TM, HEAD_DIM), pos_map),
            pl.BlockSpec((TM, HEAD_DIM), pos_map),
        ],
        out_specs=pl.BlockSpec((TM, N), lambda i: (i, 0)),
        compiler_params=_params("parallel"),
        name="qkv_proj",
    )(h, g, w, gq2, gk2, cos, sin)


def _attn_kernel(q_ref, k_ref, v_ref, o_ref, *, tq, tk, seq):
    q = q_ref[...]
    q2 = jnp.concatenate([q[:, :HEAD_DIM], q[:, HEAD_DIM:]], axis=0)
    m = jnp.full((2 * tq, 1), -jnp.inf, F32)
    l = jnp.zeros((2 * tq, 1), F32)
    acc = jnp.zeros((2 * tq, HEAD_DIM), F32)
    for c in range(seq // tk):
        k = k_ref[c * tk:(c + 1) * tk, :]
        v = v_ref[c * tk:(c + 1) * tk, :]
        s = lax.dot_general(q2, k, (((1,), (1,)), ((), ())), preferred_element_type=F32)
        m_new = jnp.maximum(m, jnp.max(s, axis=-1, keepdims=True))
        alpha = jnp.exp2(m - m_new)
        p = jnp.exp2(s - m_new)
        l = alpha * l + jnp.sum(p, axis=-1, keepdims=True)
        acc = alpha * acc + jnp.dot(p.astype(BF16), v, preferred_element_type=F32)
        m = m_new
    o = acc / l
    o_ref[:, :HEAD_DIM] = o[:tq].astype(BF16)
    o_ref[:, HEAD_DIM:] = o[tq:].astype(BF16)


def _attention(qkv, row0, batch, seq):
    assert row0 % seq == 0 and seq % TQ == 0 and seq % TKV == 0
    qw = Q_PER_KV * HEAD_DIM
    kern = functools.partial(_attn_kernel, tq=TQ, tk=TKV, seq=seq)
    q_blk0 = row0 // TQ
    s_blk0 = row0 // seq
    k_col0 = N_HEADS
    v_col0 = N_HEADS + N_KV_HEADS
    return pl.pallas_call(
        kern,
        out_shape=jax.ShapeDtypeStruct((batch * seq, N_HEADS * HEAD_DIM), BF16),
        grid=(batch, N_KV_HEADS, seq // TQ),
        in_specs=[
            pl.BlockSpec((TQ, qw), lambda b, h, i: (q_blk0 + b * (seq // TQ) + i, h)),
            pl.BlockSpec((seq, HEAD_DIM), lambda b, h, i: (s_blk0 + b, k_col0 + h)),
            pl.BlockSpec((seq, HEAD_DIM), lambda b, h, i: (s_blk0 + b, v_col0 + h)),
        ],
        out_specs=pl.BlockSpec((TQ, qw), lambda b, h, i: (b * (seq // TQ) + i, h)),
        compiler_params=_params("parallel", "parallel", "arbitrary"),
        name="attention",
    )(qkv, qkv, qkv)


def _mm_res_kernel(x_ref, w_ref, r_ref, o_ref):
    o_ref[...] = r_ref[...] + jnp.dot(x_ref[...], w_ref[...], preferred_element_type=F32)


def _matmul_residual(x, w, res):
    T, K = x.shape
    N = w.shape[1]
    return pl.pallas_call(
        _mm_res_kernel,
        out_shape=jax.ShapeDtypeStruct((T, N), F32),
        grid=(T // TM,),
        in_specs=[
            pl.BlockSpec((TM, K), lambda i: (i, 0)),
            _resident((K, N)),
            pl.BlockSpec((TM, N), lambda i: (i, 0)),
        ],
        out_specs=pl.BlockSpec((TM, N), lambda i: (i, 0)),
        compiler_params=_params("parallel"),
        name="matmul_residual",
    )(x, w, res)


R_E1, R_E2, R_W1, R_W2, R_RANK1, R_RANK2 = range(6)
ROUTE_LOGIT_E0 = N_EXPERT_GROUPS


def _route_kernel(x0_ref, x1_ref, g_ref, w_ref, b_ref, tri_ref, r_ref, cnt_ref, a_ref, carry_ref, *, n0):
    i = pl.program_id(0)

    @pl.when(i == 0)
    def _():
        carry_ref[...] = jnp.zeros_like(carry_ref)

    a = _rms(jnp.where(i < n0, x0_ref[...], x1_ref[...]), g_ref[...])
    _pack_rows(a, a_ref, a.shape[0])
    a_hi = a.astype(BF16)
    a_lo = (a - a_hi.astype(F32)).astype(BF16)
    hi_both = jnp.dot(a_hi, w_ref[...], preferred_element_type=F32)
    logits = (hi_both[:, :LANES]
              + (jnp.dot(a_lo, w_ref[:, :LANES], preferred_element_type=F32) + hi_both[:, LANES:])) + b_ref[...]
    lane = lax.broadcasted_iota(jnp.int32, logits.shape, 1)
    lane_f = lane.astype(F32)
    neg = -jnp.inf
    far = float(LANES)

    def first_max(vals):
        top = jnp.max(vals, axis=-1, keepdims=True)
        idx = jnp.min(jnp.where(vals == top, lane_f, far), axis=-1, keepdims=True)
        return top, idx.astype(jnp.int32)

    gmask = lane < N_EXPERT_GROUPS
    gtop, g_sel = first_max(jnp.where(gmask, logits, neg))
    pg = 1.0 / jnp.sum(jnp.where(gmask, jnp.exp(logits - gtop), 0.0), axis=-1, keepdims=True)
    lo = ROUTE_LOGIT_E0 + g_sel * EXPERTS_PER_GROUP
    le = jnp.where((lane >= lo) & (lane < lo + EXPERTS_PER_GROUP), logits, neg)
    t1, i1 = first_max(le)
    t2, i2 = first_max(jnp.where(lane == i1, neg, le))
    r21 = jnp.exp(t2 - t1)
    w1 = pg / (1.0 + r21)
    w2 = pg * r21 / (1.0 + r21)
    e1 = i1 - ROUTE_LOGIT_E0
    e2 = i2 - ROUTE_LOGIT_E0

    hit1 = lane == e1
    hit2 = lane == e2
    onehot = (hit1 | hit2).astype(F32)
    before = jnp.dot(tri_ref[...], onehot.astype(BF16), preferred_element_type=F32) + carry_ref[0:1, :]
    rank1 = jnp.sum(jnp.where(hit1, before, 0.0), axis=-1, keepdims=True)
    rank2 = jnp.sum(jnp.where(hit2, before, 0.0), axis=-1, keepdims=True)
    carry_ref[...] = carry_ref[...] + jnp.sum(onehot, axis=0, keepdims=True)
    cnt_ref[...] = carry_ref[...]

    rec = jnp.zeros(logits.shape, F32)
    for pos, val in ((R_E1, e1.astype(F32)), (R_E2, e2.astype(F32)), (R_W1, w1), (R_W2, w2),
                     (R_RANK1, rank1), (R_RANK2, rank2)):
        rec = jnp.where(lane == pos, val, rec)
    r_ref[...] = rec


def _route(hs, g, w_r, b_r):
    h0, h1 = hs
    D = h0.shape[1]
    T = h0.shape[0] + h1.shape[0]
    assert D == 2 * SUBLANES * LANES and h0.shape[0] % TM == 0 and h1.shape[0] % TM == 0
    n0 = h0.shape[0] // TM
    tri = (jnp.arange(TM)[:, None] > jnp.arange(TM)[None, :]).astype(BF16)
    w_hi = w_r.astype(BF16)
    w_lo = (w_r - w_hi.astype(F32)).astype(BF16)
    kern = functools.partial(_route_kernel, n0=n0)
    return pl.pallas_call(
        kern,
        out_shape=(jax.ShapeDtypeStruct((T, LANES), F32), jax.ShapeDtypeStruct((SUBLANES, LANES), F32),
                   jax.ShapeDtypeStruct((T * SUBLANES, LANES), U32)),
        grid=(T // TM,),
        in_specs=[
            pl.BlockSpec((TM, D), lambda i: (jnp.minimum(i, n0 - 1), 0)),
            pl.BlockSpec((TM, D), lambda i: (jnp.maximum(i - n0, 0), 0)),
            _resident((1, D)),
            _resident((D, 2 * LANES)),
            _resident((1, LANES)),
            _resident((TM, TM)),
        ],
        out_specs=(pl.BlockSpec((TM, LANES), lambda i: (i, 0)),
                   pl.BlockSpec((SUBLANES, LANES), lambda i: (0, 0)),
                   pl.BlockSpec((TM * SUBLANES, LANES), lambda i: (i, 0))),
        scratch_shapes=[pltpu.VMEM((SUBLANES, LANES), F32)],
        compiler_params=_params("arbitrary"),
        name="moe_route",
    )(h0, h1, g, jnp.concatenate([w_hi, w_lo], axis=1), b_r, tri)


def _tile_copy(src, src_row, dst, dst_row, sem):
    return pltpu.make_async_copy(src.at[pl.ds(pl.multiple_of(src_row * SUBLANES, SUBLANES), SUBLANES), :],
                                 dst.at[pl.ds(pl.multiple_of(dst_row * SUBLANES, SUBLANES), SUBLANES), :], sem)


def _dispatch_kernel(dest_ref, fill_ref, a_ref, xb_hbm, zero_ref, sem, *, rows, n_blocks):
    i = pl.program_id(0)
    base = i * rows

    def start(r, c):
        for k in range(TOP_K):
            _tile_copy(a_ref, r, xb_hbm, dest_ref[TOP_K * (base + r) + k], sem).start(priority=k)
        return c

    lax.fori_loop(0, rows, start, 0, unroll=8)
    for k in range(TOP_K):
        pltpu.make_async_copy(a_ref, xb_hbm.at[pl.ds(0, rows * SUBLANES), :], sem).wait()

    @pl.when(i == pl.num_programs(0) - 1)
    def _():
        zero_ref[...] = jnp.zeros_like(zero_ref)

        def per_expert(e, c):
            lo = fill_ref[2 * e]
            hi = fill_ref[2 * e + 1]

            def zstart(r, cc):
                _tile_copy(zero_ref, 0, xb_hbm, r, sem).start()
                return cc

            def zwait(r, cc):
                _tile_copy(zero_ref, 0, xb_hbm, r, sem).wait()
                return cc

            lax.fori_loop(lo, hi, zstart, 0)
            lax.fori_loop(lo, hi, zwait, 0)
            return c

        lax.fori_loop(0, N_EXPERTS, per_expert, 0)

        blk = zero_ref.shape[0]

        def block_copy(b):
            return pltpu.make_async_copy(zero_ref, xb_hbm.at[pl.ds(pl.multiple_of(b * blk, blk), blk), :], sem)

        def bstart(b, c):
            block_copy(b).start()
            return c

        def bwait(b, c):
            block_copy(b).wait()
            return c

        first_unused = fill_ref[2 * N_EXPERTS]
        lax.fori_loop(first_unused, n_blocks, bstart, 0)
        lax.fori_loop(first_unused, n_blocks, bwait, 0)


def _dispatch(a_slab, dest, fill, n_blocks, bm):
    T = a_slab.shape[0] // SUBLANES
    rows = MOVE_ROWS
    kern = functools.partial(_dispatch_kernel, rows=rows, n_blocks=n_blocks)
    return pl.pallas_call(
        kern,
        out_shape=jax.ShapeDtypeStruct((n_blocks * bm * SUBLANES, LANES), U32),
        grid_spec=pltpu.PrefetchScalarGridSpec(
            num_scalar_prefetch=2,
            grid=(T // rows,),
            in_specs=[pl.BlockSpec((rows * SUBLANES, LANES), lambda i, d, f: (i, 0))],
            out_specs=pl.BlockSpec(memory_space=pl.ANY),
            scratch_shapes=[pltpu.VMEM((bm * SUBLANES, LANES), U32), pltpu.SemaphoreType.DMA(())],
        ),
        compiler_params=_params("arbitrary"),
        name="moe_dispatch",
    )(dest, fill, a_slab)


def _expert_kernel(be_ref, nu_ref, x_ref, wg_ref, wu_ref, wd_ref, o_ref, x_s, wg_s, wu_s, wd_s, *, bm):
    b = pl.program_id(0)
    prev = be_ref[jnp.maximum(b - 1, 0)]

    @pl.when((b == 0) | (be_ref[b] != prev))
    def _():
        wg_s[...] = wg_ref[0, 0].astype(BF16)
        wu_s[...] = wu_ref[0, 0].astype(BF16)
        wd_s[...] = wd_ref[0, 0].astype(BF16)

    @pl.when(b < nu_ref[0])
    def _():
        half = SUBLANES * LANES
        for c in range(SUBLANES):
            lo, hi = _unpack_rows(x_ref, bm, c)
            x_s[:, c * LANES:(c + 1) * LANES] = lo.astype(BF16)
            x_s[:, half + c * LANES:half + (c + 1) * LANES] = hi.astype(BF16)
        a = x_s[...]
        hg = jnp.dot(a, wg_s[...], preferred_element_type=F32)
        hu = jnp.dot(a, wu_s[...], preferred_element_type=F32)
        h = hg * jax.nn.sigmoid(hg) * hu
        y = jnp.dot(h.astype(BF16), wd_s[...], preferred_element_type=F32)
        _pack_rows(y, o_ref, bm)

    @pl.when(b >= nu_ref[0])
    def _():
        o_ref[...] = jnp.zeros_like(o_ref)


def _experts(xb, w_gate, w_up, w_down, layer, blk_e, n_used):
    bm = EXPERT_BLOCK
    n_blocks = xb.shape[0] // (bm * SUBLANES)
    _, _, D, De = w_gate.shape
    kern = functools.partial(_expert_kernel, bm=bm)
    return pl.pallas_call(
        kern,
        out_shape=jax.ShapeDtypeStruct(xb.shape, U32),
        grid_spec=pltpu.PrefetchScalarGridSpec(
            num_scalar_prefetch=2,
            grid=(n_blocks,),
            in_specs=[
                pl.BlockSpec((bm * SUBLANES, LANES), lambda b, be, nu: (jnp.minimum(b, nu[0] - 1), 0)),
                pl.BlockSpec((1, 1, D, De), lambda b, be, nu: (layer, be[b], 0, 0)),
                pl.BlockSpec((1, 1, D, De), lambda b, be, nu: (layer, be[b], 0, 0)),
                pl.BlockSpec((1, 1, De, D), lambda b, be, nu: (layer, be[b], 0, 0)),
            ],
            out_specs=pl.BlockSpec((bm * SUBLANES, LANES), lambda b, be, nu: (b, 0)),
            scratch_shapes=[pltpu.VMEM((bm, D), BF16), pltpu.VMEM((D, De), BF16),
                            pltpu.VMEM((D, De), BF16), pltpu.VMEM((De, D), BF16)],
        ),
        compiler_params=_params("arbitrary"),
        name="moe_experts",
    )(blk_e, n_used, xb, w_gate, w_up, w_down)


def _combine_kernel(dest_ref, h_ref, r_ref, yb_hbm, o_ref, y_ref, sem, *, rows, tok0):
    i = pl.program_id(0)

    def issue(step, slot):
        base = tok0 + step * rows

        def start(r, c):
            for k in range(TOP_K):
                _tile_copy(yb_hbm, dest_ref[TOP_K * (base + r) + k], y_ref.at[slot, k], r,
                           sem.at[slot]).start(priority=k)
            return c

        lax.fori_loop(0, rows, start, 0, unroll=8)

    def finish(slot):
        for k in range(TOP_K):
            pltpu.make_async_copy(yb_hbm.at[pl.ds(0, rows * SUBLANES), :], y_ref.at[slot, k], sem.at[slot]).wait()
        rec = r_ref[...]
        w1 = rec[:, R_W1:R_W1 + 1]
        w2 = rec[:, R_W2:R_W2 + 1]
        half = SUBLANES * LANES
        for c in range(SUBLANES):
            lo1, hi1 = _unpack_rows(y_ref.at[slot, 0], rows, c)
            lo2, hi2 = _unpack_rows(y_ref.at[slot, 1], rows, c)
            sl = slice(c * LANES, (c + 1) * LANES)
            sh = slice(half + c * LANES, half + (c + 1) * LANES)
            o_ref[:, sl] = h_ref[:, sl] + (w1 * lo1 + w2 * lo2)
            o_ref[:, sh] = h_ref[:, sh] + (w1 * hi1 + w2 * hi2)

    @pl.when(i == 0)
    def _():
        issue(0, 0)

    for slot in range(2):
        @pl.when(i % 2 == slot)
        def _():
            @pl.when(i + 1 < pl.num_programs(0))
            def _():
                issue(i + 1, 1 - slot)

            finish(slot)


def _combine(h, rec, yb, dest, tok0):
    Tg, D = h.shape
    rows = MOVE_ROWS
    assert tok0 % rows == 0 and Tg % rows == 0
    blk0 = tok0 // rows
    kern = functools.partial(_combine_kernel, rows=rows, tok0=tok0)
    return pl.pallas_call(
        kern,
        out_shape=jax.ShapeDtypeStruct((Tg, D), F32),
        grid_spec=pltpu.PrefetchScalarGridSpec(
            num_scalar_prefetch=1,
            grid=(Tg // rows,),
            in_specs=[
                pl.BlockSpec((rows, D), lambda i, d: (i, 0)),
                pl.BlockSpec((rows, LANES), lambda i, d: (blk0 + i, 0)),
                pl.BlockSpec(memory_space=pl.ANY),
            ],
            out_specs=pl.BlockSpec((rows, D), lambda i, d: (i, 0)),
            scratch_shapes=[pltpu.VMEM((2, TOP_K, rows * SUBLANES, LANES), U32), pltpu.SemaphoreType.DMA((2,))],
        ),
        compiler_params=_params("arbitrary"),
        name="moe_combine",
    )(dest, h, rec, yb)


def _hier_moe(hs, g, w_rg, b_rg, w_re, b_re, w_gate, w_up, w_down, layer):
    D = hs[0].shape[1]
    T = hs[0].shape[0] + hs[1].shape[0]
    pad = LANES - N_EXPERT_GROUPS - N_EXPERTS
    w_r = jnp.concatenate([w_rg, w_re, jnp.zeros((D, pad), F32)], axis=1)
    b_r = jnp.concatenate([b_rg, b_re, jnp.zeros((pad,), F32)])[None, :]
    rec, cnt, a_slab = _route(hs, g, w_r, b_r)

    bm = EXPERT_BLOCK
    e_tok = rec[:, R_E1:R_E2 + 1].astype(jnp.int32)
    rank = rec[:, R_RANK1:R_RANK2 + 1].astype(jnp.int32)
    counts = cnt[0, :N_EXPERTS].astype(jnp.int32)
    pad_counts = (counts + bm - 1) // bm * bm
    pad_end = jnp.cumsum(pad_counts)
    pad_start = pad_end - pad_counts
    dest = (pad_start[e_tok] + rank).reshape(-1)
    n_blocks = (T * TOP_K) // bm + N_EXPERTS
    blk_start = jnp.arange(n_blocks, dtype=jnp.int32) * bm
    blk_e = jnp.minimum(jnp.sum(blk_start[:, None] >= pad_end[None, :], axis=1), N_EXPERTS - 1).astype(jnp.int32)
    n_used = (pad_end[-1:] // bm).astype(jnp.int32)
    fill = jnp.concatenate([jnp.stack([pad_start + counts, pad_end], axis=1).reshape(-1), n_used]).astype(jnp.int32)

    xb = _dispatch(a_slab, dest, fill, n_blocks, bm)
    yb = _experts(xb, w_gate, w_up, w_down, layer, blk_e, n_used)
    return (_combine(hs[0], rec, yb, dest, 0), _combine(hs[1], rec, yb, dest, hs[0].shape[0]))


def _ple_math(h_ref, g_ref, wg_ref, p_ref, wp_ref):
    h = h_ref[...]
    a = _rms(h, g_ref[...]).astype(BF16)
    gate = jax.nn.sigmoid(jnp.dot(a, wg_ref[...], preferred_element_type=F32))
    emb = jnp.dot(p_ref[...].astype(BF16), wp_ref[...], preferred_element_type=F32)
    return h + gate * emb


def _ple_kernel(h_ref, g_ref, wg_ref, p_ref, wp_ref, o_ref):
    o_ref[...] = _ple_math(h_ref, g_ref, wg_ref, p_ref, wp_ref)


def _ple_final_kernel(h_ref, g_ref, wg_ref, p_ref, wp_ref, gf_ref, o_ref):
    o_ref[...] = _rms(_ple_math(h_ref, g_ref, wg_ref, p_ref, wp_ref), gf_ref[...])


def _ple(h, g, w_gate, p, w_proj, g_final=None):
    T, D = h.shape
    P = p.shape[1]
    in_specs = [
        pl.BlockSpec((TM, D), lambda i: (i, 0)),
        _resident((1, D)),
        _resident((D, D)),
        pl.BlockSpec((TM, P), lambda i: (i, 0)),
        _resident((P, D)),
    ]
    args = (h, g, w_gate, p, w_proj)
    if g_final is not None:
        in_specs.append(_resident((1, D)))
        args += (g_final,)
    return pl.pallas_call(
        _ple_kernel if g_final is None else _ple_final_kernel,
        out_shape=jax.ShapeDtypeStruct((T, D), F32),
        grid=(T // TM,),
        in_specs=in_specs,
        out_specs=pl.BlockSpec((TM, D), lambda i: (i, 0)),
        compiler_params=_params("parallel"),
        name="ple" if g_final is None else "ple_final",
    )(*args)


def _dft_tables(n, scale, sin_sign):
    blk = 64
    assert n % blk == 0
    j = jnp.arange(n, dtype=jnp.int32)

    def cos_sin(k):
        ang = ((j[:, None] * k[None, :]) % n).astype(F32) * (2.0 * np.pi / n)
        return jnp.cos(ang), jnp.sin(ang)

    ca, sa = cos_sin(jnp.arange(n // blk, dtype=jnp.int32) * blk)
    cb, sb = cos_sin(jnp.arange(blk, dtype=jnp.int32))
    ca, sa, cb, sb = ca[:, :, None], sa[:, :, None], cb[:, None, :], sb[:, None, :]
    c = (ca * cb - sa * sb) * scale
    s = (sa * cb + ca * sb) * (sin_sign * scale)
    return c.reshape(n, n).astype(BF16), s.reshape(n, n).astype(BF16)


def _chan_dft_kernel(x_ref, g_ref, c_ref, s_ref, u_ref, v_ref, *, gd):
    a = _rms(x_ref[...], g_ref[...]).astype(BF16)
    c = c_ref[...]
    s = s_ref[...]
    for grp in range(a.shape[1] // gd):
        sl = slice(grp * gd, (grp + 1) * gd)
        u_ref[:, sl] = jnp.dot(a[:, sl], c, preferred_element_type=F32).astype(BF16)
        v_ref[:, sl] = jnp.dot(a[:, sl], s, preferred_element_type=F32).astype(BF16)


def _chan_dft(h, g, c, s):
    T, D = h.shape
    gd = c.shape[0]
    kern = functools.partial(_chan_dft_kernel, gd=gd)
    row = pl.BlockSpec((TM, D), lambda i: (i, 0))
    return pl.pallas_call(
        kern,
        out_shape=(jax.ShapeDtypeStruct((T, D), BF16), jax.ShapeDtypeStruct((T, D), BF16)),
        grid=(T // TM,),
        in_specs=[row, _resident((1, D)), _resident((gd, gd)), _resident((gd, gd))],
        out_specs=(row, row),
        compiler_params=_params("parallel"),
        name="fourier_channel_dft",
    )(h, g, c, s)


def _seq_dft_kernel(c_ref, s_ref, u_ref, v_ref, o_ref, acc_ref):
    k = pl.program_id(3)

    @pl.when(k == 0)
    def _():
        acc_ref[...] = jnp.zeros_like(acc_ref)

    acc_ref[...] += (jnp.dot(c_ref[...], u_ref[...], preferred_element_type=F32)
                     + jnp.dot(s_ref[...], v_ref[...], preferred_element_type=F32))

    @pl.when(k == pl.num_programs(3) - 1)
    def _():
        o_ref[...] = acc_ref[...].astype(BF16)


def _seq_dft(u, v, c, s, row0, batch, seq):
    D = u.shape[1]
    tm, tn, tk = min(DFT_TM, seq), DFT_TN, DFT_TK
    rb0 = row0 // tk
    return pl.pallas_call(
        _seq_dft_kernel,
        out_shape=jax.ShapeDtypeStruct((batch * seq, D), BF16),
        grid=(batch, seq // tm, D // tn, seq // tk),
        in_specs=[
            pl.BlockSpec((tm, tk), lambda b, i, j, k: (i, k)),
            pl.BlockSpec((tm, tk), lambda b, i, j, k: (i, k)),
            pl.BlockSpec((tk, tn), lambda b, i, j, k: (rb0 + b * (seq // tk) + k, j)),
            pl.BlockSpec((tk, tn), lambda b, i, j, k: (rb0 + b * (seq // tk) + k, j)),
        ],
        out_specs=pl.BlockSpec((tm, tn), lambda b, i, j, k: (b * (seq // tm) + i, j)),
        scratch_shapes=[pltpu.VMEM((tm, tn), F32)],
        compiler_params=_params("parallel", "parallel", "parallel", "arbitrary"),
        name="fourier_seq_dft",
    )(c, s, u, v)


def kernel(x_prompt, x_sample, p_prompt, p_sample, g_mix, w_qkv, g_q, g_k, w_attn_out, w_fourier_out, g_ffn, w_route_group, b_route_group, w_route_expert, b_route_expert, w_exp_gate, w_exp_up, w_exp_down, g_ple, w_ple_gate, w_ple_proj, g_final):
    depth = g_mix.shape[0]
    bp, sp, D = x_prompt.shape
    bs, ss, _ = x_sample.shape
    tp, ts = bp * sp, bs * ss
    gd = D // N_FOURIER_GROUPS
    geom = ((bs, ss), (bp, sp))
    hs = (x_sample.reshape(ts, D), x_prompt.reshape(tp, D))
    ps = (p_sample, p_prompt)

    n_mixers = 2
    for i in range(depth):
        jm = i // n_mixers
        g_i = g_mix[i][None, :]
        if i % n_mixers == 0:
            w, gq2, gk2 = _qkv_weights(w_qkv[jm], g_q[jm], g_k[jm])
            cos, sin = _rope_tables(max(sp, ss))
            w_o = w_attn_out[jm].astype(BF16)
            mixed = []
            for h, (batch, seq) in zip(hs, geom):
                qkv = _qkv_proj(h, g_i, w, gq2, gk2, cos, sin, seq)
                mixed.append(_matmul_residual(_attention(qkv, 0, batch, seq), w_o, h))
        else:
            cc, sc = _dft_tables(gd, 1.0 / float(np.sqrt(gd)), -1.0)
            w_o = w_fourier_out[jm].astype(BF16)
            mixed = []
            for h, (batch, seq) in zip(hs, geom):
                u, v = _chan_dft(h, g_i, cc, sc)
                cs, sn = _dft_tables(seq, 1.0 / float(np.sqrt(seq)), 1.0)
                mixed.append(_matmul_residual(_seq_dft(u, v, cs, sn, 0, batch, seq), w_o, h))
        hs = _hier_moe(tuple(mixed), g_ffn[i][None, :], w_route_group[i], b_route_group[i], w_route_expert[i],
                       b_route_expert[i], w_exp_gate, w_exp_up, w_exp_down, i)
        w_pg = w_ple_gate[i].astype(BF16)
        w_pp = w_ple_proj[i].astype(BF16)
        g_f = g_final[None, :] if i == depth - 1 else None
        hs = tuple(_ple(h, g_ple[i][None, :], w_pg, p[i].reshape(h.shape[0], -1), w_pp, g_f)
                   for h, p in zip(hs, ps))
    y_sample, y_prompt = hs
    return (y_prompt.reshape(bp, sp, D), y_sample.reshape(bs, ss, D))
```

```python
import functools

import jax
import jax.numpy as jnp
import numpy as np
from jax import lax
from jax.experimental import pallas as pl
from jax.experimental.pallas import tpu as pltpu

F32 = jnp.float32
BF16 = jnp.bfloat16
U32 = jnp.uint32

HEAD_DIM = 128
N_HEADS = 16
N_KV_HEADS = 8
Q_PER_KV = N_HEADS // N_KV_HEADS
ROPE_AXIS_DIM = HEAD_DIM // 2
ROPE_THETA = 10000.0
GRID_W = 64
N_FOURIER_GROUPS = 8
N_EXPERT_GROUPS = 4
EXPERTS_PER_GROUP = 8
N_EXPERTS = N_EXPERT_GROUPS * EXPERTS_PER_GROUP
TOP_K = 2
NORM_EPS = 1e-6

LANES = 128
SUBLANES = 8
BF16_ROWS = 16
MXU_DIM = 256
VMEM_LIMIT = 56 * 1024 * 1024

TM = 512
TQ = 256
TKV = 1024
EXPERT_BLOCK = 256
MOVE_ROWS = 256
DFT_TM, DFT_TN, DFT_TK = 1024, 1024, 2048


def _params(*sem):
    return pltpu.CompilerParams(dimension_semantics=sem, vmem_limit_bytes=VMEM_LIMIT)


def _resident(shape):
    return pl.BlockSpec(shape, lambda *_: (0,) * len(shape), pipeline_mode=pl.Buffered(1))


def _rms(x, g):
    ms = jnp.mean(x * x, axis=-1, keepdims=True)
    return x * lax.rsqrt(ms + NORM_EPS) * g


def _pack_rows(x, ref, rows):
    half = SUBLANES * LANES
    for c in range(SUBLANES):
        lo = x[:, c * LANES:(c + 1) * LANES].astype(BF16).astype(F32)
        hi = x[:, half + c * LANES:half + (c + 1) * LANES].astype(BF16).astype(F32)
        word = (pltpu.bitcast(lo, U32) >> 16) | (pltpu.bitcast(hi, U32) & jnp.uint32(0xFFFF0000))
        ref[pl.ds(c, rows, stride=SUBLANES), :] = word


def _unpack_rows(ref, rows, c):
    word = ref[pl.ds(c, rows, stride=SUBLANES), :]
    lo = pltpu.bitcast(word << 16, F32)
    hi = pltpu.bitcast(word & jnp.uint32(0xFFFF0000), F32)
    return lo, hi


def _qkv_kernel(x_ref, g_ref, w_ref, gq_ref, gk_ref, cos_ref, sin_ref, o_ref, *, d_q, d_qk):
    a = _rms(x_ref[...], g_ref[...]).astype(BF16)
    cos = cos_ref[...]
    sin = sin_ref[...]
    q_scale = float(np.log2(np.e) / np.sqrt(HEAD_DIM))
    tabs = {
        "q": (gq_ref[0:1, :] * cos * q_scale, gq_ref[1:2, :] * sin * q_scale),
        "k": (gk_ref[0:1, :] * cos, gk_ref[1:2, :] * sin),
    }
    n_out = o_ref.shape[1]
    for c0 in range(0, n_out, MXU_DIM):
        acc = jnp.dot(a, w_ref[:, c0:c0 + MXU_DIM], preferred_element_type=F32)
        if c0 >= d_qk:
            o_ref[:, c0:c0 + MXU_DIM] = acc.astype(BF16)
            continue
        t1, t2 = tabs["q" if c0 < d_q else "k"]
        for hh in range(MXU_DIM // HEAD_DIM):
            y = acc[:, hh * HEAD_DIM:(hh + 1) * HEAD_DIM]
            r = lax.rsqrt(jnp.mean(y * y, axis=-1, keepdims=True) + NORM_EPS)
            rot = (y * t1 + pltpu.roll(y, HEAD_DIM // 2, 1) * t2) * r
            o_ref[:, c0 + hh * HEAD_DIM:c0 + (hh + 1) * HEAD_DIM] = rot.astype(BF16)


def _permute_heads(x):
    lead = x.shape[:-1]
    x = x.reshape(lead + (-1, 2, 2, HEAD_DIM // 4))
    return jnp.swapaxes(x, -3, -2).reshape(lead + (-1,))


def _rope_tables(n_pos):
    pos = jnp.arange(n_pos, dtype=jnp.int32)
    row = (pos // GRID_W).astype(F32)
    col = (pos % GRID_W).astype(F32)
    inv_freq = ROPE_THETA ** (-jnp.arange(0, ROPE_AXIS_DIM, 2, dtype=F32) / ROPE_AXIS_DIM)
    ang_r = row[:, None] * inv_freq[None, :]
    ang_c = col[:, None] * inv_freq[None, :]
    cos = jnp.concatenate([jnp.cos(ang_r), jnp.cos(ang_c), jnp.cos(ang_r), jnp.cos(ang_c)], axis=-1)
    sin = jnp.concatenate([-jnp.sin(ang_r), -jnp.sin(ang_c), jnp.sin(ang_r), jnp.sin(ang_c)], axis=-1)
    return cos, sin


def _qkv_weights(w_qkv, g_q, g_k):
    d_qk = (N_HEADS + N_KV_HEADS) * HEAD_DIM
    w = jnp.concatenate([_permute_heads(w_qkv[:, :d_qk]).astype(BF16), w_qkv[:, d_qk:].astype(BF16)], axis=1)
    gq = _permute_heads(g_q)
    gk = _permute_heads(g_k)
    gq2 = jnp.stack([gq, jnp.roll(gq, HEAD_DIM // 2)])
    gk2 = jnp.stack([gk, jnp.roll(gk, HEAD_DIM // 2)])
    return w, gq2, gk2


def _qkv_proj(h, g, w, gq2, gk2, cos, sin, seq):
    T, D = h.shape
    N = w.shape[1]
    assert seq % TM == 0 and T % seq == 0
    d_q = N_HEADS * HEAD_DIM
    d_qk = d_q + N_KV_HEADS * HEAD_DIM

    def pos_map(i):
        return (i % (seq // TM), 0)

    kern = functools.partial(_qkv_kernel, d_q=d_q, d_qk=d_qk)
    return pl.pallas_call(
        kern,
        out_shape=jax.ShapeDtypeStruct((T, N), BF16),
        grid=(T // TM,),
        in_specs=[
            pl.BlockSpec((TM, D), lambda i: (i, 0)),
            _resident((1, D)),
            _resident((D, N)),
            _resident((2, HEAD_DIM)),
            _resident((2, HEAD_DIM)),
            pl.BlockSpec((TM, HEAD_DIM), pos_map),
            pl.BlockSpec((TM, HEAD_DIM), pos_map),
        ],
        out_specs=pl.BlockSpec((TM, N), lambda i: (i, 0)),
        compiler_params=_params("parallel"),
        name="qkv_proj",
    )(h, g, w, gq2, gk2, cos, sin)


def _attn_kernel(q_ref, k_ref, v_ref, o_ref, *, tq, tk, seq):
    q = q_ref[...]
    q2 = jnp.concatenate([q[:, :HEAD_DIM], q[:, HEAD_DIM:]], axis=0)
    m = jnp.full((2 * tq, 1), -jnp.inf, F32)
    l = jnp.zeros((2 * tq, 1), F32)
    acc = jnp.zeros((2 * tq, HEAD_DIM), F32)
    for c in range(seq // tk):
        k = k_ref[c * tk:(c + 1) * tk, :]
        v = v_ref[c * tk:(c + 1) * tk, :]
        s = lax.dot_general(q2, k, (((1,), (1,)), ((), ())), preferred_element_type=F32)
        m_new = jnp.maximum(m, jnp.max(s, axis=-1, keepdims=True))
        alpha = jnp.exp2(m - m_new)
        p = jnp.exp2(s - m_new)
        l = alpha * l + jnp.sum(p, axis=-1, keepdims=True)
        acc = alpha * acc + jnp.dot(p.astype(BF16), v, preferred_element_type=F32)
        m = m_new
    o = acc / l
    o_ref[:, :HEAD_DIM] = o[:tq].astype(BF16)
    o_ref[:, HEAD_DIM:] = o[tq:].astype(BF16)


def _attention(qkv, row0, batch, seq):
    assert row0 % seq == 0 and seq % TQ == 0 and seq % TKV == 0
    qw = Q_PER_KV * HEAD_DIM
    kern = functools.partial(_attn_kernel, tq=TQ, tk=TKV, seq=seq)
    q_blk0 = row0 // TQ
    s_blk0 = row0 // seq
    k_col0 = N_HEADS
    v_col0 = N_HEADS + N_KV_HEADS
    return pl.pallas_call(
        kern,
        out_shape=jax.ShapeDtypeStruct((batch * seq, N_HEADS * HEAD_DIM), BF16),
        grid=(batch, N_KV_HEADS, seq // TQ),
        in_specs=[
            pl.BlockSpec((TQ, qw), lambda b, h, i: (q_blk0 + b * (seq // TQ) + i, h)),
            pl.BlockSpec((seq, HEAD_DIM), lambda b, h, i: (s_blk0 + b, k_col0 + h)),
            pl.BlockSpec((seq, HEAD_DIM), lambda b, h, i: (s_blk0 + b, v_col0 + h)),
        ],
        out_specs=pl.BlockSpec((TQ, qw), lambda b, h, i: (b * (seq // TQ) + i, h)),
        compiler_params=_params("parallel", "parallel", "arbitrary"),
        name="attention",
    )(qkv, qkv, qkv)


def _mm_res_kernel(x_ref, w_ref, r_ref, o_ref):
    o_ref[...] = r_ref[...] + jnp.dot(x_ref[...], w_ref[...], preferred_element_type=F32)


def _matmul_residual(x, w, res):
    T, K = x.shape
    N = w.shape[1]
    return pl.pallas_call(
        _mm_res_kernel,
        out_shape=jax.ShapeDtypeStruct((T, N), F32),
        grid=(T // TM,),
        in_specs=[
            pl.BlockSpec((TM, K), lambda i: (i, 0)),
            _resident((K, N)),
            pl.BlockSpec((TM, N), lambda i: (i, 0)),
        ],
        out_specs=pl.BlockSpec((TM, N), lambda i: (i, 0)),
        compiler_params=_params("parallel"),
        name="matmul_residual",
    )(x, w, res)


R_E1, R_E2, R_W1, R_W2, R_RANK1, R_RANK2 = range(6)
ROUTE_LOGIT_E0 = N_EXPERT_GROUPS


def _route_kernel(x0_ref, x1_ref, g_ref, w_ref, b_ref, tri_ref, r_ref, cnt_ref, a_ref, carry_ref, *, n0):
    i = pl.program_id(0)

    @pl.when(i == 0)
    def _():
        carry_ref[...] = jnp.zeros_like(carry_ref)

    a = _rms(jnp.where(i < n0, x0_ref[...], x1_ref[...]), g_ref[...])
    _pack_rows(a, a_ref, a.shape[0])
    a_hi = a.astype(BF16)
    a_lo = (a - a_hi.astype(F32)).astype(BF16)
    hi_both = jnp.dot(a_hi, w_ref[...], preferred_element_type=F32)
    logits = (hi_both[:, :LANES]
              + (jnp.dot(a_lo, w_ref[:, :LANES], preferred_element_type=F32) + hi_both[:, LANES:])) + b_ref[...]
    lane = lax.broadcasted_iota(jnp.int32, logits.shape, 1)
    lane_f = lane.astype(F32)
    neg = -jnp.inf
    far = float(LANES)

    def first_max(vals):
        top = jnp.max(vals, axis=-1, keepdims=True)
        idx = jnp.min(jnp.where(vals == top, lane_f, far), axis=-1, keepdims=True)
        return top, idx.astype(jnp.int32)

    gmask = lane < N_EXPERT_GROUPS
    gtop, g_sel = first_max(jnp.where(gmask, logits, neg))
    pg = 1.0 / jnp.sum(jnp.where(gmask, jnp.exp(logits - gtop), 0.0), axis=-1, keepdims=True)
    lo = ROUTE_LOGIT_E0 + g_sel * EXPERTS_PER_GROUP
    le = jnp.where((lane >= lo) & (lane < lo + EXPERTS_PER_GROUP), logits, neg)
    t1, i1 = first_max(le)
    t2, i2 = first_max(jnp.where(lane == i1, neg, le))
    r21 = jnp.exp(t2 - t1)
    w1 = pg / (1.0 + r21)
    w2 = pg * r21 / (1.0 + r21)
    e1 = i1 - ROUTE_LOGIT_E0
    e2 = i2 - ROUTE_LOGIT_E0

    hit1 = lane == e1
    hit2 = lane == e2
    onehot = (hit1 | hit2).astype(F32)
    before = jnp.dot(tri_ref[...], onehot.astype(BF16), preferred_element_type=F32) + carry_ref[0:1, :]
    rank1 = jnp.sum(jnp.where(hit1, before, 0.0), axis=-1, keepdims=True)
    rank2 = jnp.sum(jnp.where(hit2, before, 0.0), axis=-1, keepdims=True)
    carry_ref[...] = carry_ref[...] + jnp.sum(onehot, axis=0, keepdims=True)
    cnt_ref[...] = carry_ref[...]

    rec = jnp.zeros(logits.shape, F32)
    for pos, val in ((R_E1, e1.astype(F32)), (R_E2, e2.astype(F32)), (R_W1, w1), (R_W2, w2),
                     (R_RANK1, rank1), (R_RANK2, rank2)):
        rec = jnp.where(lane == pos, val, rec)
    r_ref[...] = rec


def _route(hs, g, w_r, b_r):
    h0, h1 = hs
    D = h0.shape[1]
    T = h0.shape[0] + h1.shape[0]
    assert D == 2 * SUBLANES * LANES and h0.shape[0] % TM == 0 and h1.shape[0] % TM == 0
    n0 = h0.shape[0] // TM
    tri = (jnp.arange(TM)[:, None] > jnp.arange(TM)[None, :]).astype(BF16)
    w_hi = w_r.astype(BF16)
    w_lo = (w_r - w_hi.astype(F32)).astype(BF16)
    kern = functools.partial(_route_kernel, n0=n0)
    return pl.pallas_call(
        kern,
        out_shape=(jax.ShapeDtypeStruct((T, LANES), F32), jax.ShapeDtypeStruct((SUBLANES, LANES), F32),
                   jax.ShapeDtypeStruct((T * SUBLANES, LANES), U32)),
        grid=(T // TM,),
        in_specs=[
            pl.BlockSpec((TM, D), lambda i: (jnp.minimum(i, n0 - 1), 0)),
            pl.BlockSpec((TM, D), lambda i: (jnp.maximum(i - n0, 0), 0)),
            _resident((1, D)),
            _resident((D, 2 * LANES)),
            _resident((1, LANES)),
            _resident((TM, TM)),
        ],
        out_specs=(pl.BlockSpec((TM, LANES), lambda i: (i, 0)),
                   pl.BlockSpec((SUBLANES, LANES), lambda i: (0, 0)),
                   pl.BlockSpec((TM * SUBLANES, LANES), lambda i: (i, 0))),
        scratch_shapes=[pltpu.VMEM((SUBLANES, LANES), F32)],
        compiler_params=_params("arbitrary"),
        name="moe_route",
    )(h0, h1, g, jnp.concatenate([w_hi, w_lo], axis=1), b_r, tri)


def _tile_copy(src, src_row, dst, dst_row, sem):
    return pltpu.make_async_copy(src.at[pl.ds(pl.multiple_of(src_row * SUBLANES, SUBLANES), SUBLANES), :],
                                 dst.at[pl.ds(pl.multiple_of(dst_row * SUBLANES, SUBLANES), SUBLANES), :], sem)


def _dispatch_kernel(dest_ref, fill_ref, a_ref, xb_hbm, zero_ref, sem, *, rows, n_blocks):
    i = pl.program_id(0)
    base = i * rows

    def start(r, c):
        for k in range(TOP_K):
            _tile_copy(a_ref, r, xb_hbm, dest_ref[TOP_K * (base + r) + k], sem).start(priority=k)
        return c

    lax.fori_loop(0, rows, start, 0, unroll=8)
    for k in range(TOP_K):
        pltpu.make_async_copy(a_ref, xb_hbm.at[pl.ds(0, rows * SUBLANES), :], sem).wait()

    @pl.when(i == pl.num_programs(0) - 1)
    def _():
        zero_ref[...] = jnp.zeros_like(zero_ref)

        def per_expert(e, c):
            lo = fill_ref[2 * e]
            hi = fill_ref[2 * e + 1]

            def zstart(r, cc):
                _tile_copy(zero_ref, 0, xb_hbm, r, sem).start()
                return cc

            def zwait(r, cc):
                _tile_copy(zero_ref, 0, xb_hbm, r, sem).wait()
                return cc

            lax.fori_loop(lo, hi, zstart, 0)
            lax.fori_loop(lo, hi, zwait, 0)
            return c

        lax.fori_loop(0, N_EXPERTS, per_expert, 0)

        blk = zero_ref.shape[0]

        def block_copy(b):
            return pltpu.make_async_copy(zero_ref, xb_hbm.at[pl.ds(pl.multiple_of(b * blk, blk), blk), :], sem)

        def bstart(b, c):
            block_copy(b).start()
            return c

        def bwait(b, c):
            block_copy(b).wait()
            return c

        first_unused = fill_ref[2 * N_EXPERTS]
        lax.fori_loop(first_unused, n_blocks, bstart, 0)
        lax.fori_loop(first_unused, n_blocks, bwait, 0)


def _dispatch(a_slab, dest, fill, n_blocks, bm):
    T = a_slab.shape[0] // SUBLANES
    rows = MOVE_ROWS
    kern = functools.partial(_dispatch_kernel, rows=rows, n_blocks=n_blocks)
    return pl.pallas_call(
        kern,
        out_shape=jax.ShapeDtypeStruct((n_blocks * bm * SUBLANES, LANES), U32),
        grid_spec=pltpu.PrefetchScalarGridSpec(
            num_scalar_prefetch=2,
            grid=(T // rows,),
            in_specs=[pl.BlockSpec((rows * SUBLANES, LANES), lambda i, d, f: (i, 0))],
            out_specs=pl.BlockSpec(memory_space=pl.ANY),
            scratch_shapes=[pltpu.VMEM((bm * SUBLANES, LANES), U32), pltpu.SemaphoreType.DMA(())],
        ),
        compiler_params=_params("arbitrary"),
        name="moe_dispatch",
    )(dest, fill, a_slab)


def _expert_kernel(be_ref, nu_ref, x_ref, wg_ref, wu_ref, wd_ref, o_ref, x_s, wg_s, wu_s, wd_s, *, bm):
    b = pl.program_id(0)
    prev = be_ref[jnp.maximum(b - 1, 0)]

    @pl.when((b == 0) | (be_ref[b] != prev))
    def _():
        wg_s[...] = wg_ref[0, 0].astype(BF16)
        wu_s[...] = wu_ref[0, 0].astype(BF16)
        wd_s[...] = wd_ref[0, 0].astype(BF16)

    @pl.when(b < nu_ref[0])
    def _():
        half = SUBLANES * LANES
        for c in range(SUBLANES):
            lo, hi = _unpack_rows(x_ref, bm, c)
            x_s[:, c * LANES:(c + 1) * LANES] = lo.astype(BF16)
            x_s[:, half + c * LANES:half + (c + 1) * LANES] = hi.astype(BF16)
        a = x_s[...]
        hg = jnp.dot(a, wg_s[...], preferred_element_type=F32)
        hu = jnp.dot(a, wu_s[...], preferred_element_type=F32)
        h = hg * jax.nn.sigmoid(hg) * hu
        y = jnp.dot(h.astype(BF16), wd_s[...], preferred_element_type=F32)
        _pack_rows(y, o_ref, bm)

    @pl.when(b >= nu_ref[0])
    def _():
        o_ref[...] = jnp.zeros_like(o_ref)


def _experts(xb, w_gate, w_up, w_down, layer, blk_e, n_used):
    bm = EXPERT_BLOCK
    n_blocks = xb.shape[0] // (bm * SUBLANES)
    _, _, D, De = w_gate.shape
    kern = functools.partial(_expert_kernel, bm=bm)
    return pl.pallas_call(
        kern,
        out_shape=jax.ShapeDtypeStruct(xb.shape, U32),
        grid_spec=pltpu.PrefetchScalarGridSpec(
            num_scalar_prefetch=2,
            grid=(n_blocks,),
            in_specs=[
                pl.BlockSpec((bm * SUBLANES, LANES), lambda b, be, nu: (jnp.minimum(b, nu[0] - 1), 0)),
                pl.BlockSpec((1, 1, D, De), lambda b, be, nu: (layer, be[b], 0, 0)),
                pl.BlockSpec((1, 1, D, De), lambda b, be, nu: (layer, be[b], 0, 0)),
                pl.BlockSpec((1, 1, De, D), lambda b, be, nu: (layer, be[b], 0, 0)),
            ],
            out_specs=pl.BlockSpec((bm * SUBLANES, LANES), lambda b, be, nu: (b, 0)),
            scratch_shapes=[pltpu.VMEM((bm, D), BF16), pltpu.VMEM((D, De), BF16),
                            pltpu.VMEM((D, De), BF16), pltpu.VMEM((De, D), BF16)],
        ),
        compiler_params=_params("arbitrary"),
        name="moe_experts",
    )(blk_e, n_used, xb, w_gate, w_up, w_down)


def _combine_kernel(dest_ref, h_ref, r_ref, yb_hbm, o_ref, y_ref, sem, *, rows, tok0):
    i = pl.program_id(0)

    def issue(step, slot):
        base = tok0 + step * rows

        def start(r, c):
            for k in range(TOP_K):
                _tile_copy(yb_hbm, dest_ref[TOP_K * (base + r) + k], y_ref.at[slot, k], r,
                           sem.at[slot]).start(priority=k)
            return c

        lax.fori_loop(0, rows, start, 0, unroll=8)

    def finish(slot):
        for k in range(TOP_K):
            pltpu.make_async_copy(yb_hbm.at[pl.ds(0, rows * SUBLANES), :], y_ref.at[slot, k], sem.at[slot]).wait()
        rec = r_ref[...]
        w1 = rec[:, R_W1:R_W1 + 1]
        w2 = rec[:, R_W2:R_W2 + 1]
        half = SUBLANES * LANES
        for c in range(SUBLANES):
            lo1, hi1 = _unpack_rows(y_ref.at[slot, 0], rows, c)
            lo2, hi2 = _unpack_rows(y_ref.at[slot, 1], rows, c)
            sl = slice(c * LANES, (c + 1) * LANES)
            sh = slice(half + c * LANES, half + (c + 1) * LANES)
            o_ref[:, sl] = h_ref[:, sl] + (w1 * lo1 + w2 * lo2)
            o_ref[:, sh] = h_ref[:, sh] + (w1 * hi1 + w2 * hi2)

    @pl.when(i == 0)
    def _():
        issue(0, 0)

    for slot in range(2):
        @pl.when(i % 2 == slot)
        def _():
            @pl.when(i + 1 < pl.num_programs(0))
            def _():
                issue(i + 1, 1 - slot)

            finish(slot)


def _combine(h, rec, yb, dest, tok0):
    Tg, D = h.shape
    rows = MOVE_ROWS
    assert tok0 % rows == 0 and Tg % rows == 0
    blk0 = tok0 // rows
    kern = functools.partial(_combine_kernel, rows=rows, tok0=tok0)
    return pl.pallas_call(
        kern,
        out_shape=jax.ShapeDtypeStruct((Tg, D), F32),
        grid_spec=pltpu.PrefetchScalarGridSpec(
            num_scalar_prefetch=1,
            grid=(Tg // rows,),
            in_specs=[
                pl.BlockSpec((rows, D), lambda i, d: (i, 0)),
                pl.BlockSpec((rows, LANES), lambda i, d: (blk0 + i, 0)),
                pl.BlockSpec(memory_space=pl.ANY),
            ],
            out_specs=pl.BlockSpec((rows, D), lambda i, d: (i, 0)),
            scratch_shapes=[pltpu.VMEM((2, TOP_K, rows * SUBLANES, LANES), U32), pltpu.SemaphoreType.DMA((2,))],
        ),
        compiler_params=_params("arbitrary"),
        name="moe_combine",
    )(dest, h, rec, yb)


def _hier_moe(hs, g, w_rg, b_rg, w_re, b_re, w_gate, w_up, w_down, layer):
    D = hs[0].shape[1]
    T = hs[0].shape[0] + hs[1].shape[0]
    pad = LANES - N_EXPERT_GROUPS - N_EXPERTS
    w_r = jnp.concatenate([w_rg, w_re, jnp.zeros((D, pad), F32)], axis=1)
    b_r = jnp.concatenate([b_rg, b_re, jnp.zeros((pad,), F32)])[None, :]
    rec, cnt, a_slab = _route(hs, g, w_r, b_r)

    bm = EXPERT_BLOCK
    e_tok = rec[:, R_E1:R_E2 + 1].astype(jnp.int32)
    rank = rec[:, R_RANK1:R_RANK2 + 1].astype(jnp.int32)
    counts = cnt[0, :N_EXPERTS].astype(jnp.int32)
    pad_counts = (counts + bm - 1) // bm * bm
    pad_end = jnp.cumsum(pad_counts)
    pad_start = pad_end - pad_counts
    start_tok = jnp.sum(jnp.where(e_tok[..., None] == jnp.arange(N_EXPERTS, dtype=jnp.int32), pad_start, 0), axis=-1)
    dest = (start_tok + rank).reshape(-1)
    n_blocks = (T * TOP_K) // bm + N_EXPERTS
    blk_start = jnp.arange(n_blocks, dtype=jnp.int32) * bm
    blk_e = jnp.minimum(jnp.sum(blk_start[:, None] >= pad_end[None, :], axis=1), N_EXPERTS - 1).astype(jnp.int32)
    n_used = (pad_end[-1:] // bm).astype(jnp.int32)
    fill = jnp.concatenate([jnp.stack([pad_start + counts, pad_end], axis=1).reshape(-1), n_used]).astype(jnp.int32)

    xb = _dispatch(a_slab, dest, fill, n_blocks, bm)
    yb = _experts(xb, w_gate, w_up, w_down, layer, blk_e, n_used)
    return (_combine(hs[0], rec, yb, dest, 0), _combine(hs[1], rec, yb, dest, hs[0].shape[0]))


def _ple_math(h_ref, g_ref, wg_ref, p_ref, wp_ref):
    h = h_ref[...]
    a = _rms(h, g_ref[...]).astype(BF16)
    gate = jax.nn.sigmoid(jnp.dot(a, wg_ref[...], preferred_element_type=F32))
    emb = jnp.dot(p_ref[...].astype(BF16), wp_ref[...], preferred_element_type=F32)
    return h + gate * emb


def _ple_kernel(h_ref, g_ref, wg_ref, p_ref, wp_ref, o_ref):
    o_ref[...] = _ple_math(h_ref, g_ref, wg_ref, p_ref, wp_ref)


def _ple_final_kernel(h_ref, g_ref, wg_ref, p_ref, wp_ref, gf_ref, o_ref):
    o_ref[...] = _rms(_ple_math(h_ref, g_ref, wg_ref, p_ref, wp_ref), gf_ref[...])


def _ple(h, g, w_gate, p, layer, w_proj, g_final=None):
    T, D = h.shape
    P = p.shape[2]
    in_specs = [
        pl.BlockSpec((TM, D), lambda i: (i, 0)),
        _resident((1, D)),
        _resident((D, D)),
        pl.BlockSpec((None, TM, P), lambda i: (layer, i, 0)),
        _resident((P, D)),
    ]
    args = (h, g, w_gate, p, w_proj)
    if g_final is not None:
        in_specs.append(_resident((1, D)))
        args += (g_final,)
    return pl.pallas_call(
        _ple_kernel if g_final is None else _ple_final_kernel,
        out_shape=jax.ShapeDtypeStruct((T, D), F32),
        grid=(T // TM,),
        in_specs=in_specs,
        out_specs=pl.BlockSpec((TM, D), lambda i: (i, 0)),
        compiler_params=_params("parallel"),
        name="ple" if g_final is None else "ple_final",
    )(*args)


def _dft_tables(n, scale, rows=None):
    blk = 64
    assert n % blk == 0
    rows = n if rows is None else rows
    j = jnp.arange(rows, dtype=jnp.int32)

    def cos_sin(k):
        ang = ((j[:, None] * k[None, :]) % n).astype(F32) * (2.0 * np.pi / n)
        return jnp.cos(ang), jnp.sin(ang)

    ca, sa = cos_sin(jnp.arange(n // blk, dtype=jnp.int32) * blk)
    cb, sb = cos_sin(jnp.arange(blk, dtype=jnp.int32))
    ca, sa, cb, sb = ca[:, :, None], sa[:, :, None], cb[:, None, :], sb[:, None, :]
    c = (ca * cb - sa * sb) * scale
    s = (sa * cb + ca * sb) * scale
    return c.reshape(rows, n).astype(BF16), s.reshape(rows, n).astype(BF16)


def _norm_nyq_kernel(x_ref, g_ref, a_ref, nyq_ref):
    a = _rms(x_ref[...], g_ref[...]).astype(BF16)
    a_ref[...] = a
    row = lax.broadcasted_iota(jnp.int32, (a.shape[0], 1), 0)
    sign = jnp.where(row % 2 == 0, 1.0, -1.0)
    part = jnp.sum(a.astype(F32) * sign, axis=0, keepdims=True)

    @pl.when(pl.program_id(1) == 0)
    def _():
        nyq_ref[...] = jnp.zeros_like(nyq_ref)

    nyq_ref[...] += jnp.broadcast_to(part, nyq_ref.shape[1:])[None]


def _norm_nyq(h, g, batch, seq):
    T, D = h.shape
    nt = seq // TM
    return pl.pallas_call(
        _norm_nyq_kernel,
        out_shape=(jax.ShapeDtypeStruct((T, D), BF16), jax.ShapeDtypeStruct((batch, BF16_ROWS, D), F32)),
        grid=(batch, nt),
        in_specs=[pl.BlockSpec((TM, D), lambda b, i: (b * nt + i, 0)), _resident((1, D))],
        out_specs=(pl.BlockSpec((TM, D), lambda b, i: (b * nt + i, 0)),
                   pl.BlockSpec((1, BF16_ROWS, D), lambda b, i: (b, 0, 0))),
        compiler_params=_params("parallel", "arbitrary"),
        name="fourier_norm",
    )(h, g)


def _half_dft_kernel(c_ref, s_ref, a_ref, p_ref, q_ref, accp_ref, accq_ref):
    k = pl.program_id(3)

    @pl.when(k == 0)
    def _():
        accp_ref[...] = jnp.zeros_like(accp_ref)
        accq_ref[...] = jnp.zeros_like(accq_ref)

    a = a_ref[...]
    accp_ref[...] += jnp.dot(c_ref[...], a, preferred_element_type=F32)
    accq_ref[...] += jnp.dot(s_ref[...], a, preferred_element_type=F32)

    @pl.when(k == pl.num_programs(3) - 1)
    def _():
        p_ref[...] = accp_ref[...].astype(BF16)
        q_ref[...] = accq_ref[...].astype(BF16)


def _half_dft(a, c, s, batch, seq):
    D = a.shape[1]
    half = seq // 2
    tm, tn, tk = min(DFT_TM, half), DFT_TN, min(DFT_TK, seq)
    out = jax.ShapeDtypeStruct((batch * half, D), BF16)
    ospec = pl.BlockSpec((tm, tn), lambda b, i, j, k: (b * (half // tm) + i, j))
    return pl.pallas_call(
        _half_dft_kernel,
        out_shape=(out, out),
        grid=(batch, half // tm, D // tn, seq // tk),
        in_specs=[
            pl.BlockSpec((tm, tk), lambda b, i, j, k: (i, k)),
            pl.BlockSpec((tm, tk), lambda b, i, j, k: (i, k)),
            pl.BlockSpec((tk, tn), lambda b, i, j, k: (b * (seq // tk) + k, j)),
        ],
        out_specs=(ospec, ospec),
        scratch_shapes=[pltpu.VMEM((tm, tn), F32), pltpu.VMEM((tm, tn), F32)],
        compiler_params=_params("parallel", "parallel", "parallel", "arbitrary"),
        name="fourier_half_dft",
    )(c, s, a)


def _chan_dft(p, q, c_ref, s_ref, sign, gd):
    outs = []
    for grp in range(p.shape[1] // gd):
        sl = slice(grp * gd, (grp + 1) * gd)
        pc = jnp.dot(p[:, sl], c_ref[...], preferred_element_type=F32)
        qs = jnp.dot(q[:, sl], s_ref[...], preferred_element_type=F32)
        outs.append(pc + sign * qs)
    return jnp.concatenate(outs, axis=1)


def _fourier_out_kernel(pa_ref, qa_ref, pb_ref, qb_ref, nyq_ref, c_ref, s_ref, rev_ref, w_ref, h_ref, o_ref,
                        *, n_lo, gd):
    i = pl.program_id(1)

    @pl.when(i < n_lo)
    def _():
        f = _chan_dft(pa_ref[...], qa_ref[...], c_ref, s_ref, -1.0, gd)
        o_ref[...] = h_ref[...] + jnp.dot(f.astype(BF16), w_ref[...], preferred_element_type=F32)

    @pl.when(i >= n_lo)
    def _():
        m = _chan_dft(pa_ref[...], qa_ref[...], c_ref, s_ref, 1.0, gd).astype(BF16)
        rev = jnp.dot(rev_ref[...], m, preferred_element_type=F32)
        mb = _chan_dft(pb_ref[...], qb_ref[...], c_ref, s_ref, 1.0, gd)[0:1, :]
        nyq = nyq_ref[0].astype(BF16)
        mn = _chan_dft(nyq, jnp.zeros_like(nyq), c_ref, s_ref, 1.0, gd)[0:1, :]
        first = jnp.where(i == n_lo, mn, mb)
        row = lax.broadcasted_iota(jnp.int32, rev.shape, 0)
        f = jnp.where(row == 0, first, rev)
        o_ref[...] = h_ref[...] + jnp.dot(f.astype(BF16), w_ref[...], preferred_element_type=F32)


def _fourier_out(p, q, nyq, cc, sc, w, h, batch, seq):
    D = h.shape[1]
    gd = cc.shape[0]
    tm = TM
    half = seq // 2
    assert half % tm == 0
    n_lo = half // tm
    nt = seq // tm
    r = jnp.arange(tm)
    rev = ((r[:, None] + r[None, :]) == tm).astype(BF16)

    def a_map(b, i):
        return (b * n_lo + jnp.where(i < n_lo, i, nt - 1 - i), 0)

    def b_map(b, i):
        t = jnp.where(i < n_lo, i, jnp.minimum(nt - i, n_lo - 1))
        return ((b * n_lo + t) * (tm // BF16_ROWS), 0)

    kern = functools.partial(_fourier_out_kernel, n_lo=n_lo, gd=gd)
    return pl.pallas_call(
        kern,
        out_shape=jax.ShapeDtypeStruct(h.shape, F32),
        grid=(batch, nt),
        in_specs=[
            pl.BlockSpec((tm, D), a_map),
            pl.BlockSpec((tm, D), a_map),
            pl.BlockSpec((BF16_ROWS, D), b_map),
            pl.BlockSpec((BF16_ROWS, D), b_map),
            pl.BlockSpec((1, BF16_ROWS, D), lambda b, i: (b, 0, 0)),
            _resident((gd, gd)),
            _resident((gd, gd)),
            _resident((tm, tm)),
            _resident((D, D)),
            pl.BlockSpec((tm, D), lambda b, i: (b * nt + i, 0)),
        ],
        out_specs=pl.BlockSpec((tm, D), lambda b, i: (b * nt + i, 0)),
        compiler_params=_params("parallel", "parallel"),
        name="fourier_out",
    )(p, q, p, q, nyq, cc, sc, rev, w, h)


def _fourier_mix(h, g, w_out, cc, sc, batch, seq):
    scale = 1.0 / float(np.sqrt(seq))
    cs, sn = _dft_tables(seq, scale, rows=seq // 2)
    a, nyq = _norm_nyq(h, g, batch, seq)
    p, q = _half_dft(a, cs, sn, batch, seq)
    return _fourier_out(p, q, nyq * scale, cc, sc, w_out, h, batch, seq)


def kernel(x_prompt, x_sample, p_prompt, p_sample, g_mix, w_qkv, g_q, g_k, w_attn_out, w_fourier_out, g_ffn, w_route_group, b_route_group, w_route_expert, b_route_expert, w_exp_gate, w_exp_up, w_exp_down, g_ple, w_ple_gate, w_ple_proj, g_final):
    depth = g_mix.shape[0]
    bp, sp, D = x_prompt.shape
    bs, ss, _ = x_sample.shape
    tp, ts = bp * sp, bs * ss
    gd = D // N_FOURIER_GROUPS
    geom = ((bs, ss), (bp, sp))
    hs = (x_sample.reshape(ts, D), x_prompt.reshape(tp, D))
    ps = (p_sample, p_prompt)

    n_mixers = 2
    for i in range(depth):
        jm = i // n_mixers
        g_i = g_mix[i][None, :]
        if i % n_mixers == 0:
            w, gq2, gk2 = _qkv_weights(w_qkv[jm], g_q[jm], g_k[jm])
            cos, sin = _rope_tables(max(sp, ss))
            w_o = w_attn_out[jm].astype(BF16)
            mixed = []
            for h, (batch, seq) in zip(hs, geom):
                qkv = _qkv_proj(h, g_i, w, gq2, gk2, cos, sin, seq)
                mixed.append(_matmul_residual(_attention(qkv, 0, batch, seq), w_o, h))
        else:
            cc, sc = _dft_tables(gd, 1.0 / float(np.sqrt(gd)))
            w_o = w_fourier_out[jm].astype(BF16)
            mixed = [_fourier_mix(h, g_i, w_o, cc, sc, batch, seq) for h, (batch, seq) in zip(hs, geom)]
        hs = _hier_moe(tuple(mixed), g_ffn[i][None, :], w_route_group[i], b_route_group[i], w_route_expert[i],
                       b_route_expert[i], w_exp_gate, w_exp_up, w_exp_down, i)
        w_pg = w_ple_gate[i].astype(BF16)
        w_pp = w_ple_proj[i].astype(BF16)
        g_f = g_final[None, :] if i == depth - 1 else None
        hs = tuple(_ple(h, g_ple[i][None, :], w_pg, p.reshape(depth, h.shape[0], -1), i, w_pp, g_f)
                   for h, p in zip(hs, ps))
    y_sample, y_prompt = hs
    return (y_prompt.reshape(bp, sp, D), y_sample.reshape(bs, ss, D))
```

```python
import functools

import jax
import jax.numpy as jnp
import numpy as np
from jax import lax
from jax.experimental import pallas as pl
from jax.experimental.pallas import tpu as pltpu

F32 = jnp.float32
BF16 = jnp.bfloat16
U32 = jnp.uint32

HEAD_DIM = 128
N_HEADS = 16
N_KV_HEADS = 8
Q_PER_KV = N_HEADS // N_KV_HEADS
ROPE_AXIS_DIM = HEAD_DIM // 2
ROPE_THETA = 10000.0
GRID_W = 64
N_FOURIER_GROUPS = 8
N_EXPERT_GROUPS = 4
EXPERTS_PER_GROUP = 8
N_EXPERTS = N_EXPERT_GROUPS * EXPERTS_PER_GROUP
TOP_K = 2
NORM_EPS = 1e-6

LANES = 128
SUBLANES = 8
BF16_ROWS = 16
MXU_DIM = 256
VMEM_LIMIT = 56 * 1024 * 1024

TM = 512
TQ = 256
TKV = 1024
EXPERT_BLOCK = 256
MOVE_ROWS = 256
DFT_TM, DFT_TN, DFT_TK = 1024, 1024, 2048


def _params(*sem):
    return pltpu.CompilerParams(dimension_semantics=sem, vmem_limit_bytes=VMEM_LIMIT)


def _resident(shape):
    return pl.BlockSpec(shape, lambda *_: (0,) * len(shape), pipeline_mode=pl.Buffered(1))


def _rms(x, g):
    ms = jnp.mean(x * x, axis=-1, keepdims=True)
    return x * lax.rsqrt(ms + NORM_EPS) * g


def _pack_rows(x, ref, rows):
    half = SUBLANES * LANES
    for c in range(SUBLANES):
        lo = x[:, c * LANES:(c + 1) * LANES].astype(BF16).astype(F32)
        hi = x[:, half + c * LANES:half + (c + 1) * LANES].astype(BF16).astype(F32)
        word = (pltpu.bitcast(lo, U32) >> 16) | (pltpu.bitcast(hi, U32) & jnp.uint32(0xFFFF0000))
        ref[pl.ds(c, rows, stride=SUBLANES), :] = word


def _unpack_rows(ref, rows, c):
    word = ref[pl.ds(c, rows, stride=SUBLANES), :]
    lo = pltpu.bitcast(word << 16, F32)
    hi = pltpu.bitcast(word & jnp.uint32(0xFFFF0000), F32)
    return lo, hi


def _qkv_kernel(x_ref, g_ref, w_ref, gq_ref, gk_ref, cos_ref, sin_ref, o_ref, *, d_q, d_qk):
    a = _rms(x_ref[...], g_ref[...]).astype(BF16)
    cos = cos_ref[...]
    sin = sin_ref[...]
    q_scale = float(np.log2(np.e) / np.sqrt(HEAD_DIM))
    tabs = {
        "q": (gq_ref[0:1, :] * cos * q_scale, gq_ref[1:2, :] * sin * q_scale),
        "k": (gk_ref[0:1, :] * cos, gk_ref[1:2, :] * sin),
    }
    n_out = o_ref.shape[1]
    for c0 in range(0, n_out, MXU_DIM):
        acc = jnp.dot(a, w_ref[:, c0:c0 + MXU_DIM], preferred_element_type=F32)
        if c0 >= d_qk:
            o_ref[:, c0:c0 + MXU_DIM] = acc.astype(BF16)
            continue
        t1, t2 = tabs["q" if c0 < d_q else "k"]
        for hh in range(MXU_DIM // HEAD_DIM):
            y = acc[:, hh * HEAD_DIM:(hh + 1) * HEAD_DIM]
            r = lax.rsqrt(jnp.mean(y * y, axis=-1, keepdims=True) + NORM_EPS)
            rot = (y * t1 + pltpu.roll(y, HEAD_DIM // 2, 1) * t2) * r
            o_ref[:, c0 + hh * HEAD_DIM:c0 + (hh + 1) * HEAD_DIM] = rot.astype(BF16)


def _permute_heads(x):
    lead = x.shape[:-1]
    x = x.reshape(lead + (-1, 2, 2, HEAD_DIM // 4))
    return jnp.swapaxes(x, -3, -2).reshape(lead + (-1,))


def _rope_tables(n_pos):
    pos = jnp.arange(n_pos, dtype=jnp.int32)
    row = (pos // GRID_W).astype(F32)
    col = (pos % GRID_W).astype(F32)
    inv_freq = ROPE_THETA ** (-jnp.arange(0, ROPE_AXIS_DIM, 2, dtype=F32) / ROPE_AXIS_DIM)
    ang_r = row[:, None] * inv_freq[None, :]
    ang_c = col[:, None] * inv_freq[None, :]
    cos = jnp.concatenate([jnp.cos(ang_r), jnp.cos(ang_c), jnp.cos(ang_r), jnp.cos(ang_c)], axis=-1)
    sin = jnp.concatenate([-jnp.sin(ang_r), -jnp.sin(ang_c), jnp.sin(ang_r), jnp.sin(ang_c)], axis=-1)
    return cos, sin


def _qkv_weights(w_qkv, g_q, g_k):
    d_qk = (N_HEADS + N_KV_HEADS) * HEAD_DIM
    w = jnp.concatenate([_permute_heads(w_qkv[:, :d_qk]).astype(BF16), w_qkv[:, d_qk:].astype(BF16)], axis=1)
    gq = _permute_heads(g_q)
    gk = _permute_heads(g_k)
    gq2 = jnp.stack([gq, jnp.roll(gq, HEAD_DIM // 2)])
    gk2 = jnp.stack([gk, jnp.roll(gk, HEAD_DIM // 2)])
    return w, gq2, gk2


def _qkv_proj(h, g, w, gq2, gk2, cos, sin, seq):
    T, D = h.shape
    N = w.shape[1]
    assert seq % TM == 0 and T % seq == 0
    d_q = N_HEADS * HEAD_DIM
    d_qk = d_q + N_KV_HEADS * HEAD_DIM

    def pos_map(i):
        return (i % (seq // TM), 0)

    kern = functools.partial(_qkv_kernel, d_q=d_q, d_qk=d_qk)
    return pl.pallas_call(
        kern,
        out_shape=jax.ShapeDtypeStruct((T, N), BF16),
        grid=(T // TM,),
        in_specs=[
            pl.BlockSpec((TM, D), lambda i: (i, 0)),
            _resident((1, D)),
            _resident((D, N)),
            _resident((2, HEAD_DIM)),
            _resident((2, HEAD_DIM)),
            pl.BlockSpec((TM, HEAD_DIM), pos_map),
            pl.BlockSpec((TM, HEAD_DIM), pos_map),
        ],
        out_specs=pl.BlockSpec((TM, N), lambda i: (i, 0)),
        compiler_params=_params("parallel"),
        name="qkv_proj",
    )(h, g, w, gq2, gk2, cos, sin)


def _attn_kernel(q_ref, k_ref, v_ref, o_ref, *, tq, tk, seq):
    q = q_ref[...]
    q2 = jnp.concatenate([q[:, :HEAD_DIM], q[:, HEAD_DIM:]], axis=0)
    m = jnp.full((2 * tq, 1), -jnp.inf, F32)
    l = jnp.zeros((2 * tq, 1), F32)
    acc = jnp.zeros((2 * tq, HEAD_DIM), F32)
    for c in range(seq // tk):
        k = k_ref[c * tk:(c + 1) * tk, :]
        v = v_ref[c * tk:(c + 1) * tk, :]
        s = lax.dot_general(q2, k, (((1,), (1,)), ((), ())), preferred_element_type=F32)
        m_new = jnp.maximum(m, jnp.max(s, axis=-1, keepdims=True))
        alpha = jnp.exp2(m - m_new)
        p = jnp.exp2(s - m_new)
        l = alpha * l + jnp.sum(p, axis=-1, keepdims=True)
        acc = alpha * acc + jnp.dot(p.astype(BF16), v, preferred_element_type=F32)
        m = m_new
    o = acc / l
    o_ref[:, :HEAD_DIM] = o[:tq].astype(BF16)
    o_ref[:, HEAD_DIM:] = o[tq:].astype(BF16)


def _attention(qkv, row0, batch, seq):
    assert row0 % seq == 0 and seq % TQ == 0 and seq % TKV == 0
    qw = Q_PER_KV * HEAD_DIM
    kern = functools.partial(_attn_kernel, tq=TQ, tk=TKV, seq=seq)
    q_blk0 = row0 // TQ
    s_blk0 = row0 // seq
    k_col0 = N_HEADS
    v_col0 = N_HEADS + N_KV_HEADS
    return pl.pallas_call(
        kern,
        out_shape=jax.ShapeDtypeStruct((batch * seq, N_HEADS * HEAD_DIM), BF16),
        grid=(batch, N_KV_HEADS, seq // TQ),
        in_specs=[
            pl.BlockSpec((TQ, qw), lambda b, h, i: (q_blk0 + b * (seq // TQ) + i, h)),
            pl.BlockSpec((seq, HEAD_DIM), lambda b, h, i: (s_blk0 + b, k_col0 + h)),
            pl.BlockSpec((seq, HEAD_DIM), lambda b, h, i: (s_blk0 + b, v_col0 + h)),
        ],
        out_specs=pl.BlockSpec((TQ, qw), lambda b, h, i: (b * (seq // TQ) + i, h)),
        compiler_params=_params("parallel", "parallel", "arbitrary"),
        name="attention",
    )(qkv, qkv, qkv)


def _mm_res_kernel(x_ref, w_ref, r_ref, o_ref):
    o_ref[...] = r_ref[...] + jnp.dot(x_ref[...], w_ref[...], preferred_element_type=F32)


def _matmul_residual(x, w, res):
    T, K = x.shape
    N = w.shape[1]
    return pl.pallas_call(
        _mm_res_kernel,
        out_shape=jax.ShapeDtypeStruct((T, N), F32),
        grid=(T // TM,),
        in_specs=[
            pl.BlockSpec((TM, K), lambda i: (i, 0)),
            _resident((K, N)),
            pl.BlockSpec((TM, N), lambda i: (i, 0)),
        ],
        out_specs=pl.BlockSpec((TM, N), lambda i: (i, 0)),
        compiler_params=_params("parallel"),
        name="matmul_residual",
    )(x, w, res)


R_E1, R_E2, R_W1, R_W2, R_RANK1, R_RANK2 = range(6)
ROUTE_LOGIT_E0 = N_EXPERT_GROUPS


def _route_kernel(x0_ref, x1_ref, g_ref, w_ref, b_ref, tri_ref, r_ref, cnt_ref, a_ref, carry_ref, *, n0):
    i = pl.program_id(0)

    @pl.when(i == 0)
    def _():
        carry_ref[...] = jnp.zeros_like(carry_ref)

    a = _rms(jnp.where(i < n0, x0_ref[...], x1_ref[...]), g_ref[...])
    _pack_rows(a, a_ref, a.shape[0])
    a_hi = a.astype(BF16)
    a_lo = (a - a_hi.astype(F32)).astype(BF16)
    hi_both = jnp.dot(a_hi, w_ref[...], preferred_element_type=F32)
    logits = (hi_both[:, :LANES]
              + (jnp.dot(a_lo, w_ref[:, :LANES], preferred_element_type=F32) + hi_both[:, LANES:])) + b_ref[...]
    lane = lax.broadcasted_iota(jnp.int32, logits.shape, 1)
    lane_f = lane.astype(F32)
    neg = -jnp.inf
    far = float(LANES)

    def first_max(vals):
        top = jnp.max(vals, axis=-1, keepdims=True)
        idx = jnp.min(jnp.where(vals == top, lane_f, far), axis=-1, keepdims=True)
        return top, idx.astype(jnp.int32)

    gmask = lane < N_EXPERT_GROUPS
    gtop, g_sel = first_max(jnp.where(gmask, logits, neg))
    pg = 1.0 / jnp.sum(jnp.where(gmask, jnp.exp(logits - gtop), 0.0), axis=-1, keepdims=True)
    lo = ROUTE_LOGIT_E0 + g_sel * EXPERTS_PER_GROUP
    le = jnp.where((lane >= lo) & (lane < lo + EXPERTS_PER_GROUP), logits, neg)
    t1, i1 = first_max(le)
    t2, i2 = first_max(jnp.where(lane == i1, neg, le))
    r21 = jnp.exp(t2 - t1)
    w1 = pg / (1.0 + r21)
    w2 = pg * r21 / (1.0 + r21)
    e1 = i1 - ROUTE_LOGIT_E0
    e2 = i2 - ROUTE_LOGIT_E0

    hit1 = lane == e1
    hit2 = lane == e2
    onehot = (hit1 | hit2).astype(F32)
    before = jnp.dot(tri_ref[...], onehot.astype(BF16), preferred_element_type=F32) + carry_ref[0:1, :]
    rank1 = jnp.sum(jnp.where(hit1, before, 0.0), axis=-1, keepdims=True)
    rank2 = jnp.sum(jnp.where(hit2, before, 0.0), axis=-1, keepdims=True)
    carry_ref[...] = carry_ref[...] + jnp.sum(onehot, axis=0, keepdims=True)
    cnt_ref[...] = carry_ref[...]

    rec = jnp.zeros(logits.shape, F32)
    for pos, val in ((R_E1, e1.astype(F32)), (R_E2, e2.astype(F32)), (R_W1, w1), (R_W2, w2),
                     (R_RANK1, rank1), (R_RANK2, rank2)):
        rec = jnp.where(lane == pos, val, rec)
    r_ref[...] = rec


def _route(hs, g, w_r, b_r):
    h0, h1 = hs
    D = h0.shape[1]
    T = h0.shape[0] + h1.shape[0]
    assert D == 2 * SUBLANES * LANES and h0.shape[0] % TM == 0 and h1.shape[0] % TM == 0
    n0 = h0.shape[0] // TM
    tri = (jnp.arange(TM)[:, None] > jnp.arange(TM)[None, :]).astype(BF16)
    w_hi = w_r.astype(BF16)
    w_lo = (w_r - w_hi.astype(F32)).astype(BF16)
    kern = functools.partial(_route_kernel, n0=n0)
    return pl.pallas_call(
        kern,
        out_shape=(jax.ShapeDtypeStruct((T, LANES), F32), jax.ShapeDtypeStruct((SUBLANES, LANES), F32),
                   jax.ShapeDtypeStruct((T * SUBLANES, LANES), U32)),
        grid=(T // TM,),
        in_specs=[
            pl.BlockSpec((TM, D), lambda i: (jnp.minimum(i, n0 - 1), 0)),
            pl.BlockSpec((TM, D), lambda i: (jnp.maximum(i - n0, 0), 0)),
            _resident((1, D)),
            _resident((D, 2 * LANES)),
            _resident((1, LANES)),
            _resident((TM, TM)),
        ],
        out_specs=(pl.BlockSpec((TM, LANES), lambda i: (i, 0)),
                   pl.BlockSpec((SUBLANES, LANES), lambda i: (0, 0)),
                   pl.BlockSpec((TM * SUBLANES, LANES), lambda i: (i, 0))),
        scratch_shapes=[pltpu.VMEM((SUBLANES, LANES), F32)],
        compiler_params=_params("arbitrary"),
        name="moe_route",
    )(h0, h1, g, jnp.concatenate([w_hi, w_lo], axis=1), b_r, tri)


def _tile_copy(src, src_row, dst, dst_row, sem):
    return pltpu.make_async_copy(src.at[pl.ds(pl.multiple_of(src_row * SUBLANES, SUBLANES), SUBLANES), :],
                                 dst.at[pl.ds(pl.multiple_of(dst_row * SUBLANES, SUBLANES), SUBLANES), :], sem)


def _dispatch_kernel(dest_ref, fill_ref, a_ref, xb_hbm, zero_ref, sem, *, rows, n_blocks):
    i = pl.program_id(0)
    base = i * rows

    def start(r, c):
        for k in range(TOP_K):
            _tile_copy(a_ref, r, xb_hbm, dest_ref[TOP_K * (base + r) + k], sem).start(priority=k)
        return c

    lax.fori_loop(0, rows, start, 0, unroll=8)
    for k in range(TOP_K):
        pltpu.make_async_copy(a_ref, xb_hbm.at[pl.ds(0, rows * SUBLANES), :], sem).wait()

    @pl.when(i == pl.num_programs(0) - 1)
    def _():
        zero_ref[...] = jnp.zeros_like(zero_ref)

        def per_expert(e, c):
            lo = fill_ref[2 * e]
            hi = fill_ref[2 * e + 1]

            def zstart(r, cc):
                _tile_copy(zero_ref, 0, xb_hbm, r, sem).start()
                return cc

            def zwait(r, cc):
                _tile_copy(zero_ref, 0, xb_hbm, r, sem).wait()
                return cc

            lax.fori_loop(lo, hi, zstart, 0)
            lax.fori_loop(lo, hi, zwait, 0)
            return c

        lax.fori_loop(0, N_EXPERTS, per_expert, 0)

        blk = zero_ref.shape[0]

        def block_copy(b):
            return pltpu.make_async_copy(zero_ref, xb_hbm.at[pl.ds(pl.multiple_of(b * blk, blk), blk), :], sem)

        def bstart(b, c):
            block_copy(b).start()
            return c

        def bwait(b, c):
            block_copy(b).wait()
            return c

        first_unused = fill_ref[2 * N_EXPERTS]
        lax.fori_loop(first_unused, n_blocks, bstart, 0)
        lax.fori_loop(first_unused, n_blocks, bwait, 0)


def _dispatch(a_slab, dest, fill, n_blocks, bm):
    T = a_slab.shape[0] // SUBLANES
    rows = MOVE_ROWS
    kern = functools.partial(_dispatch_kernel, rows=rows, n_blocks=n_blocks)
    return pl.pallas_call(
        kern,
        out_shape=jax.ShapeDtypeStruct((n_blocks * bm * SUBLANES, LANES), U32),
        grid_spec=pltpu.PrefetchScalarGridSpec(
            num_scalar_prefetch=2,
            grid=(T // rows,),
            in_specs=[pl.BlockSpec((rows * SUBLANES, LANES), lambda i, d, f: (i, 0))],
            out_specs=pl.BlockSpec(memory_space=pl.ANY),
            scratch_shapes=[pltpu.VMEM((bm * SUBLANES, LANES), U32), pltpu.SemaphoreType.DMA(())],
        ),
        compiler_params=_params("arbitrary"),
        name="moe_dispatch",
    )(dest, fill, a_slab)


M_USED, M_RUNS, M_EXPERT0 = 0, 1, 2


def _expert_kernel(be_ref, run_ref, meta_ref, x_ref, wg_hbm, wu_hbm, wd_hbm, o_ref,
                   x_s, wg_s, wu_s, wd_s, wg_f, wu_f, wd_f, sem, *, bm, layer):
    b = pl.program_id(0)
    n_used = meta_ref[M_USED]
    r = run_ref[b]

    def weight_copies(run, slot):
        e = meta_ref[M_EXPERT0 + run]
        return [pltpu.make_async_copy(w.at[layer, e], f.at[slot], sem.at[slot, j])
                for j, (w, f) in enumerate(((wg_hbm, wg_f), (wu_hbm, wu_f), (wd_hbm, wd_f)))]

    @pl.when((b < n_used) & ((b == 0) | (be_ref[b] != be_ref[jnp.maximum(b - 1, 0)])))
    def _():
        slot = r % 2

        @pl.when(r == 0)
        def _():
            for cp in weight_copies(0, 0):
                cp.start()

        for cp in weight_copies(r, slot):
            cp.wait()
        wg_s[...] = wg_f[slot].astype(BF16)
        wu_s[...] = wu_f[slot].astype(BF16)
        wd_s[...] = wd_f[slot].astype(BF16)

        @pl.when(r + 1 < meta_ref[M_RUNS])
        def _():
            for cp in weight_copies(r + 1, 1 - slot):
                cp.start()

    @pl.when(b < n_used)
    def _():
        half = SUBLANES * LANES
        for c in range(SUBLANES):
            lo, hi = _unpack_rows(x_ref, bm, c)
            x_s[:, c * LANES:(c + 1) * LANES] = lo.astype(BF16)
            x_s[:, half + c * LANES:half + (c + 1) * LANES] = hi.astype(BF16)
        a = x_s[...]
        hg = jnp.dot(a, wg_s[...], preferred_element_type=F32)
        hu = jnp.dot(a, wu_s[...], preferred_element_type=F32)
        h = hg * jax.nn.sigmoid(hg) * hu
        y = jnp.dot(h.astype(BF16), wd_s[...], preferred_element_type=F32)
        _pack_rows(y, o_ref, bm)

    @pl.when(b >= n_used)
    def _():
        o_ref[...] = jnp.zeros_like(o_ref)


def _experts(xb, w_gate, w_up, w_down, layer, blk_e, blk_run, meta):
    bm = EXPERT_BLOCK
    n_blocks = xb.shape[0] // (bm * SUBLANES)
    _, _, D, De = w_gate.shape
    kern = functools.partial(_expert_kernel, bm=bm, layer=layer)
    hbm = pl.BlockSpec(memory_space=pl.ANY)
    return pl.pallas_call(
        kern,
        out_shape=jax.ShapeDtypeStruct(xb.shape, U32),
        grid_spec=pltpu.PrefetchScalarGridSpec(
            num_scalar_prefetch=3,
            grid=(n_blocks,),
            in_specs=[
                pl.BlockSpec((bm * SUBLANES, LANES), lambda b, be, run, meta: (jnp.minimum(b, meta[M_USED] - 1), 0)),
                hbm, hbm, hbm,
            ],
            out_specs=pl.BlockSpec((bm * SUBLANES, LANES), lambda b, be, run, meta: (b, 0)),
            scratch_shapes=[pltpu.VMEM((bm, D), BF16), pltpu.VMEM((D, De), BF16),
                            pltpu.VMEM((D, De), BF16), pltpu.VMEM((De, D), BF16),
                            pltpu.VMEM((2, D, De), F32), pltpu.VMEM((2, D, De), F32), pltpu.VMEM((2, De, D), F32),
                            pltpu.SemaphoreType.DMA((2, 3))],
        ),
        compiler_params=_params("arbitrary"),
        name="moe_experts",
    )(blk_e, blk_run, meta, xb, w_gate, w_up, w_down)


def _combine_kernel(dest_ref, h_ref, r_ref, yb_hbm, o_ref, y_ref, sem, *, rows, tok0):
    i = pl.program_id(0)

    def issue(step, slot):
        base = tok0 + step * rows

        def start(r, c):
            for k in range(TOP_K):
                _tile_copy(yb_hbm, dest_ref[TOP_K * (base + r) + k], y_ref.at[slot, k], r,
                           sem.at[slot]).start(priority=k)
            return c

        lax.fori_loop(0, rows, start, 0, unroll=8)

    def finish(slot):
        for k in range(TOP_K):
            pltpu.make_async_copy(yb_hbm.at[pl.ds(0, rows * SUBLANES), :], y_ref.at[slot, k], sem.at[slot]).wait()
        rec = r_ref[...]
        w1 = rec[:, R_W1:R_W1 + 1]
        w2 = rec[:, R_W2:R_W2 + 1]
        half = SUBLANES * LANES
        for c in range(SUBLANES):
            lo1, hi1 = _unpack_rows(y_ref.at[slot, 0], rows, c)
            lo2, hi2 = _unpack_rows(y_ref.at[slot, 1], rows, c)
            sl = slice(c * LANES, (c + 1) * LANES)
            sh = slice(half + c * LANES, half + (c + 1) * LANES)
            o_ref[:, sl] = h_ref[:, sl] + (w1 * lo1 + w2 * lo2)
            o_ref[:, sh] = h_ref[:, sh] + (w1 * hi1 + w2 * hi2)

    @pl.when(i == 0)
    def _():
        issue(0, 0)

    for slot in range(2):
        @pl.when(i % 2 == slot)
        def _():
            @pl.when(i + 1 < pl.num_programs(0))
            def _():
                issue(i + 1, 1 - slot)

            finish(slot)


def _combine(h, rec, yb, dest, tok0):
    Tg, D = h.shape
    rows = MOVE_ROWS
    assert tok0 % rows == 0 and Tg % rows == 0
    blk0 = tok0 // rows
    kern = functools.partial(_combine_kernel, rows=rows, tok0=tok0)
    return pl.pallas_call(
        kern,
        out_shape=jax.ShapeDtypeStruct((Tg, D), F32),
        grid_spec=pltpu.PrefetchScalarGridSpec(
            num_scalar_prefetch=1,
            grid=(Tg // rows,),
            in_specs=[
                pl.BlockSpec((rows, D), lambda i, d: (i, 0)),
                pl.BlockSpec((rows, LANES), lambda i, d: (blk0 + i, 0)),
                pl.BlockSpec(memory_space=pl.ANY),
            ],
            out_specs=pl.BlockSpec((rows, D), lambda i, d: (i, 0)),
            scratch_shapes=[pltpu.VMEM((2, TOP_K, rows * SUBLANES, LANES), U32), pltpu.SemaphoreType.DMA((2,))],
        ),
        compiler_params=_params("arbitrary"),
        name="moe_combine",
    )(dest, h, rec, yb)


def _hier_moe(hs, g, w_rg, b_rg, w_re, b_re, w_gate, w_up, w_down, layer):
    D = hs[0].shape[1]
    T = hs[0].shape[0] + hs[1].shape[0]
    pad = LANES - N_EXPERT_GROUPS - N_EXPERTS
    w_r = jnp.concatenate([w_rg, w_re, jnp.zeros((D, pad), F32)], axis=1)
    b_r = jnp.concatenate([b_rg, b_re, jnp.zeros((pad,), F32)])[None, :]
    rec, cnt, a_slab = _route(hs, g, w_r, b_r)

    bm = EXPERT_BLOCK
    e_tok = rec[:, R_E1:R_E2 + 1].astype(jnp.int32)
    rank = rec[:, R_RANK1:R_RANK2 + 1].astype(jnp.int32)
    counts = cnt[0, :N_EXPERTS].astype(jnp.int32)
    pad_counts = (counts + bm - 1) // bm * bm
    pad_end = jnp.cumsum(pad_counts)
    pad_start = pad_end - pad_counts
    start_tok = jnp.sum(jnp.where(e_tok[..., None] == jnp.arange(N_EXPERTS, dtype=jnp.int32), pad_start, 0), axis=-1)
    dest = (start_tok + rank).reshape(-1)
    n_blocks = (T * TOP_K) // bm + N_EXPERTS
    blk_start = jnp.arange(n_blocks, dtype=jnp.int32) * bm
    blk_e = jnp.minimum(jnp.sum(blk_start[:, None] >= pad_end[None, :], axis=1), N_EXPERTS - 1).astype(jnp.int32)
    n_used = (pad_end[-1:] // bm).astype(jnp.int32)
    fill = jnp.concatenate([jnp.stack([pad_start + counts, pad_end], axis=1).reshape(-1), n_used]).astype(jnp.int32)
    blk = jnp.arange(n_blocks, dtype=jnp.int32)
    run_starts = (blk < n_used[0]) & ((blk == 0) | (blk_e != jnp.roll(blk_e, 1)))
    blk_run = (jnp.cumsum(run_starts) - 1).astype(jnp.int32)
    e_ids = jnp.arange(N_EXPERTS, dtype=jnp.int32)
    run_experts = jnp.sort(jnp.where(counts > 0, e_ids, e_ids + N_EXPERTS)) % N_EXPERTS
    meta = jnp.concatenate([n_used, jnp.sum(run_starts, keepdims=True), run_experts]).astype(jnp.int32)

    xb = _dispatch(a_slab, dest, fill, n_blocks, bm)
    yb = _experts(xb, w_gate, w_up, w_down, layer, blk_e, blk_run, meta)
    return (_combine(hs[0], rec, yb, dest, 0), _combine(hs[1], rec, yb, dest, hs[0].shape[0]))


def _ple_math(h_ref, g_ref, wg_ref, p_ref, wp_ref):
    h = h_ref[...]
    a = _rms(h, g_ref[...]).astype(BF16)
    gate = jax.nn.sigmoid(jnp.dot(a, wg_ref[...], preferred_element_type=F32))
    emb = jnp.dot(p_ref[...].astype(BF16), wp_ref[...], preferred_element_type=F32)
    return h + gate * emb


def _ple_kernel(h_ref, g_ref, wg_ref, p_ref, wp_ref, o_ref):
    o_ref[...] = _ple_math(h_ref, g_ref, wg_ref, p_ref, wp_ref)


def _ple_final_kernel(h_ref, g_ref, wg_ref, p_ref, wp_ref, gf_ref, o_ref):
    o_ref[...] = _rms(_ple_math(h_ref, g_ref, wg_ref, p_ref, wp_ref), gf_ref[...])


def _ple(h, g, w_gate, p, layer, w_proj, g_final=None):
    T, D = h.shape
    P = p.shape[2]
    in_specs = [
        pl.BlockSpec((TM, D), lambda i: (i, 0)),
        _resident((1, D)),
        _resident((D, D)),
        pl.BlockSpec((None, TM, P), lambda i: (layer, i, 0)),
        _resident((P, D)),
    ]
    args = (h, g, w_gate, p, w_proj)
    if g_final is not None:
        in_specs.append(_resident((1, D)))
        args += (g_final,)
    return pl.pallas_call(
        _ple_kernel if g_final is None else _ple_final_kernel,
        out_shape=jax.ShapeDtypeStruct((T, D), F32),
        grid=(T // TM,),
        in_specs=in_specs,
        out_specs=pl.BlockSpec((TM, D), lambda i: (i, 0)),
        compiler_params=_params("parallel"),
        name="ple" if g_final is None else "ple_final",
    )(*args)


def _dft_tables(n, scale, rows=None):
    blk = 64
    assert n % blk == 0
    rows = n if rows is None else rows
    j = jnp.arange(rows, dtype=jnp.int32)

    def cos_sin(k):
        ang = ((j[:, None] * k[None, :]) % n).astype(F32) * (2.0 * np.pi / n)
        return jnp.cos(ang), jnp.sin(ang)

    ca, sa = cos_sin(jnp.arange(n // blk, dtype=jnp.int32) * blk)
    cb, sb = cos_sin(jnp.arange(blk, dtype=jnp.int32))
    ca, sa, cb, sb = ca[:, :, None], sa[:, :, None], cb[:, None, :], sb[:, None, :]
    c = (ca * cb - sa * sb) * scale
    s = (sa * cb + ca * sb) * scale
    return c.reshape(rows, n).astype(BF16), s.reshape(rows, n).astype(BF16)


def _norm_nyq_kernel(x_ref, g_ref, a_ref, nyq_ref):
    a = _rms(x_ref[...], g_ref[...]).astype(BF16)
    a_ref[...] = a
    row = lax.broadcasted_iota(jnp.int32, (a.shape[0], 1), 0)
    sign = jnp.where(row % 2 == 0, 1.0, -1.0)
    part = jnp.sum(a.astype(F32) * sign, axis=0, keepdims=True)

    @pl.when(pl.program_id(1) == 0)
    def _():
        nyq_ref[...] = jnp.zeros_like(nyq_ref)

    nyq_ref[...] += jnp.broadcast_to(part, nyq_ref.shape[1:])[None]


def _norm_nyq(h, g, batch, seq):
    T, D = h.shape
    nt = seq // TM
    return pl.pallas_call(
        _norm_nyq_kernel,
        out_shape=(jax.ShapeDtypeStruct((T, D), BF16), jax.ShapeDtypeStruct((batch, BF16_ROWS, D), F32)),
        grid=(batch, nt),
        in_specs=[pl.BlockSpec((TM, D), lambda b, i: (b * nt + i, 0)), _resident((1, D))],
        out_specs=(pl.BlockSpec((TM, D), lambda b, i: (b * nt + i, 0)),
                   pl.BlockSpec((1, BF16_ROWS, D), lambda b, i: (b, 0, 0))),
        compiler_params=_params("parallel", "arbitrary"),
        name="fourier_norm",
    )(h, g)


def _half_dft_kernel(c_ref, s_ref, a_ref, p_ref, q_ref, accp_ref, accq_ref):
    k = pl.program_id(3)

    @pl.when(k == 0)
    def _():
        accp_ref[...] = jnp.zeros_like(accp_ref)
        accq_ref[...] = jnp.zeros_like(accq_ref)

    a = a_ref[...]
    accp_ref[...] += jnp.dot(c_ref[...], a, preferred_element_type=F32)
    accq_ref[...] += jnp.dot(s_ref[...], a, preferred_element_type=F32)

    @pl.when(k == pl.num_programs(3) - 1)
    def _():
        p_ref[...] = accp_ref[...].astype(BF16)
        q_ref[...] = accq_ref[...].astype(BF16)


def _half_dft(a, c, s, batch, seq):
    D = a.shape[1]
    half = seq // 2
    tm, tn, tk = min(DFT_TM, half), DFT_TN, min(DFT_TK, seq)
    out = jax.ShapeDtypeStruct((batch * half, D), BF16)
    ospec = pl.BlockSpec((tm, tn), lambda b, i, j, k: (b * (half // tm) + i, j))
    return pl.pallas_call(
        _half_dft_kernel,
        out_shape=(out, out),
        grid=(batch, half // tm, D // tn, seq // tk),
        in_specs=[
            pl.BlockSpec((tm, tk), lambda b, i, j, k: (i, k)),
            pl.BlockSpec((tm, tk), lambda b, i, j, k: (i, k)),
            pl.BlockSpec((tk, tn), lambda b, i, j, k: (b * (seq // tk) + k, j)),
        ],
        out_specs=(ospec, ospec),
        scratch_shapes=[pltpu.VMEM((tm, tn), F32), pltpu.VMEM((tm, tn), F32)],
        compiler_params=_params("parallel", "parallel", "parallel", "arbitrary"),
        name="fourier_half_dft",
    )(c, s, a)


def _chan_dft(p, q, c_ref, s_ref, sign, gd):
    outs = []
    for grp in range(p.shape[1] // gd):
        sl = slice(grp * gd, (grp + 1) * gd)
        pc = jnp.dot(p[:, sl], c_ref[...], preferred_element_type=F32)
        qs = jnp.dot(q[:, sl], s_ref[...], preferred_element_type=F32)
        outs.append(pc + sign * qs)
    return jnp.concatenate(outs, axis=1)


def _fourier_out_kernel(pa_ref, qa_ref, pb_ref, qb_ref, nyq_ref, c_ref, s_ref, rev_ref, w_ref, h_ref, o_ref,
                        *, n_lo, gd):
    i = pl.program_id(1)

    @pl.when(i < n_lo)
    def _():
        f = _chan_dft(pa_ref[...], qa_ref[...], c_ref, s_ref, -1.0, gd)
        o_ref[...] = h_ref[...] + jnp.dot(f.astype(BF16), w_ref[...], preferred_element_type=F32)

    @pl.when(i >= n_lo)
    def _():
        m = _chan_dft(pa_ref[...], qa_ref[...], c_ref, s_ref, 1.0, gd).astype(BF16)
        rev = jnp.dot(rev_ref[...], m, preferred_element_type=F32)
        mb = _chan_dft(pb_ref[...], qb_ref[...], c_ref, s_ref, 1.0, gd)[0:1, :]
        nyq = nyq_ref[0].astype(BF16)
        mn = _chan_dft(nyq, jnp.zeros_like(nyq), c_ref, s_ref, 1.0, gd)[0:1, :]
        first = jnp.where(i == n_lo, mn, mb)
        row = lax.broadcasted_iota(jnp.int32, rev.shape, 0)
        f = jnp.where(row == 0, first, rev)
        o_ref[...] = h_ref[...] + jnp.dot(f.astype(BF16), w_ref[...], preferred_element_type=F32)


def _fourier_out(p, q, nyq, cc, sc, w, h, batch, seq):
    D = h.shape[1]
    gd = cc.shape[0]
    tm = TM
    half = seq // 2
    assert half % tm == 0
    n_lo = half // tm
    nt = seq // tm
    r = jnp.arange(tm)
    rev = ((r[:, None] + r[None, :]) == tm).astype(BF16)

    def a_map(b, i):
        return (b * n_lo + jnp.where(i < n_lo, i, nt - 1 - i), 0)

    def b_map(b, i):
        t = jnp.where(i < n_lo, i, jnp.minimum(nt - i, n_lo - 1))
        return ((b * n_lo + t) * (tm // BF16_ROWS), 0)

    kern = functools.partial(_fourier_out_kernel, n_lo=n_lo, gd=gd)
    return pl.pallas_call(
        kern,
        out_shape=jax.ShapeDtypeStruct(h.shape, F32),
        grid=(batch, nt),
        in_specs=[
            pl.BlockSpec((tm, D), a_map),
            pl.BlockSpec((tm, D), a_map),
            pl.BlockSpec((BF16_ROWS, D), b_map),
            pl.BlockSpec((BF16_ROWS, D), b_map),
            pl.BlockSpec((1, BF16_ROWS, D), lambda b, i: (b, 0, 0)),
            _resident((gd, gd)),
            _resident((gd, gd)),
            _resident((tm, tm)),
            _resident((D, D)),
            pl.BlockSpec((tm, D), lambda b, i: (b * nt + i, 0)),
        ],
        out_specs=pl.BlockSpec((tm, D), lambda b, i: (b * nt + i, 0)),
        compiler_params=_params("parallel", "parallel"),
        name="fourier_out",
    )(p, q, p, q, nyq, cc, sc, rev, w, h)


def _fourier_mix(h, g, w_out, cc, sc, batch, seq):
    scale = 1.0 / float(np.sqrt(seq))
    cs, sn = _dft_tables(seq, scale, rows=seq // 2)
    a, nyq = _norm_nyq(h, g, batch, seq)
    p, q = _half_dft(a, cs, sn, batch, seq)
    return _fourier_out(p, q, nyq * scale, cc, sc, w_out, h, batch, seq)


def kernel(x_prompt, x_sample, p_prompt, p_sample, g_mix, w_qkv, g_q, g_k, w_attn_out, w_fourier_out, g_ffn, w_route_group, b_route_group, w_route_expert, b_route_expert, w_exp_gate, w_exp_up, w_exp_down, g_ple, w_ple_gate, w_ple_proj, g_final):
    depth = g_mix.shape[0]
    bp, sp, D = x_prompt.shape
    bs, ss, _ = x_sample.shape
    tp, ts = bp * sp, bs * ss
    gd = D // N_FOURIER_GROUPS
    geom = ((bs, ss), (bp, sp))
    hs = (x_sample.reshape(ts, D), x_prompt.reshape(tp, D))
    ps = (p_sample, p_prompt)

    n_mixers = 2
    for i in range(depth):
        jm = i // n_mixers
        g_i = g_mix[i][None, :]
        if i % n_mixers == 0:
            w, gq2, gk2 = _qkv_weights(w_qkv[jm], g_q[jm], g_k[jm])
            cos, sin = _rope_tables(max(sp, ss))
            w_o = w_attn_out[jm].astype(BF16)
            mixed = []
            for h, (batch, seq) in zip(hs, geom):
                qkv = _qkv_proj(h, g_i, w, gq2, gk2, cos, sin, seq)
                mixed.append(_matmul_residual(_attention(qkv, 0, batch, seq), w_o, h))
        else:
            cc, sc = _dft_tables(gd, 1.0 / float(np.sqrt(gd)))
            w_o = w_fourier_out[jm].astype(BF16)
            mixed = [_fourier_mix(h, g_i, w_o, cc, sc, batch, seq) for h, (batch, seq) in zip(hs, geom)]
        hs = _hier_moe(tuple(mixed), g_ffn[i][None, :], w_route_group[i], b_route_group[i], w_route_expert[i],
                       b_route_expert[i], w_exp_gate, w_exp_up, w_exp_down, i)
        w_pg = w_ple_gate[i].astype(BF16)
        w_pp = w_ple_proj[i].astype(BF16)
        g_f = g_final[None, :] if i == depth - 1 else None
        hs = tuple(_ple(h, g_ple[i][None, :], w_pg, p.reshape(depth, h.shape[0], -1), i, w_pp, g_f)
                   for h, p in zip(hs, ps))
    y_sample, y_prompt = hs
    return (y_prompt.reshape(bp, sp, D), y_sample.reshape(bs, ss, D))
```

```python
import functools

import jax
import jax.numpy as jnp
import numpy as np
from jax import lax
from jax.experimental import pallas as pl
from jax.experimental.pallas import tpu as pltpu

F32 = jnp.float32
BF16 = jnp.bfloat16
U32 = jnp.uint32

HEAD_DIM = 128
N_HEADS = 16
N_KV_HEADS = 8
Q_PER_KV = N_HEADS // N_KV_HEADS
ROPE_AXIS_DIM = HEAD_DIM // 2
ROPE_THETA = 10000.0
GRID_W = 64
N_FOURIER_GROUPS = 8
N_EXPERT_GROUPS = 4
EXPERTS_PER_GROUP = 8
N_EXPERTS = N_EXPERT_GROUPS * EXPERTS_PER_GROUP
TOP_K = 2
NORM_EPS = 1e-6

LANES = 128
SUBLANES = 8
BF16_ROWS = 16
MXU_DIM = 256
VMEM_LIMIT = 56 * 1024 * 1024

TM = 512
TQ = 256
TKV = 1024
EXPERT_BLOCK = 256
MOVE_ROWS = 512
DFT_TM, DFT_TN, DFT_TK = 1024, 1024, 2048


def _params(*sem):
    return pltpu.CompilerParams(dimension_semantics=sem, vmem_limit_bytes=VMEM_LIMIT)


def _resident(shape):
    return pl.BlockSpec(shape, lambda *_: (0,) * len(shape), pipeline_mode=pl.Buffered(1))


def _rms(x, g):
    ms = jnp.mean(x * x, axis=-1, keepdims=True)
    return x * lax.rsqrt(ms + NORM_EPS) * g


def _pack_rows(x, ref, rows):
    half = SUBLANES * LANES
    for c in range(SUBLANES):
        lo = x[:, c * LANES:(c + 1) * LANES].astype(BF16).astype(F32)
        hi = x[:, half + c * LANES:half + (c + 1) * LANES].astype(BF16).astype(F32)
        word = (pltpu.bitcast(lo, U32) >> 16) | (pltpu.bitcast(hi, U32) & jnp.uint32(0xFFFF0000))
        ref[pl.ds(c, rows, stride=SUBLANES), :] = word


def _unpack_rows(ref, rows, c):
    word = ref[pl.ds(c, rows, stride=SUBLANES), :]
    lo = pltpu.bitcast(word << 16, F32)
    hi = pltpu.bitcast(word & jnp.uint32(0xFFFF0000), F32)
    return lo, hi


def _qkv_kernel(x_ref, g_ref, w_ref, gq_ref, gk_ref, cos_ref, sin_ref, o_ref, *, d_q, d_qk):
    a = _rms(x_ref[...], g_ref[...]).astype(BF16)
    cos = cos_ref[...]
    sin = sin_ref[...]
    q_scale = float(np.log2(np.e) / np.sqrt(HEAD_DIM))
    tabs = {
        "q": (gq_ref[0:1, :] * cos * q_scale, gq_ref[1:2, :] * sin * q_scale),
        "k": (gk_ref[0:1, :] * cos, gk_ref[1:2, :] * sin),
    }
    n_out = o_ref.shape[1]
    for c0 in range(0, n_out, MXU_DIM):
        acc = jnp.dot(a, w_ref[:, c0:c0 + MXU_DIM], preferred_element_type=F32)
        if c0 >= d_qk:
            o_ref[:, c0:c0 + MXU_DIM] = acc.astype(BF16)
            continue
        t1, t2 = tabs["q" if c0 < d_q else "k"]
        for hh in range(MXU_DIM // HEAD_DIM):
            y = acc[:, hh * HEAD_DIM:(hh + 1) * HEAD_DIM]
            r = lax.rsqrt(jnp.mean(y * y, axis=-1, keepdims=True) + NORM_EPS)
            rot = (y * t1 + pltpu.roll(y, HEAD_DIM // 2, 1) * t2) * r
            o_ref[:, c0 + hh * HEAD_DIM:c0 + (hh + 1) * HEAD_DIM] = rot.astype(BF16)


def _permute_heads(x):
    lead = x.shape[:-1]
    x = x.reshape(lead + (-1, 2, 2, HEAD_DIM // 4))
    return jnp.swapaxes(x, -3, -2).reshape(lead + (-1,))


def _rope_tables(n_pos):
    pos = jnp.arange(n_pos, dtype=jnp.int32)
    row = (pos // GRID_W).astype(F32)
    col = (pos % GRID_W).astype(F32)
    inv_freq = ROPE_THETA ** (-jnp.arange(0, ROPE_AXIS_DIM, 2, dtype=F32) / ROPE_AXIS_DIM)
    ang_r = row[:, None] * inv_freq[None, :]
    ang_c = col[:, None] * inv_freq[None, :]
    cos = jnp.concatenate([jnp.cos(ang_r), jnp.cos(ang_c), jnp.cos(ang_r), jnp.cos(ang_c)], axis=-1)
    sin = jnp.concatenate([-jnp.sin(ang_r), -jnp.sin(ang_c), jnp.sin(ang_r), jnp.sin(ang_c)], axis=-1)
    return cos, sin


def _qkv_weights(w_qkv, g_q, g_k):
    d_qk = (N_HEADS + N_KV_HEADS) * HEAD_DIM
    w = jnp.concatenate([_permute_heads(w_qkv[:, :d_qk]).astype(BF16), w_qkv[:, d_qk:].astype(BF16)], axis=1)
    gq = _permute_heads(g_q)
    gk = _permute_heads(g_k)
    gq2 = jnp.stack([gq, jnp.roll(gq, HEAD_DIM // 2)])
    gk2 = jnp.stack([gk, jnp.roll(gk, HEAD_DIM // 2)])
    return w, gq2, gk2


def _qkv_proj(h, g, w, gq2, gk2, cos, sin, seq):
    T, D = h.shape
    N = w.shape[1]
    assert seq % TM == 0 and T % seq == 0
    d_q = N_HEADS * HEAD_DIM
    d_qk = d_q + N_KV_HEADS * HEAD_DIM

    def pos_map(i):
        return (i % (seq // TM), 0)

    kern = functools.partial(_qkv_kernel, d_q=d_q, d_qk=d_qk)
    return pl.pallas_call(
        kern,
        out_shape=jax.ShapeDtypeStruct((T, N), BF16),
        grid=(T // TM,),
        in_specs=[
            pl.BlockSpec((TM, D), lambda i: (i, 0)),
            _resident((1, D)),
            _resident((D, N)),
            _resident((2, HEAD_DIM)),
            _resident((2, HEAD_DIM)),
            pl.BlockSpec((TM, HEAD_DIM), pos_map),
            pl.BlockSpec((TM, HEAD_DIM), pos_map),
        ],
        out_specs=pl.BlockSpec((TM, N), lambda i: (i, 0)),
        compiler_params=_params("parallel"),
        name="qkv_proj",
    )(h, g, w, gq2, gk2, cos, sin)


def _attn_kernel(q_ref, k_ref, v_ref, o_ref, *, tq, tk, seq):
    q = q_ref[...]
    q2 = jnp.concatenate([q[:, :HEAD_DIM], q[:, HEAD_DIM:]], axis=0)
    m = jnp.full((2 * tq, 1), -jnp.inf, F32)
    l = jnp.zeros((2 * tq, 1), F32)
    acc = jnp.zeros((2 * tq, HEAD_DIM), F32)
    for c in range(seq // tk):
        k = k_ref[c * tk:(c + 1) * tk, :]
        v = v_ref[c * tk:(c + 1) * tk, :]
        s = lax.dot_general(q2, k, (((1,), (1,)), ((), ())), preferred_element_type=F32)
        m_new = jnp.maximum(m, jnp.max(s, axis=-1, keepdims=True))
        alpha = jnp.exp2(m - m_new)
        p = jnp.exp2(s - m_new)
        l = alpha * l + jnp.sum(p, axis=-1, keepdims=True)
        acc = alpha * acc + jnp.dot(p.astype(BF16), v, preferred_element_type=F32)
        m = m_new
    o = acc / l
    o_ref[:, :HEAD_DIM] = o[:tq].astype(BF16)
    o_ref[:, HEAD_DIM:] = o[tq:].astype(BF16)


def _attention(qkv, row0, batch, seq):
    assert row0 % seq == 0 and seq % TQ == 0 and seq % TKV == 0
    qw = Q_PER_KV * HEAD_DIM
    kern = functools.partial(_attn_kernel, tq=TQ, tk=TKV, seq=seq)
    q_blk0 = row0 // TQ
    s_blk0 = row0 // seq
    k_col0 = N_HEADS
    v_col0 = N_HEADS + N_KV_HEADS
    return pl.pallas_call(
        kern,
        out_shape=jax.ShapeDtypeStruct((batch * seq, N_HEADS * HEAD_DIM), BF16),
        grid=(batch, N_KV_HEADS, seq // TQ),
        in_specs=[
            pl.BlockSpec((TQ, qw), lambda b, h, i: (q_blk0 + b * (seq // TQ) + i, h)),
            pl.BlockSpec((seq, HEAD_DIM), lambda b, h, i: (s_blk0 + b, k_col0 + h)),
            pl.BlockSpec((seq, HEAD_DIM), lambda b, h, i: (s_blk0 + b, v_col0 + h)),
        ],
        out_specs=pl.BlockSpec((TQ, qw), lambda b, h, i: (b * (seq // TQ) + i, h)),
        compiler_params=_params("parallel", "parallel", "arbitrary"),
        name="attention",
    )(qkv, qkv, qkv)


def _mm_res_kernel(x_ref, w_ref, r_ref, o_ref):
    o_ref[...] = r_ref[...] + jnp.dot(x_ref[...], w_ref[...], preferred_element_type=F32)


def _matmul_residual(x, w, res):
    T, K = x.shape
    N = w.shape[1]
    return pl.pallas_call(
        _mm_res_kernel,
        out_shape=jax.ShapeDtypeStruct((T, N), F32),
        grid=(T // TM,),
        in_specs=[
            pl.BlockSpec((TM, K), lambda i: (i, 0)),
            _resident((K, N)),
            pl.BlockSpec((TM, N), lambda i: (i, 0)),
        ],
        out_specs=pl.BlockSpec((TM, N), lambda i: (i, 0)),
        compiler_params=_params("parallel"),
        name="matmul_residual",
    )(x, w, res)


R_E1, R_E2, R_W1, R_W2, R_RANK1, R_RANK2 = range(6)
ROUTE_LOGIT_E0 = N_EXPERT_GROUPS


def _route_kernel(x0_ref, x1_ref, g_ref, w_ref, b_ref, tri_ref, r_ref, cnt_ref, a_ref, carry_ref, *, n0):
    i = pl.program_id(0)

    @pl.when(i == 0)
    def _():
        carry_ref[...] = jnp.zeros_like(carry_ref)

    a = _rms(jnp.where(i < n0, x0_ref[...], x1_ref[...]), g_ref[...])
    _pack_rows(a, a_ref, a.shape[0])
    a_hi = a.astype(BF16)
    a_lo = (a - a_hi.astype(F32)).astype(BF16)
    hi_both = jnp.dot(a_hi, w_ref[...], preferred_element_type=F32)
    logits = (hi_both[:, :LANES]
              + (jnp.dot(a_lo, w_ref[:, :LANES], preferred_element_type=F32) + hi_both[:, LANES:])) + b_ref[...]
    lane = lax.broadcasted_iota(jnp.int32, logits.shape, 1)
    lane_f = lane.astype(F32)
    neg = -jnp.inf
    far = float(LANES)

    def first_max(vals):
        top = jnp.max(vals, axis=-1, keepdims=True)
        idx = jnp.min(jnp.where(vals == top, lane_f, far), axis=-1, keepdims=True)
        return top, idx.astype(jnp.int32)

    gmask = lane < N_EXPERT_GROUPS
    gtop, g_sel = first_max(jnp.where(gmask, logits, neg))
    pg = 1.0 / jnp.sum(jnp.where(gmask, jnp.exp(logits - gtop), 0.0), axis=-1, keepdims=True)
    lo = ROUTE_LOGIT_E0 + g_sel * EXPERTS_PER_GROUP
    le = jnp.where((lane >= lo) & (lane < lo + EXPERTS_PER_GROUP), logits, neg)
    t1, i1 = first_max(le)
    t2, i2 = first_max(jnp.where(lane == i1, neg, le))
    r21 = jnp.exp(t2 - t1)
    w1 = pg / (1.0 + r21)
    w2 = pg * r21 / (1.0 + r21)
    e1 = i1 - ROUTE_LOGIT_E0
    e2 = i2 - ROUTE_LOGIT_E0

    hit1 = lane == e1
    hit2 = lane == e2
    onehot = (hit1 | hit2).astype(F32)
    before = jnp.dot(tri_ref[...], onehot.astype(BF16), preferred_element_type=F32) + carry_ref[0:1, :]
    rank1 = jnp.sum(jnp.where(hit1, before, 0.0), axis=-1, keepdims=True)
    rank2 = jnp.sum(jnp.where(hit2, before, 0.0), axis=-1, keepdims=True)
    carry_ref[...] = carry_ref[...] + jnp.sum(onehot, axis=0, keepdims=True)
    cnt_ref[...] = carry_ref[...]

    rec = jnp.zeros(logits.shape, F32)
    for pos, val in ((R_E1, e1.astype(F32)), (R_E2, e2.astype(F32)), (R_W1, w1), (R_W2, w2),
                     (R_RANK1, rank1), (R_RANK2, rank2)):
        rec = jnp.where(lane == pos, val, rec)
    r_ref[...] = rec


def _route(hs, g, w_r, b_r):
    h0, h1 = hs
    D = h0.shape[1]
    T = h0.shape[0] + h1.shape[0]
    assert D == 2 * SUBLANES * LANES and h0.shape[0] % TM == 0 and h1.shape[0] % TM == 0
    n0 = h0.shape[0] // TM
    tri = (jnp.arange(TM)[:, None] > jnp.arange(TM)[None, :]).astype(BF16)
    w_hi = w_r.astype(BF16)
    w_lo = (w_r - w_hi.astype(F32)).astype(BF16)
    kern = functools.partial(_route_kernel, n0=n0)
    return pl.pallas_call(
        kern,
        out_shape=(jax.ShapeDtypeStruct((T, LANES), F32), jax.ShapeDtypeStruct((SUBLANES, LANES), F32),
                   jax.ShapeDtypeStruct((T * SUBLANES, LANES), U32)),
        grid=(T // TM,),
        in_specs=[
            pl.BlockSpec((TM, D), lambda i: (jnp.minimum(i, n0 - 1), 0)),
            pl.BlockSpec((TM, D), lambda i: (jnp.maximum(i - n0, 0), 0)),
            _resident((1, D)),
            _resident((D, 2 * LANES)),
            _resident((1, LANES)),
            _resident((TM, TM)),
        ],
        out_specs=(pl.BlockSpec((TM, LANES), lambda i: (i, 0)),
                   pl.BlockSpec((SUBLANES, LANES), lambda i: (0, 0)),
                   pl.BlockSpec((TM * SUBLANES, LANES), lambda i: (i, 0))),
        scratch_shapes=[pltpu.VMEM((SUBLANES, LANES), F32)],
        compiler_params=_params("arbitrary"),
        name="moe_route",
    )(h0, h1, g, jnp.concatenate([w_hi, w_lo], axis=1), b_r, tri)


def _tile_copy(src, src_row, dst, dst_row, sem):
    return pltpu.make_async_copy(src.at[pl.ds(pl.multiple_of(src_row * SUBLANES, SUBLANES), SUBLANES), :],
                                 dst.at[pl.ds(pl.multiple_of(dst_row * SUBLANES, SUBLANES), SUBLANES), :], sem)


def _dispatch_kernel(dest_ref, fill_ref, a_ref, xb_hbm, zero_ref, sem, *, rows, n_blocks):
    i = pl.program_id(0)
    base = i * rows

    def start(r, c):
        for k in range(TOP_K):
            _tile_copy(a_ref, r, xb_hbm, dest_ref[TOP_K * (base + r) + k], sem).start(priority=k)
        return c

    lax.fori_loop(0, rows, start, 0, unroll=8)
    for k in range(TOP_K):
        pltpu.make_async_copy(a_ref, xb_hbm.at[pl.ds(0, rows * SUBLANES), :], sem).wait()

    @pl.when(i == pl.num_programs(0) - 1)
    def _():
        zero_ref[...] = jnp.zeros_like(zero_ref)

        def per_expert(e, c):
            lo = fill_ref[2 * e]
            hi = fill_ref[2 * e + 1]

            def zstart(r, cc):
                _tile_copy(zero_ref, 0, xb_hbm, r, sem).start()
                return cc

            def zwait(r, cc):
                _tile_copy(zero_ref, 0, xb_hbm, r, sem).wait()
                return cc

            lax.fori_loop(lo, hi, zstart, 0)
            lax.fori_loop(lo, hi, zwait, 0)
            return c

        lax.fori_loop(0, N_EXPERTS, per_expert, 0)

        blk = zero_ref.shape[0]

        def block_copy(b):
            return pltpu.make_async_copy(zero_ref, xb_hbm.at[pl.ds(pl.multiple_of(b * blk, blk), blk), :], sem)

        def bstart(b, c):
            block_copy(b).start()
            return c

        def bwait(b, c):
            block_copy(b).wait()
            return c

        first_unused = fill_ref[2 * N_EXPERTS]
        lax.fori_loop(first_unused, n_blocks, bstart, 0)
        lax.fori_loop(first_unused, n_blocks, bwait, 0)


def _dispatch(a_slab, dest, fill, n_blocks, bm):
    T = a_slab.shape[0] // SUBLANES
    rows = MOVE_ROWS
    kern = functools.partial(_dispatch_kernel, rows=rows, n_blocks=n_blocks)
    return pl.pallas_call(
        kern,
        out_shape=jax.ShapeDtypeStruct((n_blocks * bm * SUBLANES, LANES), U32),
        grid_spec=pltpu.PrefetchScalarGridSpec(
            num_scalar_prefetch=2,
            grid=(T // rows,),
            in_specs=[pl.BlockSpec((rows * SUBLANES, LANES), lambda i, d, f: (i, 0))],
            out_specs=pl.BlockSpec(memory_space=pl.ANY),
            scratch_shapes=[pltpu.VMEM((bm * SUBLANES, LANES), U32), pltpu.SemaphoreType.DMA(())],
        ),
        compiler_params=_params("arbitrary"),
        name="moe_dispatch",
    )(dest, fill, a_slab)


M_USED, M_RUNS, M_EXPERT0 = 0, 1, 2


def _expert_kernel(be_ref, run_ref, meta_ref, x_ref, wg_hbm, wu_hbm, wd_hbm, o_ref,
                   x_s, wg_s, wu_s, wd_s, wg_f, wu_f, wd_f, sem, *, bm, layer):
    b = pl.program_id(0)
    n_used = meta_ref[M_USED]
    r = run_ref[b]

    def weight_copies(run, slot):
        e = meta_ref[M_EXPERT0 + run]
        return [pltpu.make_async_copy(w.at[layer, e], f.at[slot], sem.at[slot, j])
                for j, (w, f) in enumerate(((wg_hbm, wg_f), (wu_hbm, wu_f), (wd_hbm, wd_f)))]

    @pl.when((b < n_used) & ((b == 0) | (be_ref[b] != be_ref[jnp.maximum(b - 1, 0)])))
    def _():
        slot = r % 2

        @pl.when(r == 0)
        def _():
            for cp in weight_copies(0, 0):
                cp.start()

        for cp in weight_copies(r, slot):
            cp.wait()
        wg_s[...] = wg_f[slot].astype(BF16)
        wu_s[...] = wu_f[slot].astype(BF16)
        wd_s[...] = wd_f[slot].astype(BF16)

        @pl.when(r + 1 < meta_ref[M_RUNS])
        def _():
            for cp in weight_copies(r + 1, 1 - slot):
                cp.start()

    @pl.when(b < n_used)
    def _():
        half = SUBLANES * LANES
        for c in range(SUBLANES):
            lo, hi = _unpack_rows(x_ref, bm, c)
            x_s[:, c * LANES:(c + 1) * LANES] = lo.astype(BF16)
            x_s[:, half + c * LANES:half + (c + 1) * LANES] = hi.astype(BF16)
        a = x_s[...]
        hg = jnp.dot(a, wg_s[...], preferred_element_type=F32)
        hu = jnp.dot(a, wu_s[...], preferred_element_type=F32)
        h = hg * jax.nn.sigmoid(hg) * hu
        y = jnp.dot(h.astype(BF16), wd_s[...], preferred_element_type=F32)
        _pack_rows(y, o_ref, bm)

    @pl.when(b >= n_used)
    def _():
        o_ref[...] = jnp.zeros_like(o_ref)


def _experts(xb, w_gate, w_up, w_down, layer, blk_e, blk_run, meta):
    bm = EXPERT_BLOCK
    n_blocks = xb.shape[0] // (bm * SUBLANES)
    _, _, D, De = w_gate.shape
    kern = functools.partial(_expert_kernel, bm=bm, layer=layer)
    hbm = pl.BlockSpec(memory_space=pl.ANY)
    return pl.pallas_call(
        kern,
        out_shape=jax.ShapeDtypeStruct(xb.shape, U32),
        grid_spec=pltpu.PrefetchScalarGridSpec(
            num_scalar_prefetch=3,
            grid=(n_blocks,),
            in_specs=[
                pl.BlockSpec((bm * SUBLANES, LANES), lambda b, be, run, meta: (jnp.minimum(b, meta[M_USED] - 1), 0)),
                hbm, hbm, hbm,
            ],
            out_specs=pl.BlockSpec((bm * SUBLANES, LANES), lambda b, be, run, meta: (b, 0)),
            scratch_shapes=[pltpu.VMEM((bm, D), BF16), pltpu.VMEM((D, De), BF16),
                            pltpu.VMEM((D, De), BF16), pltpu.VMEM((De, D), BF16),
                            pltpu.VMEM((2, D, De), F32), pltpu.VMEM((2, D, De), F32), pltpu.VMEM((2, De, D), F32),
                            pltpu.SemaphoreType.DMA((2, 3))],
        ),
        compiler_params=_params("arbitrary"),
        name="moe_experts",
    )(blk_e, blk_run, meta, xb, w_gate, w_up, w_down)


def _hier_moe(hs, g, w_rg, b_rg, w_re, b_re, w_gate, w_up, w_down, layer):
    D = hs[0].shape[1]
    T = hs[0].shape[0] + hs[1].shape[0]
    pad = LANES - N_EXPERT_GROUPS - N_EXPERTS
    w_r = jnp.concatenate([w_rg, w_re, jnp.zeros((D, pad), F32)], axis=1)
    b_r = jnp.concatenate([b_rg, b_re, jnp.zeros((pad,), F32)])[None, :]
    rec, cnt, a_slab = _route(hs, g, w_r, b_r)

    bm = EXPERT_BLOCK
    e_tok = rec[:, R_E1:R_E2 + 1].astype(jnp.int32)
    rank = rec[:, R_RANK1:R_RANK2 + 1].astype(jnp.int32)
    counts = cnt[0, :N_EXPERTS].astype(jnp.int32)
    pad_counts = (counts + bm - 1) // bm * bm
    pad_end = jnp.cumsum(pad_counts)
    pad_start = pad_end - pad_counts
    start_tok = jnp.sum(jnp.where(e_tok[..., None] == jnp.arange(N_EXPERTS, dtype=jnp.int32), pad_start, 0), axis=-1)
    dest = (start_tok + rank).reshape(-1)
    n_blocks = (T * TOP_K) // bm + N_EXPERTS
    blk_start = jnp.arange(n_blocks, dtype=jnp.int32) * bm
    blk_e = jnp.minimum(jnp.sum(blk_start[:, None] >= pad_end[None, :], axis=1), N_EXPERTS - 1).astype(jnp.int32)
    n_used = (pad_end[-1:] // bm).astype(jnp.int32)
    fill = jnp.concatenate([jnp.stack([pad_start + counts, pad_end], axis=1).reshape(-1), n_used]).astype(jnp.int32)
    blk = jnp.arange(n_blocks, dtype=jnp.int32)
    run_starts = (blk < n_used[0]) & ((blk == 0) | (blk_e != jnp.roll(blk_e, 1)))
    blk_run = (jnp.cumsum(run_starts) - 1).astype(jnp.int32)
    e_ids = jnp.arange(N_EXPERTS, dtype=jnp.int32)
    run_experts = jnp.sort(jnp.where(counts > 0, e_ids, e_ids + N_EXPERTS)) % N_EXPERTS
    meta = jnp.concatenate([n_used, jnp.sum(run_starts, keepdims=True), run_experts]).astype(jnp.int32)

    xb = _dispatch(a_slab, dest, fill, n_blocks, bm)
    yb = _experts(xb, w_gate, w_up, w_down, layer, blk_e, blk_run, meta)
    return rec, yb, dest


def _combine_ple_kernel(dest_ref, h_ref, r_ref, yb_hbm, g_ref, wg_ref, p_ref, wp_ref, *rest, rows, tok0, final):
    if final:
        gf_ref, o_ref, y_ref, h2_ref, sem = rest
    else:
        o_ref, y_ref, h2_ref, sem = rest
    i = pl.program_id(0)

    def issue(step, slot):
        base = tok0 + step * rows

        def start(r, c):
            for k in range(TOP_K):
                _tile_copy(yb_hbm, dest_ref[TOP_K * (base + r) + k], y_ref.at[slot, k], r,
                           sem.at[slot]).start(priority=k)
            return c

        lax.fori_loop(0, rows, start, 0, unroll=8)

    def finish(slot):
        for k in range(TOP_K):
            pltpu.make_async_copy(yb_hbm.at[pl.ds(0, rows * SUBLANES), :], y_ref.at[slot, k], sem.at[slot]).wait()
        rec = r_ref[...]
        w1 = rec[:, R_W1:R_W1 + 1]
        w2 = rec[:, R_W2:R_W2 + 1]
        half = SUBLANES * LANES
        for c in range(SUBLANES):
            lo1, hi1 = _unpack_rows(y_ref.at[slot, 0], rows, c)
            lo2, hi2 = _unpack_rows(y_ref.at[slot, 1], rows, c)
            sl = slice(c * LANES, (c + 1) * LANES)
            sh = slice(half + c * LANES, half + (c + 1) * LANES)
            h2_ref[:, sl] = h_ref[:, sl] + (w1 * lo1 + w2 * lo2)
            h2_ref[:, sh] = h_ref[:, sh] + (w1 * hi1 + w2 * hi2)

    @pl.when(i == 0)
    def _():
        issue(0, 0)

    for slot in range(2):
        @pl.when(i % 2 == slot)
        def _():
            @pl.when(i + 1 < pl.num_programs(0))
            def _():
                issue(i + 1, 1 - slot)

            finish(slot)

    h2 = h2_ref[...]
    a = _rms(h2, g_ref[...]).astype(BF16)
    gate = jax.nn.sigmoid(jnp.dot(a, wg_ref[...], preferred_element_type=F32))
    emb = jnp.dot(p_ref[...].astype(BF16), wp_ref[...], preferred_element_type=F32)
    out = h2 + gate * emb
    o_ref[...] = _rms(out, gf_ref[...]) if final else out


def _combine_ple(h, rec, yb, dest, tok0, g, w_gate, p, layer, w_proj, g_final=None):
    Tg, D = h.shape
    P = p.shape[2]
    rows = TM
    assert tok0 % rows == 0 and Tg % rows == 0
    blk0 = tok0 // rows
    final = g_final is not None
    in_specs = [
        pl.BlockSpec((rows, D), lambda i, d: (i, 0)),
        pl.BlockSpec((rows, LANES), lambda i, d: (blk0 + i, 0)),
        pl.BlockSpec(memory_space=pl.ANY),
        _resident((1, D)),
        _resident((D, D)),
        pl.BlockSpec((None, rows, P), lambda i, d: (layer, i, 0)),
        _resident((P, D)),
    ]
    args = (dest, h, rec, yb, g, w_gate, p, w_proj)
    if final:
        in_specs.append(_resident((1, D)))
        args += (g_final,)
    kern = functools.partial(_combine_ple_kernel, rows=rows, tok0=tok0, final=final)
    return pl.pallas_call(
        kern,
        out_shape=jax.ShapeDtypeStruct((Tg, D), F32),
        grid_spec=pltpu.PrefetchScalarGridSpec(
            num_scalar_prefetch=1,
            grid=(Tg // rows,),
            in_specs=in_specs,
            out_specs=pl.BlockSpec((rows, D), lambda i, d: (i, 0)),
            scratch_shapes=[pltpu.VMEM((2, TOP_K, rows * SUBLANES, LANES), U32), pltpu.VMEM((rows, D), F32),
                            pltpu.SemaphoreType.DMA((2,))],
        ),
        compiler_params=_params("arbitrary"),
        name="moe_combine_ple_final" if final else "moe_combine_ple",
    )(*args)


def _dft_tables(n, scale, rows=None):
    blk = 64
    assert n % blk == 0
    rows = n if rows is None else rows
    j = jnp.arange(rows, dtype=jnp.int32)

    def cos_sin(k):
        ang = ((j[:, None] * k[None, :]) % n).astype(F32) * (2.0 * np.pi / n)
        return jnp.cos(ang), jnp.sin(ang)

    ca, sa = cos_sin(jnp.arange(n // blk, dtype=jnp.int32) * blk)
    cb, sb = cos_sin(jnp.arange(blk, dtype=jnp.int32))
    ca, sa, cb, sb = ca[:, :, None], sa[:, :, None], cb[:, None, :], sb[:, None, :]
    c = (ca * cb - sa * sb) * scale
    s = (sa * cb + ca * sb) * scale
    return c.reshape(rows, n).astype(BF16), s.reshape(rows, n).astype(BF16)


def _norm_nyq_kernel(x_ref, g_ref, a_ref, nyq_ref):
    a = _rms(x_ref[...], g_ref[...]).astype(BF16)
    a_ref[...] = a
    row = lax.broadcasted_iota(jnp.int32, (a.shape[0], 1), 0)
    sign = jnp.where(row % 2 == 0, 1.0, -1.0)
    part = jnp.sum(a.astype(F32) * sign, axis=0, keepdims=True)

    @pl.when(pl.program_id(1) == 0)
    def _():
        nyq_ref[...] = jnp.zeros_like(nyq_ref)

    nyq_ref[...] += jnp.broadcast_to(part, nyq_ref.shape[1:])[None]


def _norm_nyq(h, g, batch, seq):
    T, D = h.shape
    nt = seq // TM
    return pl.pallas_call(
        _norm_nyq_kernel,
        out_shape=(jax.ShapeDtypeStruct((T, D), BF16), jax.ShapeDtypeStruct((batch, BF16_ROWS, D), F32)),
        grid=(batch, nt),
        in_specs=[pl.BlockSpec((TM, D), lambda b, i: (b * nt + i, 0)), _resident((1, D))],
        out_specs=(pl.BlockSpec((TM, D), lambda b, i: (b * nt + i, 0)),
                   pl.BlockSpec((1, BF16_ROWS, D), lambda b, i: (b, 0, 0))),
        compiler_params=_params("parallel", "arbitrary"),
        name="fourier_norm",
    )(h, g)


def _half_dft_kernel(c_ref, s_ref, a_ref, p_ref, q_ref, accp_ref, accq_ref):
    k = pl.program_id(3)

    @pl.when(k == 0)
    def _():
        accp_ref[...] = jnp.zeros_like(accp_ref)
        accq_ref[...] = jnp.zeros_like(accq_ref)

    a = a_ref[...]
    accp_ref[...] += jnp.dot(c_ref[...], a, preferred_element_type=F32)
    accq_ref[...] += jnp.dot(s_ref[...], a, preferred_element_type=F32)

    @pl.when(k == pl.num_programs(3) - 1)
    def _():
        p_ref[...] = accp_ref[...].astype(BF16)
        q_ref[...] = accq_ref[...].astype(BF16)


def _half_dft(a, c, s, batch, seq):
    D = a.shape[1]
    half = seq // 2
    tm, tn, tk = min(DFT_TM, half), DFT_TN, min(DFT_TK, seq)
    out = jax.ShapeDtypeStruct((batch * half, D), BF16)
    ospec = pl.BlockSpec((tm, tn), lambda b, i, j, k: (b * (half // tm) + i, j))
    return pl.pallas_call(
        _half_dft_kernel,
        out_shape=(out, out),
        grid=(batch, half // tm, D // tn, seq // tk),
        in_specs=[
            pl.BlockSpec((tm, tk), lambda b, i, j, k: (i, k)),
            pl.BlockSpec((tm, tk), lambda b, i, j, k: (i, k)),
            pl.BlockSpec((tk, tn), lambda b, i, j, k: (b * (seq // tk) + k, j)),
        ],
        out_specs=(ospec, ospec),
        scratch_shapes=[pltpu.VMEM((tm, tn), F32), pltpu.VMEM((tm, tn), F32)],
        compiler_params=_params("parallel", "parallel", "parallel", "arbitrary"),
        name="fourier_half_dft",
    )(c, s, a)


def _chan_dft(p, q, c_ref, s_ref, sign, gd):
    outs = []
    for grp in range(p.shape[1] // gd):
        sl = slice(grp * gd, (grp + 1) * gd)
        pc = jnp.dot(p[:, sl], c_ref[...], preferred_element_type=F32)
        qs = jnp.dot(q[:, sl], s_ref[...], preferred_element_type=F32)
        outs.append(pc + sign * qs)
    return jnp.concatenate(outs, axis=1)


def _fourier_out_kernel(pa_ref, qa_ref, pb_ref, qb_ref, nyq_ref, c_ref, s_ref, rev_ref, w_ref, h_ref, o_ref,
                        *, n_lo, gd):
    i = pl.program_id(1)

    @pl.when(i < n_lo)
    def _():
        f = _chan_dft(pa_ref[...], qa_ref[...], c_ref, s_ref, -1.0, gd)
        o_ref[...] = h_ref[...] + jnp.dot(f.astype(BF16), w_ref[...], preferred_element_type=F32)

    @pl.when(i >= n_lo)
    def _():
        m = _chan_dft(pa_ref[...], qa_ref[...], c_ref, s_ref, 1.0, gd).astype(BF16)
        rev = jnp.dot(rev_ref[...], m, preferred_element_type=F32)
        mb = _chan_dft(pb_ref[...], qb_ref[...], c_ref, s_ref, 1.0, gd)[0:1, :]
        nyq = nyq_ref[0].astype(BF16)
        mn = _chan_dft(nyq, jnp.zeros_like(nyq), c_ref, s_ref, 1.0, gd)[0:1, :]
        first = jnp.where(i == n_lo, mn, mb)
        row = lax.broadcasted_iota(jnp.int32, rev.shape, 0)
        f = jnp.where(row == 0, first, rev)
        o_ref[...] = h_ref[...] + jnp.dot(f.astype(BF16), w_ref[...], preferred_element_type=F32)


def _fourier_out(p, q, nyq, cc, sc, w, h, batch, seq):
    D = h.shape[1]
    gd = cc.shape[0]
    tm = TM
    half = seq // 2
    assert half % tm == 0
    n_lo = half // tm
    nt = seq // tm
    r = jnp.arange(tm)
    rev = ((r[:, None] + r[None, :]) == tm).astype(BF16)

    def a_map(b, i):
        return (b * n_lo + jnp.where(i < n_lo, i, nt - 1 - i), 0)

    def b_map(b, i):
        t = jnp.where(i < n_lo, i, jnp.minimum(nt - i, n_lo - 1))
        return ((b * n_lo + t) * (tm // BF16_ROWS), 0)

    kern = functools.partial(_fourier_out_kernel, n_lo=n_lo, gd=gd)
    return pl.pallas_call(
        kern,
        out_shape=jax.ShapeDtypeStruct(h.shape, F32),
        grid=(batch, nt),
        in_specs=[
            pl.BlockSpec((tm, D), a_map),
            pl.BlockSpec((tm, D), a_map),
            pl.BlockSpec((BF16_ROWS, D), b_map),
            pl.BlockSpec((BF16_ROWS, D), b_map),
            pl.BlockSpec((1, BF16_ROWS, D), lambda b, i: (b, 0, 0)),
            _resident((gd, gd)),
            _resident((gd, gd)),
            _resident((tm, tm)),
            _resident((D, D)),
            pl.BlockSpec((tm, D), lambda b, i: (b * nt + i, 0)),
        ],
        out_specs=pl.BlockSpec((tm, D), lambda b, i: (b * nt + i, 0)),
        compiler_params=_params("parallel", "parallel"),
        name="fourier_out",
    )(p, q, p, q, nyq, cc, sc, rev, w, h)


def _fourier_mix(h, g, w_out, cc, sc, batch, seq):
    scale = 1.0 / float(np.sqrt(seq))
    cs, sn = _dft_tables(seq, scale, rows=seq // 2)
    a, nyq = _norm_nyq(h, g, batch, seq)
    p, q = _half_dft(a, cs, sn, batch, seq)
    return _fourier_out(p, q, nyq * scale, cc, sc, w_out, h, batch, seq)


def kernel(x_prompt, x_sample, p_prompt, p_sample, g_mix, w_qkv, g_q, g_k, w_attn_out, w_fourier_out, g_ffn, w_route_group, b_route_group, w_route_expert, b_route_expert, w_exp_gate, w_exp_up, w_exp_down, g_ple, w_ple_gate, w_ple_proj, g_final):
    depth = g_mix.shape[0]
    bp, sp, D = x_prompt.shape
    bs, ss, _ = x_sample.shape
    tp, ts = bp * sp, bs * ss
    gd = D // N_FOURIER_GROUPS
    geom = ((bs, ss), (bp, sp))
    hs = (x_sample.reshape(ts, D), x_prompt.reshape(tp, D))
    ps = (p_sample, p_prompt)

    n_mixers = 2
    for i in range(depth):
        jm = i // n_mixers
        g_i = g_mix[i][None, :]
        if i % n_mixers == 0:
            w, gq2, gk2 = _qkv_weights(w_qkv[jm], g_q[jm], g_k[jm])
            cos, sin = _rope_tables(max(sp, ss))
            w_o = w_attn_out[jm].astype(BF16)
            mixed = []
            for h, (batch, seq) in zip(hs, geom):
                qkv = _qkv_proj(h, g_i, w, gq2, gk2, cos, sin, seq)
                mixed.append(_matmul_residual(_attention(qkv, 0, batch, seq), w_o, h))
        else:
            cc, sc = _dft_tables(gd, 1.0 / float(np.sqrt(gd)))
            w_o = w_fourier_out[jm].astype(BF16)
            mixed = [_fourier_mix(h, g_i, w_o, cc, sc, batch, seq) for h, (batch, seq) in zip(hs, geom)]
        rec, yb, dest = _hier_moe(tuple(mixed), g_ffn[i][None, :], w_route_group[i], b_route_group[i],
                                  w_route_expert[i], b_route_expert[i], w_exp_gate, w_exp_up, w_exp_down, i)
        w_pg = w_ple_gate[i].astype(BF16)
        w_pp = w_ple_proj[i].astype(BF16)
        g_f = g_final[None, :] if i == depth - 1 else None
        tok0s = (0, mixed[0].shape[0])
        hs = tuple(_combine_ple(h, rec, yb, dest, tok0, g_ple[i][None, :], w_pg,
                                p.reshape(depth, h.shape[0], -1), i, w_pp, g_f)
                   for h, tok0, p in zip(mixed, tok0s, ps))
    y_sample, y_prompt = hs
    return (y_prompt.reshape(bp, sp, D), y_sample.reshape(bs, ss, D))
```

```python
import functools

import jax
import jax.numpy as jnp
import numpy as np
from jax import lax
from jax.experimental import pallas as pl
from jax.experimental.pallas import tpu as pltpu

F32 = jnp.float32
BF16 = jnp.bfloat16
U32 = jnp.uint32

HEAD_DIM = 128
N_HEADS = 16
N_KV_HEADS = 8
Q_PER_KV = N_HEADS // N_KV_HEADS
ROPE_AXIS_DIM = HEAD_DIM // 2
ROPE_THETA = 10000.0
GRID_W = 64
N_FOURIER_GROUPS = 8
N_EXPERT_GROUPS = 4
EXPERTS_PER_GROUP = 8
N_EXPERTS = N_EXPERT_GROUPS * EXPERTS_PER_GROUP
TOP_K = 2
NORM_EPS = 1e-6

LANES = 128
SUBLANES = 8
BF16_ROWS = 16
MXU_DIM = 256
VMEM_LIMIT = 56 * 1024 * 1024

TM = 512
TQ = 512
TKV = 1024
EXPERT_BLOCK = 256
MOVE_ROWS = 512
COMBINE_ROWS = 256
DFT_TM, DFT_TN, DFT_TK = 1024, 1024, 2048


def _params(*sem):
    return pltpu.CompilerParams(dimension_semantics=sem, vmem_limit_bytes=VMEM_LIMIT)


def _resident(shape):
    return pl.BlockSpec(shape, lambda *_: (0,) * len(shape), pipeline_mode=pl.Buffered(1))


def _rms(x, g):
    ms = jnp.mean(x * x, axis=-1, keepdims=True)
    return x * lax.rsqrt(ms + NORM_EPS) * g


def _pack_rows(x, ref, rows):
    half = SUBLANES * LANES
    for c in range(SUBLANES):
        lo = x[:, c * LANES:(c + 1) * LANES].astype(BF16).astype(F32)
        hi = x[:, half + c * LANES:half + (c + 1) * LANES].astype(BF16).astype(F32)
        word = (pltpu.bitcast(lo, U32) >> 16) | (pltpu.bitcast(hi, U32) & jnp.uint32(0xFFFF0000))
        ref[pl.ds(c, rows, stride=SUBLANES), :] = word


def _unpack_rows(ref, rows, c):
    word = ref[pl.ds(c, rows, stride=SUBLANES), :]
    lo = pltpu.bitcast(word << 16, F32)
    hi = pltpu.bitcast(word & jnp.uint32(0xFFFF0000), F32)
    return lo, hi


def _qkv_kernel(x_ref, g_ref, w_ref, gq_ref, gk_ref, cos_ref, sin_ref, o_ref, *, d_q, d_qk):
    a = _rms(x_ref[...], g_ref[...]).astype(BF16)
    cos = cos_ref[...]
    sin = sin_ref[...]
    q_scale = float(np.log2(np.e) / np.sqrt(HEAD_DIM))
    tabs = {
        "q": (gq_ref[0:1, :] * cos * q_scale, gq_ref[1:2, :] * sin * q_scale),
        "k": (gk_ref[0:1, :] * cos, gk_ref[1:2, :] * sin),
    }
    n_out = o_ref.shape[1]
    for c0 in range(0, n_out, MXU_DIM):
        acc = jnp.dot(a, w_ref[:, c0:c0 + MXU_DIM], preferred_element_type=F32)
        if c0 >= d_qk:
            o_ref[:, c0:c0 + MXU_DIM] = acc.astype(BF16)
            continue
        t1, t2 = tabs["q" if c0 < d_q else "k"]
        for hh in range(MXU_DIM // HEAD_DIM):
            y = acc[:, hh * HEAD_DIM:(hh + 1) * HEAD_DIM]
            r = lax.rsqrt(jnp.mean(y * y, axis=-1, keepdims=True) + NORM_EPS)
            rot = (y * t1 + pltpu.roll(y, HEAD_DIM // 2, 1) * t2) * r
            o_ref[:, c0 + hh * HEAD_DIM:c0 + (hh + 1) * HEAD_DIM] = rot.astype(BF16)


def _permute_heads(x):
    lead = x.shape[:-1]
    x = x.reshape(lead + (-1, 2, 2, HEAD_DIM // 4))
    return jnp.swapaxes(x, -3, -2).reshape(lead + (-1,))


def _rope_tables(n_pos):
    pos = jnp.arange(n_pos, dtype=jnp.int32)
    row = (pos // GRID_W).astype(F32)
    col = (pos % GRID_W).astype(F32)
    inv_freq = ROPE_THETA ** (-jnp.arange(0, ROPE_AXIS_DIM, 2, dtype=F32) / ROPE_AXIS_DIM)
    ang_r = row[:, None] * inv_freq[None, :]
    ang_c = col[:, None] * inv_freq[None, :]
    cos = jnp.concatenate([jnp.cos(ang_r), jnp.cos(ang_c), jnp.cos(ang_r), jnp.cos(ang_c)], axis=-1)
    sin = jnp.concatenate([-jnp.sin(ang_r), -jnp.sin(ang_c), jnp.sin(ang_r), jnp.sin(ang_c)], axis=-1)
    return cos, sin


def _qkv_weights(w_qkv, g_q, g_k):
    d_qk = (N_HEADS + N_KV_HEADS) * HEAD_DIM
    w = jnp.concatenate([_permute_heads(w_qkv[:, :d_qk]).astype(BF16), w_qkv[:, d_qk:].astype(BF16)], axis=1)
    gq = _permute_heads(g_q)
    gk = _permute_heads(g_k)
    gq2 = jnp.stack([gq, jnp.roll(gq, HEAD_DIM // 2)])
    gk2 = jnp.stack([gk, jnp.roll(gk, HEAD_DIM // 2)])
    return w, gq2, gk2


def _qkv_proj(h, g, w, gq2, gk2, cos, sin, seq):
    T, D = h.shape
    N = w.shape[1]
    assert seq % TM == 0 and T % seq == 0
    d_q = N_HEADS * HEAD_DIM
    d_qk = d_q + N_KV_HEADS * HEAD_DIM

    def pos_map(i):
        return (i % (seq // TM), 0)

    kern = functools.partial(_qkv_kernel, d_q=d_q, d_qk=d_qk)
    return pl.pallas_call(
        kern,
        out_shape=jax.ShapeDtypeStruct((T, N), BF16),
        grid=(T // TM,),
        in_specs=[
            pl.BlockSpec((TM, D), lambda i: (i, 0)),
            _resident((1, D)),
            _resident((D, N)),
            _resident((2, HEAD_DIM)),
            _resident((2, HEAD_DIM)),
            pl.BlockSpec((TM, HEAD_DIM), pos_map),
            pl.BlockSpec((TM, HEAD_DIM), pos_map),
        ],
        out_specs=pl.BlockSpec((TM, N), lambda i: (i, 0)),
        compiler_params=_params("parallel"),
        name="qkv_proj",
    )(h, g, w, gq2, gk2, cos, sin)


def _attn_kernel(q_ref, k_ref, v_ref, o_ref, *, tq, tk, seq):
    q = q_ref[...]
    q2 = jnp.concatenate([q[:, :HEAD_DIM], q[:, HEAD_DIM:]], axis=0)
    m = jnp.full((2 * tq, 1), -jnp.inf, F32)
    l = jnp.zeros((2 * tq, 1), F32)
    acc = jnp.zeros((2 * tq, HEAD_DIM), F32)
    for c in range(seq // tk):
        k = k_ref[c * tk:(c + 1) * tk, :]
        v = v_ref[c * tk:(c + 1) * tk, :]
        s = lax.dot_general(q2, k, (((1,), (1,)), ((), ())), preferred_element_type=F32)
        m_new = jnp.maximum(m, jnp.max(s, axis=-1, keepdims=True))
        alpha = jnp.exp2(m - m_new)
        p = jnp.exp2(s - m_new)
        l = alpha * l + jnp.sum(p, axis=-1, keepdims=True)
        acc = alpha * acc + jnp.dot(p.astype(BF16), v, preferred_element_type=F32)
        m = m_new
    o = acc / l
    o_ref[:, :HEAD_DIM] = o[:tq].astype(BF16)
    o_ref[:, HEAD_DIM:] = o[tq:].astype(BF16)


def _attention(qkv, row0, batch, seq):
    assert row0 % seq == 0 and seq % TQ == 0 and seq % TKV == 0
    qw = Q_PER_KV * HEAD_DIM
    kern = functools.partial(_attn_kernel, tq=TQ, tk=TKV, seq=seq)
    q_blk0 = row0 // TQ
    s_blk0 = row0 // seq
    k_col0 = N_HEADS
    v_col0 = N_HEADS + N_KV_HEADS
    return pl.pallas_call(
        kern,
        out_shape=jax.ShapeDtypeStruct((batch * seq, N_HEADS * HEAD_DIM), BF16),
        grid=(batch, N_KV_HEADS, seq // TQ),
        in_specs=[
            pl.BlockSpec((TQ, qw), lambda b, h, i: (q_blk0 + b * (seq // TQ) + i, h)),
            pl.BlockSpec((seq, HEAD_DIM), lambda b, h, i: (s_blk0 + b, k_col0 + h)),
            pl.BlockSpec((seq, HEAD_DIM), lambda b, h, i: (s_blk0 + b, v_col0 + h)),
        ],
        out_specs=pl.BlockSpec((TQ, qw), lambda b, h, i: (b * (seq // TQ) + i, h)),
        compiler_params=_params("parallel", "parallel", "arbitrary"),
        name="attention",
    )(qkv, qkv, qkv)


def _mm_res_kernel(x_ref, w_ref, r_ref, o_ref):
    o_ref[...] = r_ref[...] + jnp.dot(x_ref[...], w_ref[...], preferred_element_type=F32)


def _matmul_residual(x, w, res):
    T, K = x.shape
    N = w.shape[1]
    return pl.pallas_call(
        _mm_res_kernel,
        out_shape=jax.ShapeDtypeStruct((T, N), F32),
        grid=(T // TM,),
        in_specs=[
            pl.BlockSpec((TM, K), lambda i: (i, 0)),
            _resident((K, N)),
            pl.BlockSpec((TM, N), lambda i: (i, 0)),
        ],
        out_specs=pl.BlockSpec((TM, N), lambda i: (i, 0)),
        compiler_params=_params("parallel"),
        name="matmul_residual",
    )(x, w, res)


R_E1, R_E2, R_W1, R_W2, R_RANK1, R_RANK2 = range(6)
ROUTE_LOGIT_E0 = N_EXPERT_GROUPS


def _route_kernel(x0_ref, x1_ref, g_ref, w_ref, b_ref, tri_ref, r_ref, cnt_ref, a_ref, carry_ref, *, n0):
    i = pl.program_id(0)

    @pl.when(i == 0)
    def _():
        carry_ref[...] = jnp.zeros_like(carry_ref)

    a = _rms(jnp.where(i < n0, x0_ref[...], x1_ref[...]), g_ref[...])
    _pack_rows(a, a_ref, a.shape[0])
    a_hi = a.astype(BF16)
    a_lo = (a - a_hi.astype(F32)).astype(BF16)
    hi_both = jnp.dot(a_hi, w_ref[...], preferred_element_type=F32)
    logits = (hi_both[:, :LANES]
              + (jnp.dot(a_lo, w_ref[:, :LANES], preferred_element_type=F32) + hi_both[:, LANES:])) + b_ref[...]
    lane = lax.broadcasted_iota(jnp.int32, logits.shape, 1)
    lane_f = lane.astype(F32)
    neg = -jnp.inf
    far = float(LANES)

    def first_max(vals):
        top = jnp.max(vals, axis=-1, keepdims=True)
        idx = jnp.min(jnp.where(vals == top, lane_f, far), axis=-1, keepdims=True)
        return top, idx.astype(jnp.int32)

    gmask = lane < N_EXPERT_GROUPS
    gtop, g_sel = first_max(jnp.where(gmask, logits, neg))
    pg = 1.0 / jnp.sum(jnp.where(gmask, jnp.exp(logits - gtop), 0.0), axis=-1, keepdims=True)
    lo = ROUTE_LOGIT_E0 + g_sel * EXPERTS_PER_GROUP
    le = jnp.where((lane >= lo) & (lane < lo + EXPERTS_PER_GROUP), logits, neg)
    t1, i1 = first_max(le)
    t2, i2 = first_max(jnp.where(lane == i1, neg, le))
    r21 = jnp.exp(t2 - t1)
    w1 = pg / (1.0 + r21)
    w2 = pg * r21 / (1.0 + r21)
    e1 = i1 - ROUTE_LOGIT_E0
    e2 = i2 - ROUTE_LOGIT_E0

    hit1 = lane == e1
    hit2 = lane == e2
    onehot = (hit1 | hit2).astype(F32)
    before = jnp.dot(tri_ref[...], onehot.astype(BF16), preferred_element_type=F32) + carry_ref[0:1, :]
    rank1 = jnp.sum(jnp.where(hit1, before, 0.0), axis=-1, keepdims=True)
    rank2 = jnp.sum(jnp.where(hit2, before, 0.0), axis=-1, keepdims=True)
    carry_ref[...] = carry_ref[...] + jnp.sum(onehot, axis=0, keepdims=True)
    cnt_ref[...] = carry_ref[...]

    rec = jnp.zeros(logits.shape, F32)
    for pos, val in ((R_E1, e1.astype(F32)), (R_E2, e2.astype(F32)), (R_W1, w1), (R_W2, w2),
                     (R_RANK1, rank1), (R_RANK2, rank2)):
        rec = jnp.where(lane == pos, val, rec)
    r_ref[...] = rec


def _route(hs, g, w_r, b_r):
    h0, h1 = hs
    D = h0.shape[1]
    T = h0.shape[0] + h1.shape[0]
    assert D == 2 * SUBLANES * LANES and h0.shape[0] % TM == 0 and h1.shape[0] % TM == 0
    n0 = h0.shape[0] // TM
    tri = (jnp.arange(TM)[:, None] > jnp.arange(TM)[None, :]).astype(BF16)
    w_hi = w_r.astype(BF16)
    w_lo = (w_r - w_hi.astype(F32)).astype(BF16)
    kern = functools.partial(_route_kernel, n0=n0)
    return pl.pallas_call(
        kern,
        out_shape=(jax.ShapeDtypeStruct((T, LANES), F32), jax.ShapeDtypeStruct((SUBLANES, LANES), F32),
                   jax.ShapeDtypeStruct((T * SUBLANES, LANES), U32)),
        grid=(T // TM,),
        in_specs=[
            pl.BlockSpec((TM, D), lambda i: (jnp.minimum(i, n0 - 1), 0)),
            pl.BlockSpec((TM, D), lambda i: (jnp.maximum(i - n0, 0), 0)),
            _resident((1, D)),
            _resident((D, 2 * LANES)),
            _resident((1, LANES)),
            _resident((TM, TM)),
        ],
        out_specs=(pl.BlockSpec((TM, LANES), lambda i: (i, 0)),
                   pl.BlockSpec((SUBLANES, LANES), lambda i: (0, 0)),
                   pl.BlockSpec((TM * SUBLANES, LANES), lambda i: (i, 0))),
        scratch_shapes=[pltpu.VMEM((SUBLANES, LANES), F32)],
        compiler_params=_params("arbitrary"),
        name="moe_route",
    )(h0, h1, g, jnp.concatenate([w_hi, w_lo], axis=1), b_r, tri)


def _tile_copy(src, src_row, dst, dst_row, sem):
    return pltpu.make_async_copy(src.at[pl.ds(pl.multiple_of(src_row * SUBLANES, SUBLANES), SUBLANES), :],
                                 dst.at[pl.ds(pl.multiple_of(dst_row * SUBLANES, SUBLANES), SUBLANES), :], sem)


def _dispatch_kernel(dest_ref, fill_ref, a_ref, xb_hbm, zero_ref, sem, *, rows, n_blocks):
    i = pl.program_id(0)
    base = i * rows

    def start(r, c):
        for k in range(TOP_K):
            _tile_copy(a_ref, r, xb_hbm, dest_ref[TOP_K * (base + r) + k], sem).start(priority=k)
        return c

    lax.fori_loop(0, rows, start, 0, unroll=8)
    for k in range(TOP_K):
        pltpu.make_async_copy(a_ref, xb_hbm.at[pl.ds(0, rows * SUBLANES), :], sem).wait()

    @pl.when(i == pl.num_programs(0) - 1)
    def _():
        zero_ref[...] = jnp.zeros_like(zero_ref)

        def per_expert(e, c):
            lo = fill_ref[2 * e]
            hi = fill_ref[2 * e + 1]

            def zstart(r, cc):
                _tile_copy(zero_ref, 0, xb_hbm, r, sem).start()
                return cc

            def zwait(r, cc):
                _tile_copy(zero_ref, 0, xb_hbm, r, sem).wait()
                return cc

            lax.fori_loop(lo, hi, zstart, 0)
            lax.fori_loop(lo, hi, zwait, 0)
            return c

        lax.fori_loop(0, N_EXPERTS, per_expert, 0)

        blk = zero_ref.shape[0]

        def block_copy(b):
            return pltpu.make_async_copy(zero_ref, xb_hbm.at[pl.ds(pl.multiple_of(b * blk, blk), blk), :], sem)

        def bstart(b, c):
            block_copy(b).start()
            return c

        def bwait(b, c):
            block_copy(b).wait()
            return c

        first_unused = fill_ref[2 * N_EXPERTS]
        lax.fori_loop(first_unused, n_blocks, bstart, 0)
        lax.fori_loop(first_unused, n_blocks, bwait, 0)


def _dispatch(a_slab, dest, fill, n_blocks, bm):
    T = a_slab.shape[0] // SUBLANES
    rows = MOVE_ROWS
    kern = functools.partial(_dispatch_kernel, rows=rows, n_blocks=n_blocks)
    return pl.pallas_call(
        kern,
        out_shape=jax.ShapeDtypeStruct((n_blocks * bm * SUBLANES, LANES), U32),
        grid_spec=pltpu.PrefetchScalarGridSpec(
            num_scalar_prefetch=2,
            grid=(T // rows,),
            in_specs=[pl.BlockSpec((rows * SUBLANES, LANES), lambda i, d, f: (i, 0))],
            out_specs=pl.BlockSpec(memory_space=pl.ANY),
            scratch_shapes=[pltpu.VMEM((bm * SUBLANES, LANES), U32), pltpu.SemaphoreType.DMA(())],
        ),
        compiler_params=_params("arbitrary"),
        name="moe_dispatch",
    )(dest, fill, a_slab)


M_USED, M_RUNS, M_EXPERT0 = 0, 1, 2


def _expert_kernel(be_ref, run_ref, meta_ref, x_ref, wg_hbm, wu_hbm, wd_hbm, o_ref,
                   x_s, wg_s, wu_s, wd_s, wg_f, wu_f, wd_f, sem, *, bm, layer):
    b = pl.program_id(0)
    n_used = meta_ref[M_USED]
    r = run_ref[b]

    def weight_copies(run, slot):
        e = meta_ref[M_EXPERT0 + run]
        return [pltpu.make_async_copy(w.at[layer, e], f.at[slot], sem.at[slot, j])
                for j, (w, f) in enumerate(((wg_hbm, wg_f), (wu_hbm, wu_f), (wd_hbm, wd_f)))]

    @pl.when((b < n_used) & ((b == 0) | (be_ref[b] != be_ref[jnp.maximum(b - 1, 0)])))
    def _():
        slot = r % 2

        @pl.when(r == 0)
        def _():
            for cp in weight_copies(0, 0):
                cp.start()

        for cp in weight_copies(r, slot):
            cp.wait()
        wg_s[...] = wg_f[slot].astype(BF16)
        wu_s[...] = wu_f[slot].astype(BF16)
        wd_s[...] = wd_f[slot].astype(BF16)

        @pl.when(r + 1 < meta_ref[M_RUNS])
        def _():
            for cp in weight_copies(r + 1, 1 - slot):
                cp.start()

    @pl.when(b < n_used)
    def _():
        half = SUBLANES * LANES
        for c in range(SUBLANES):
            lo, hi = _unpack_rows(x_ref, bm, c)
            x_s[:, c * LANES:(c + 1) * LANES] = lo.astype(BF16)
            x_s[:, half + c * LANES:half + (c + 1) * LANES] = hi.astype(BF16)
        a = x_s[...]
        hg = jnp.dot(a, wg_s[...], preferred_element_type=F32)
        hu = jnp.dot(a, wu_s[...], preferred_element_type=F32)
        h = hg * jax.nn.sigmoid(hg) * hu
        y = jnp.dot(h.astype(BF16), wd_s[...], preferred_element_type=F32)
        _pack_rows(y, o_ref, bm)

    @pl.when(b >= n_used)
    def _():
        o_ref[...] = jnp.zeros_like(o_ref)


def _experts(xb, w_gate, w_up, w_down, layer, blk_e, blk_run, meta):
    bm = EXPERT_BLOCK
    n_blocks = xb.shape[0] // (bm * SUBLANES)
    _, _, D, De = w_gate.shape
    kern = functools.partial(_expert_kernel, bm=bm, layer=layer)
    hbm = pl.BlockSpec(memory_space=pl.ANY)
    return pl.pallas_call(
        kern,
        out_shape=jax.ShapeDtypeStruct(xb.shape, U32),
        grid_spec=pltpu.PrefetchScalarGridSpec(
            num_scalar_prefetch=3,
            grid=(n_blocks,),
            in_specs=[
                pl.BlockSpec((bm * SUBLANES, LANES), lambda b, be, run, meta: (jnp.minimum(b, meta[M_USED] - 1), 0)),
                hbm, hbm, hbm,
            ],
            out_specs=pl.BlockSpec((bm * SUBLANES, LANES), lambda b, be, run, meta: (b, 0)),
            scratch_shapes=[pltpu.VMEM((bm, D), BF16), pltpu.VMEM((D, De), BF16),
                            pltpu.VMEM((D, De), BF16), pltpu.VMEM((De, D), BF16),
                            pltpu.VMEM((2, D, De), F32), pltpu.VMEM((2, D, De), F32), pltpu.VMEM((2, De, D), F32),
                            pltpu.SemaphoreType.DMA((2, 3))],
        ),
        compiler_params=_params("arbitrary"),
        name="moe_experts",
    )(blk_e, blk_run, meta, xb, w_gate, w_up, w_down)


def _hier_moe(hs, g, w_rg, b_rg, w_re, b_re, w_gate, w_up, w_down, layer):
    D = hs[0].shape[1]
    T = hs[0].shape[0] + hs[1].shape[0]
    pad = LANES - N_EXPERT_GROUPS - N_EXPERTS
    w_r = jnp.concatenate([w_rg, w_re, jnp.zeros((D, pad), F32)], axis=1)
    b_r = jnp.concatenate([b_rg, b_re, jnp.zeros((pad,), F32)])[None, :]
    rec, cnt, a_slab = _route(hs, g, w_r, b_r)

    bm = EXPERT_BLOCK
    e_tok = rec[:, R_E1:R_E2 + 1].astype(jnp.int32)
    rank = rec[:, R_RANK1:R_RANK2 + 1].astype(jnp.int32)
    counts = cnt[0, :N_EXPERTS].astype(jnp.int32)
    pad_counts = (counts + bm - 1) // bm * bm
    pad_end = jnp.cumsum(pad_counts)
    pad_start = pad_end - pad_counts
    start_tok = jnp.sum(jnp.where(e_tok[..., None] == jnp.arange(N_EXPERTS, dtype=jnp.int32), pad_start, 0), axis=-1)
    dest = (start_tok + rank).reshape(-1)
    n_blocks = (T * TOP_K) // bm + N_EXPERTS
    blk_start = jnp.arange(n_blocks, dtype=jnp.int32) * bm
    blk_e = jnp.minimum(jnp.sum(blk_start[:, None] >= pad_end[None, :], axis=1), N_EXPERTS - 1).astype(jnp.int32)
    n_used = (pad_end[-1:] // bm).astype(jnp.int32)
    fill = jnp.concatenate([jnp.stack([pad_start + counts, pad_end], axis=1).reshape(-1), n_used]).astype(jnp.int32)
    blk = jnp.arange(n_blocks, dtype=jnp.int32)
    run_starts = (blk < n_used[0]) & ((blk == 0) | (blk_e != jnp.roll(blk_e, 1)))
    blk_run = (jnp.cumsum(run_starts) - 1).astype(jnp.int32)
    e_ids = jnp.arange(N_EXPERTS, dtype=jnp.int32)
    run_experts = jnp.sort(jnp.where(counts > 0, e_ids, e_ids + N_EXPERTS)) % N_EXPERTS
    meta = jnp.concatenate([n_used, jnp.sum(run_starts, keepdims=True), run_experts]).astype(jnp.int32)

    xb = _dispatch(a_slab, dest, fill, n_blocks, bm)
    yb = _experts(xb, w_gate, w_up, w_down, layer, blk_e, blk_run, meta)
    return (_combine(hs[0], rec, yb, dest, 0), _combine(hs[1], rec, yb, dest, hs[0].shape[0]))


def _combine_kernel(dest_ref, h_ref, r_ref, yb_hbm, o_ref, y_ref, sem, *, rows, tok0):
    i = pl.program_id(0)

    def issue(step, slot):
        base = tok0 + step * rows

        def start(r, c):
            for k in range(TOP_K):
                _tile_copy(yb_hbm, dest_ref[TOP_K * (base + r) + k], y_ref.at[slot, k], r,
                           sem.at[slot]).start(priority=k)
            return c

        lax.fori_loop(0, rows, start, 0, unroll=8)

    def finish(slot):
        for k in range(TOP_K):
            pltpu.make_async_copy(yb_hbm.at[pl.ds(0, rows * SUBLANES), :], y_ref.at[slot, k], sem.at[slot]).wait()
        rec = r_ref[...]
        w1 = rec[:, R_W1:R_W1 + 1]
        w2 = rec[:, R_W2:R_W2 + 1]
        half = SUBLANES * LANES
        for c in range(SUBLANES):
            lo1, hi1 = _unpack_rows(y_ref.at[slot, 0], rows, c)
            lo2, hi2 = _unpack_rows(y_ref.at[slot, 1], rows, c)
            sl = slice(c * LANES, (c + 1) * LANES)
            sh = slice(half + c * LANES, half + (c + 1) * LANES)
            o_ref[:, sl] = h_ref[:, sl] + (w1 * lo1 + w2 * lo2)
            o_ref[:, sh] = h_ref[:, sh] + (w1 * hi1 + w2 * hi2)

    @pl.when(i == 0)
    def _():
        issue(0, 0)

    for slot in range(2):
        @pl.when(i % 2 == slot)
        def _():
            @pl.when(i + 1 < pl.num_programs(0))
            def _():
                issue(i + 1, 1 - slot)

            finish(slot)


def _combine(h, rec, yb, dest, tok0):
    Tg, D = h.shape
    rows = COMBINE_ROWS
    assert tok0 % rows == 0 and Tg % rows == 0
    blk0 = tok0 // rows
    kern = functools.partial(_combine_kernel, rows=rows, tok0=tok0)
    return pl.pallas_call(
        kern,
        out_shape=jax.ShapeDtypeStruct((Tg, D), F32),
        grid_spec=pltpu.PrefetchScalarGridSpec(
            num_scalar_prefetch=1,
            grid=(Tg // rows,),
            in_specs=[
                pl.BlockSpec((rows, D), lambda i, d: (i, 0)),
                pl.BlockSpec((rows, LANES), lambda i, d: (blk0 + i, 0)),
                pl.BlockSpec(memory_space=pl.ANY),
            ],
            out_specs=pl.BlockSpec((rows, D), lambda i, d: (i, 0)),
            scratch_shapes=[pltpu.VMEM((2, TOP_K, rows * SUBLANES, LANES), U32), pltpu.SemaphoreType.DMA((2,))],
        ),
        compiler_params=_params("arbitrary"),
        name="moe_combine",
    )(dest, h, rec, yb)


def _ple_math(h_ref, g_ref, wg_ref, p_ref, wp_ref):
    h = h_ref[...]
    a = _rms(h, g_ref[...]).astype(BF16)
    gate = jax.nn.sigmoid(jnp.dot(a, wg_ref[...], preferred_element_type=F32))
    emb = jnp.dot(p_ref[...].astype(BF16), wp_ref[...], preferred_element_type=F32)
    return h + gate * emb


def _ple_kernel(h_ref, g_ref, wg_ref, p_ref, wp_ref, o_ref):
    o_ref[...] = _ple_math(h_ref, g_ref, wg_ref, p_ref, wp_ref)


def _ple_final_kernel(h_ref, g_ref, wg_ref, p_ref, wp_ref, gf_ref, o_ref):
    o_ref[...] = _rms(_ple_math(h_ref, g_ref, wg_ref, p_ref, wp_ref), gf_ref[...])


def _ple(h, g, w_gate, p, layer, w_proj, g_final=None):
    T, D = h.shape
    P = p.shape[2]
    in_specs = [
        pl.BlockSpec((TM, D), lambda i: (i, 0)),
        _resident((1, D)),
        _resident((D, D)),
        pl.BlockSpec((None, TM, P), lambda i: (layer, i, 0)),
        _resident((P, D)),
    ]
    args = (h, g, w_gate, p, w_proj)
    if g_final is not None:
        in_specs.append(_resident((1, D)))
        args += (g_final,)
    return pl.pallas_call(
        _ple_kernel if g_final is None else _ple_final_kernel,
        out_shape=jax.ShapeDtypeStruct((T, D), F32),
        grid=(T // TM,),
        in_specs=in_specs,
        out_specs=pl.BlockSpec((TM, D), lambda i: (i, 0)),
        compiler_params=_params("parallel"),
        name="ple" if g_final is None else "ple_final",
    )(*args)


def _dft_tables(n, scale, rows=None):
    blk = 64
    assert n % blk == 0
    rows = n if rows is None else rows
    j = jnp.arange(rows, dtype=jnp.int32)

    def cos_sin(k):
        ang = ((j[:, None] * k[None, :]) % n).astype(F32) * (2.0 * np.pi / n)
        return jnp.cos(ang), jnp.sin(ang)

    ca, sa = cos_sin(jnp.arange(n // blk, dtype=jnp.int32) * blk)
    cb, sb = cos_sin(jnp.arange(blk, dtype=jnp.int32))
    ca, sa, cb, sb = ca[:, :, None], sa[:, :, None], cb[:, None, :], sb[:, None, :]
    c = (ca * cb - sa * sb) * scale
    s = (sa * cb + ca * sb) * scale
    return c.reshape(rows, n).astype(BF16), s.reshape(rows, n).astype(BF16)


def _norm_nyq_kernel(x_ref, g_ref, a_ref, nyq_ref):
    a = _rms(x_ref[...], g_ref[...]).astype(BF16)
    a_ref[...] = a
    row = lax.broadcasted_iota(jnp.int32, (a.shape[0], 1), 0)
    sign = jnp.where(row % 2 == 0, 1.0, -1.0)
    part = jnp.sum(a.astype(F32) * sign, axis=0, keepdims=True)

    @pl.when(pl.program_id(1) == 0)
    def _():
        nyq_ref[...] = jnp.zeros_like(nyq_ref)

    nyq_ref[...] += jnp.broadcast_to(part, nyq_ref.shape[1:])[None]


def _norm_nyq(h, g, batch, seq):
    T, D = h.shape
    nt = seq // TM
    return pl.pallas_call(
        _norm_nyq_kernel,
        out_shape=(jax.ShapeDtypeStruct((T, D), BF16), jax.ShapeDtypeStruct((batch, BF16_ROWS, D), F32)),
        grid=(batch, nt),
        in_specs=[pl.BlockSpec((TM, D), lambda b, i: (b * nt + i, 0)), _resident((1, D))],
        out_specs=(pl.BlockSpec((TM, D), lambda b, i: (b * nt + i, 0)),
                   pl.BlockSpec((1, BF16_ROWS, D), lambda b, i: (b, 0, 0))),
        compiler_params=_params("parallel", "arbitrary"),
        name="fourier_norm",
    )(h, g)


def _half_dft_kernel(c_ref, s_ref, a_ref, p_ref, q_ref, accp_ref, accq_ref):
    k = pl.program_id(3)

    @pl.when(k == 0)
    def _():
        accp_ref[...] = jnp.zeros_like(accp_ref)
        accq_ref[...] = jnp.zeros_like(accq_ref)

    a = a_ref[...]
    accp_ref[...] += jnp.dot(c_ref[...], a, preferred_element_type=F32)
    accq_ref[...] += jnp.dot(s_ref[...], a, preferred_element_type=F32)

    @pl.when(k == pl.num_programs(3) - 1)
    def _():
        p_ref[...] = accp_ref[...].astype(BF16)
        q_ref[...] = accq_ref[...].astype(BF16)


def _half_dft(a, c, s, batch, seq):
    D = a.shape[1]
    half = seq // 2
    tm, tn, tk = min(DFT_TM, half), DFT_TN, min(DFT_TK, seq)
    out = jax.ShapeDtypeStruct((batch * half, D), BF16)
    ospec = pl.BlockSpec((tm, tn), lambda b, i, j, k: (b * (half // tm) + i, j))
    return pl.pallas_call(
        _half_dft_kernel,
        out_shape=(out, out),
        grid=(batch, half // tm, D // tn, seq // tk),
        in_specs=[
            pl.BlockSpec((tm, tk), lambda b, i, j, k: (i, k)),
            pl.BlockSpec((tm, tk), lambda b, i, j, k: (i, k)),
            pl.BlockSpec((tk, tn), lambda b, i, j, k: (b * (seq // tk) + k, j)),
        ],
        out_specs=(ospec, ospec),
        scratch_shapes=[pltpu.VMEM((tm, tn), F32), pltpu.VMEM((tm, tn), F32)],
        compiler_params=_params("parallel", "parallel", "parallel", "arbitrary"),
        name="fourier_half_dft",
    )(c, s, a)


def _chan_dft(p, q, c_ref, s_ref, sign, gd):
    outs = []
    for grp in range(p.shape[1] // gd):
        sl = slice(grp * gd, (grp + 1) * gd)
        pc = jnp.dot(p[:, sl], c_ref[...], preferred_element_type=F32)
        qs = jnp.dot(q[:, sl], s_ref[...], preferred_element_type=F32)
        outs.append(pc + sign * qs)
    return jnp.concatenate(outs, axis=1)


def _fourier_out_kernel(pa_ref, qa_ref, pb_ref, qb_ref, nyq_ref, c_ref, s_ref, rev_ref, w_ref, h_ref, o_ref,
                        *, n_lo, gd):
    i = pl.program_id(1)

    @pl.when(i < n_lo)
    def _():
        f = _chan_dft(pa_ref[...], qa_ref[...], c_ref, s_ref, -1.0, gd)
        o_ref[...] = h_ref[...] + jnp.dot(f.astype(BF16), w_ref[...], preferred_element_type=F32)

    @pl.when(i >= n_lo)
    def _():
        m = _chan_dft(pa_ref[...], qa_ref[...], c_ref, s_ref, 1.0, gd).astype(BF16)
        rev = jnp.dot(rev_ref[...], m, preferred_element_type=F32)
        mb = _chan_dft(pb_ref[...], qb_ref[...], c_ref, s_ref, 1.0, gd)[0:1, :]
        nyq = nyq_ref[0].astype(BF16)
        mn = _chan_dft(nyq, jnp.zeros_like(nyq), c_ref, s_ref, 1.0, gd)[0:1, :]
        first = jnp.where(i == n_lo, mn, mb)
        row = lax.broadcasted_iota(jnp.int32, rev.shape, 0)
        f = jnp.where(row == 0, first, rev)
        o_ref[...] = h_ref[...] + jnp.dot(f.astype(BF16), w_ref[...], preferred_element_type=F32)


def _fourier_out(p, q, nyq, cc, sc, w, h, batch, seq):
    D = h.shape[1]
    gd = cc.shape[0]
    tm = TM
    half = seq // 2
    assert half % tm == 0
    n_lo = half // tm
    nt = seq // tm
    r = jnp.arange(tm)
    rev = ((r[:, None] + r[None, :]) == tm).astype(BF16)

    def a_map(b, i):
        return (b * n_lo + jnp.where(i < n_lo, i, nt - 1 - i), 0)

    def b_map(b, i):
        t = jnp.where(i < n_lo, i, jnp.minimum(nt - i, n_lo - 1))
        return ((b * n_lo + t) * (tm // BF16_ROWS), 0)

    kern = functools.partial(_fourier_out_kernel, n_lo=n_lo, gd=gd)
    return pl.pallas_call(
        kern,
        out_shape=jax.ShapeDtypeStruct(h.shape, F32),
        grid=(batch, nt),
        in_specs=[
            pl.BlockSpec((tm, D), a_map),
            pl.BlockSpec((tm, D), a_map),
            pl.BlockSpec((BF16_ROWS, D), b_map),
            pl.BlockSpec((BF16_ROWS, D), b_map),
            pl.BlockSpec((1, BF16_ROWS, D), lambda b, i: (b, 0, 0)),
            _resident((gd, gd)),
            _resident((gd, gd)),
            _resident((tm, tm)),
            _resident((D, D)),
            pl.BlockSpec((tm, D), lambda b, i: (b * nt + i, 0)),
        ],
        out_specs=pl.BlockSpec((tm, D), lambda b, i: (b * nt + i, 0)),
        compiler_params=_params("parallel", "parallel"),
        name="fourier_out",
    )(p, q, p, q, nyq, cc, sc, rev, w, h)


def _fourier_mix(h, g, w_out, cc, sc, batch, seq):
    scale = 1.0 / float(np.sqrt(seq))
    cs, sn = _dft_tables(seq, scale, rows=seq // 2)
    a, nyq = _norm_nyq(h, g, batch, seq)
    p, q = _half_dft(a, cs, sn, batch, seq)
    return _fourier_out(p, q, nyq * scale, cc, sc, w_out, h, batch, seq)


def kernel(x_prompt, x_sample, p_prompt, p_sample, g_mix, w_qkv, g_q, g_k, w_attn_out, w_fourier_out, g_ffn, w_route_group, b_route_group, w_route_expert, b_route_expert, w_exp_gate, w_exp_up, w_exp_down, g_ple, w_ple_gate, w_ple_proj, g_final):
    depth = g_mix.shape[0]
    bp, sp, D = x_prompt.shape
    bs, ss, _ = x_sample.shape
    tp, ts = bp * sp, bs * ss
    gd = D // N_FOURIER_GROUPS
    geom = ((bs, ss), (bp, sp))
    hs = (x_sample.reshape(ts, D), x_prompt.reshape(tp, D))
    ps = (p_sample, p_prompt)

    n_mixers = 2
    for i in range(depth):
        jm = i // n_mixers
        g_i = g_mix[i][None, :]
        if i % n_mixers == 0:
            w, gq2, gk2 = _qkv_weights(w_qkv[jm], g_q[jm], g_k[jm])
            cos, sin = _rope_tables(max(sp, ss))
            w_o = w_attn_out[jm].astype(BF16)
            mixed = []
            for h, (batch, seq) in zip(hs, geom):
                qkv = _qkv_proj(h, g_i, w, gq2, gk2, cos, sin, seq)
                mixed.append(_matmul_residual(_attention(qkv, 0, batch, seq), w_o, h))
        else:
            cc, sc = _dft_tables(gd, 1.0 / float(np.sqrt(gd)))
            w_o = w_fourier_out[jm].astype(BF16)
            mixed = [_fourier_mix(h, g_i, w_o, cc, sc, batch, seq) for h, (batch, seq) in zip(hs, geom)]
        hs = _hier_moe(tuple(mixed), g_ffn[i][None, :], w_route_group[i], b_route_group[i], w_route_expert[i],
                       b_route_expert[i], w_exp_gate, w_exp_up, w_exp_down, i)
        w_pg = w_ple_gate[i].astype(BF16)
        w_pp = w_ple_proj[i].astype(BF16)
        g_f = g_final[None, :] if i == depth - 1 else None
        hs = tuple(_ple(h, g_ple[i][None, :], w_pg, p.reshape(depth, h.shape[0], -1), i, w_pp, g_f)
                   for h, p in zip(hs, ps))
    y_sample, y_prompt = hs
    return (y_prompt.reshape(bp, sp, D), y_sample.reshape(bs, ss, D))
```

```python
import functools

import jax
import jax.numpy as jnp
import numpy as np
from jax import lax
from jax.experimental import pallas as pl
from jax.experimental.pallas import tpu as pltpu

F32 = jnp.float32
BF16 = jnp.bfloat16
U32 = jnp.uint32

HEAD_DIM = 128
N_HEADS = 16
N_KV_HEADS = 8
Q_PER_KV = N_HEADS // N_KV_HEADS
ROPE_AXIS_DIM = HEAD_DIM // 2
ROPE_THETA = 10000.0
GRID_W = 64
N_FOURIER_GROUPS = 8
N_EXPERT_GROUPS = 4
EXPERTS_PER_GROUP = 8
N_EXPERTS = N_EXPERT_GROUPS * EXPERTS_PER_GROUP
TOP_K = 2
NORM_EPS = 1e-6

LANES = 128
SUBLANES = 8
BF16_ROWS = 16
MXU_DIM = 256
VMEM_LIMIT = 56 * 1024 * 1024

TM = 512
TQ = 1024
TKV = 1024
EXPERT_BLOCK = 256
MOVE_ROWS = 512
COMBINE_ROWS = 512
DFT_TM, DFT_TN, DFT_TK = 1024, 1024, 2048


def _params(*sem):
    return pltpu.CompilerParams(dimension_semantics=sem, vmem_limit_bytes=VMEM_LIMIT)


def _resident(shape):
    return pl.BlockSpec(shape, lambda *_: (0,) * len(shape), pipeline_mode=pl.Buffered(1))


def _rms(x, g):
    ms = jnp.mean(x * x, axis=-1, keepdims=True)
    return x * lax.rsqrt(ms + NORM_EPS) * g


def _pack_rows(x, ref, rows):
    half = SUBLANES * LANES
    for c in range(SUBLANES):
        lo = x[:, c * LANES:(c + 1) * LANES].astype(BF16).astype(F32)
        hi = x[:, half + c * LANES:half + (c + 1) * LANES].astype(BF16).astype(F32)
        word = (pltpu.bitcast(lo, U32) >> 16) | (pltpu.bitcast(hi, U32) & jnp.uint32(0xFFFF0000))
        ref[pl.ds(c, rows, stride=SUBLANES), :] = word


def _unpack_rows(ref, rows, c):
    word = ref[pl.ds(c, rows, stride=SUBLANES), :]
    lo = pltpu.bitcast(word << 16, F32)
    hi = pltpu.bitcast(word & jnp.uint32(0xFFFF0000), F32)
    return lo, hi


def _qkv_kernel(x_ref, g_ref, w_ref, gq_ref, gk_ref, cos_ref, sin_ref, o_ref, *, d_q, d_qk):
    a = _rms(x_ref[...], g_ref[...]).astype(BF16)
    cos = cos_ref[...]
    sin = sin_ref[...]
    q_scale = float(np.log2(np.e) / np.sqrt(HEAD_DIM))
    tabs = {
        "q": (gq_ref[0:1, :] * cos * q_scale, gq_ref[1:2, :] * sin * q_scale),
        "k": (gk_ref[0:1, :] * cos, gk_ref[1:2, :] * sin),
    }
    n_out = o_ref.shape[1]
    for c0 in range(0, n_out, MXU_DIM):
        acc = jnp.dot(a, w_ref[:, c0:c0 + MXU_DIM], preferred_element_type=F32)
        if c0 >= d_qk:
            o_ref[:, c0:c0 + MXU_DIM] = acc.astype(BF16)
            continue
        t1, t2 = tabs["q" if c0 < d_q else "k"]
        for hh in range(MXU_DIM // HEAD_DIM):
            y = acc[:, hh * HEAD_DIM:(hh + 1) * HEAD_DIM]
            r = lax.rsqrt(jnp.mean(y * y, axis=-1, keepdims=True) + NORM_EPS)
            rot = (y * t1 + pltpu.roll(y, HEAD_DIM // 2, 1) * t2) * r
            o_ref[:, c0 + hh * HEAD_DIM:c0 + (hh + 1) * HEAD_DIM] = rot.astype(BF16)


def _permute_heads(x):
    lead = x.shape[:-1]
    x = x.reshape(lead + (-1, 2, 2, HEAD_DIM // 4))
    return jnp.swapaxes(x, -3, -2).reshape(lead + (-1,))


def _rope_tables(n_pos):
    pos = jnp.arange(n_pos, dtype=jnp.int32)
    row = (pos // GRID_W).astype(F32)
    col = (pos % GRID_W).astype(F32)
    inv_freq = ROPE_THETA ** (-jnp.arange(0, ROPE_AXIS_DIM, 2, dtype=F32) / ROPE_AXIS_DIM)
    ang_r = row[:, None] * inv_freq[None, :]
    ang_c = col[:, None] * inv_freq[None, :]
    cos = jnp.concatenate([jnp.cos(ang_r), jnp.cos(ang_c), jnp.cos(ang_r), jnp.cos(ang_c)], axis=-1)
    sin = jnp.concatenate([-jnp.sin(ang_r), -jnp.sin(ang_c), jnp.sin(ang_r), jnp.sin(ang_c)], axis=-1)
    return cos, sin


def _qkv_weights(w_qkv, g_q, g_k):
    d_qk = (N_HEADS + N_KV_HEADS) * HEAD_DIM
    w = jnp.concatenate([_permute_heads(w_qkv[:, :d_qk]).astype(BF16), w_qkv[:, d_qk:].astype(BF16)], axis=1)
    gq = _permute_heads(g_q)
    gk = _permute_heads(g_k)
    gq2 = jnp.stack([gq, jnp.roll(gq, HEAD_DIM // 2)])
    gk2 = jnp.stack([gk, jnp.roll(gk, HEAD_DIM // 2)])
    return w, gq2, gk2


def _qkv_proj(h, g, w, gq2, gk2, cos, sin, seq):
    T, D = h.shape
    N = w.shape[1]
    assert seq % TM == 0 and T % seq == 0
    d_q = N_HEADS * HEAD_DIM
    d_qk = d_q + N_KV_HEADS * HEAD_DIM

    def pos_map(i):
        return (i % (seq // TM), 0)

    kern = functools.partial(_qkv_kernel, d_q=d_q, d_qk=d_qk)
    return pl.pallas_call(
        kern,
        out_shape=jax.ShapeDtypeStruct((T, N), BF16),
        grid=(T // TM,),
        in_specs=[
            pl.BlockSpec((TM, D), lambda i: (i, 0)),
            _resident((1, D)),
            _resident((D, N)),
            _resident((2, HEAD_DIM)),
            _resident((2, HEAD_DIM)),
            pl.BlockSpec((TM, HEAD_DIM), pos_map),
            pl.BlockSpec((TM, HEAD_DIM), pos_map),
        ],
        out_specs=pl.BlockSpec((TM, N), lambda i: (i, 0)),
        compiler_params=_params("parallel"),
        name="qkv_proj",
    )(h, g, w, gq2, gk2, cos, sin)


def _attn_kernel(q_ref, k_ref, v_ref, o_ref, *, tq, tk, seq):
    q = q_ref[...]
    q2 = jnp.concatenate([q[:, :HEAD_DIM], q[:, HEAD_DIM:]], axis=0)
    m = jnp.full((2 * tq, 1), -jnp.inf, F32)
    l = jnp.zeros((2 * tq, 1), F32)
    acc = jnp.zeros((2 * tq, HEAD_DIM), F32)
    for c in range(seq // tk):
        k = k_ref[c * tk:(c + 1) * tk, :]
        v = v_ref[c * tk:(c + 1) * tk, :]
        s = lax.dot_general(q2, k, (((1,), (1,)), ((), ())), preferred_element_type=F32)
        m_new = jnp.maximum(m, jnp.max(s, axis=-1, keepdims=True))
        alpha = jnp.exp2(m - m_new)
        p = jnp.exp2(s - m_new)
        l = alpha * l + jnp.sum(p, axis=-1, keepdims=True)
        acc = alpha * acc + jnp.dot(p.astype(BF16), v, preferred_element_type=F32)
        m = m_new
    o = acc / l
    o_ref[:, :HEAD_DIM] = o[:tq].astype(BF16)
    o_ref[:, HEAD_DIM:] = o[tq:].astype(BF16)


def _attention(qkv, row0, batch, seq):
    assert row0 % seq == 0 and seq % TQ == 0 and seq % TKV == 0
    qw = Q_PER_KV * HEAD_DIM
    kern = functools.partial(_attn_kernel, tq=TQ, tk=TKV, seq=seq)
    q_blk0 = row0 // TQ
    s_blk0 = row0 // seq
    k_col0 = N_HEADS
    v_col0 = N_HEADS + N_KV_HEADS
    return pl.pallas_call(
        kern,
        out_shape=jax.ShapeDtypeStruct((batch * seq, N_HEADS * HEAD_DIM), BF16),
        grid=(batch, N_KV_HEADS, seq // TQ),
        in_specs=[
            pl.BlockSpec((TQ, qw), lambda b, h, i: (q_blk0 + b * (seq // TQ) + i, h)),
            pl.BlockSpec((seq, HEAD_DIM), lambda b, h, i: (s_blk0 + b, k_col0 + h)),
            pl.BlockSpec((seq, HEAD_DIM), lambda b, h, i: (s_blk0 + b, v_col0 + h)),
        ],
        out_specs=pl.BlockSpec((TQ, qw), lambda b, h, i: (b * (seq // TQ) + i, h)),
        compiler_params=_params("parallel", "parallel", "arbitrary"),
        name="attention",
    )(qkv, qkv, qkv)


def _mm_res_kernel(x_ref, w_ref, r_ref, o_ref):
    o_ref[...] = r_ref[...] + jnp.dot(x_ref[...], w_ref[...], preferred_element_type=F32)


def _matmul_residual(x, w, res):
    T, K = x.shape
    N = w.shape[1]
    return pl.pallas_call(
        _mm_res_kernel,
        out_shape=jax.ShapeDtypeStruct((T, N), F32),
        grid=(T // TM,),
        in_specs=[
            pl.BlockSpec((TM, K), lambda i: (i, 0)),
            _resident((K, N)),
            pl.BlockSpec((TM, N), lambda i: (i, 0)),
        ],
        out_specs=pl.BlockSpec((TM, N), lambda i: (i, 0)),
        compiler_params=_params("parallel"),
        name="matmul_residual",
    )(x, w, res)


R_E1, R_E2, R_W1, R_W2, R_RANK1, R_RANK2 = range(6)
ROUTE_LOGIT_E0 = N_EXPERT_GROUPS


def _route_kernel(x0_ref, x1_ref, g_ref, w_ref, b_ref, tri_ref, r_ref, cnt_ref, a_ref, carry_ref, *, n0):
    i = pl.program_id(0)

    @pl.when(i == 0)
    def _():
        carry_ref[...] = jnp.zeros_like(carry_ref)

    a = _rms(jnp.where(i < n0, x0_ref[...], x1_ref[...]), g_ref[...])
    _pack_rows(a, a_ref, a.shape[0])
    a_hi = a.astype(BF16)
    a_lo = (a - a_hi.astype(F32)).astype(BF16)
    hi_both = jnp.dot(a_hi, w_ref[...], preferred_element_type=F32)
    logits = (hi_both[:, :LANES]
              + (jnp.dot(a_lo, w_ref[:, :LANES], preferred_element_type=F32) + hi_both[:, LANES:])) + b_ref[...]
    lane = lax.broadcasted_iota(jnp.int32, logits.shape, 1)
    lane_f = lane.astype(F32)
    neg = -jnp.inf
    far = float(LANES)

    def first_max(vals):
        top = jnp.max(vals, axis=-1, keepdims=True)
        idx = jnp.min(jnp.where(vals == top, lane_f, far), axis=-1, keepdims=True)
        return top, idx.astype(jnp.int32)

    gmask = lane < N_EXPERT_GROUPS
    gtop, g_sel = first_max(jnp.where(gmask, logits, neg))
    pg = 1.0 / jnp.sum(jnp.where(gmask, jnp.exp(logits - gtop), 0.0), axis=-1, keepdims=True)
    lo = ROUTE_LOGIT_E0 + g_sel * EXPERTS_PER_GROUP
    le = jnp.where((lane >= lo) & (lane < lo + EXPERTS_PER_GROUP), logits, neg)
    t1, i1 = first_max(le)
    t2, i2 = first_max(jnp.where(lane == i1, neg, le))
    r21 = jnp.exp(t2 - t1)
    w1 = pg / (1.0 + r21)
    w2 = pg * r21 / (1.0 + r21)
    e1 = i1 - ROUTE_LOGIT_E0
    e2 = i2 - ROUTE_LOGIT_E0

    hit1 = lane == e1
    hit2 = lane == e2
    onehot = (hit1 | hit2).astype(F32)
    before = jnp.dot(tri_ref[...], onehot.astype(BF16), preferred_element_type=F32) + carry_ref[0:1, :]
    rank1 = jnp.sum(jnp.where(hit1, before, 0.0), axis=-1, keepdims=True)
    rank2 = jnp.sum(jnp.where(hit2, before, 0.0), axis=-1, keepdims=True)
    carry_ref[...] = carry_ref[...] + jnp.sum(onehot, axis=0, keepdims=True)
    cnt_ref[...] = carry_ref[...]

    rec = jnp.zeros(logits.shape, F32)
    for pos, val in ((R_E1, e1.astype(F32)), (R_E2, e2.astype(F32)), (R_W1, w1), (R_W2, w2),
                     (R_RANK1, rank1), (R_RANK2, rank2)):
        rec = jnp.where(lane == pos, val, rec)
    r_ref[...] = rec


def _route(hs, g, w_r, b_r):
    h0, h1 = hs
    D = h0.shape[1]
    T = h0.shape[0] + h1.shape[0]
    assert D == 2 * SUBLANES * LANES and h0.shape[0] % TM == 0 and h1.shape[0] % TM == 0
    n0 = h0.shape[0] // TM
    tri = (jnp.arange(TM)[:, None] > jnp.arange(TM)[None, :]).astype(BF16)
    w_hi = w_r.astype(BF16)
    w_lo = (w_r - w_hi.astype(F32)).astype(BF16)
    kern = functools.partial(_route_kernel, n0=n0)
    return pl.pallas_call(
        kern,
        out_shape=(jax.ShapeDtypeStruct((T, LANES), F32), jax.ShapeDtypeStruct((SUBLANES, LANES), F32),
                   jax.ShapeDtypeStruct((T * SUBLANES, LANES), U32)),
        grid=(T // TM,),
        in_specs=[
            pl.BlockSpec((TM, D), lambda i: (jnp.minimum(i, n0 - 1), 0)),
            pl.BlockSpec((TM, D), lambda i: (jnp.maximum(i - n0, 0), 0)),
            _resident((1, D)),
            _resident((D, 2 * LANES)),
            _resident((1, LANES)),
            _resident((TM, TM)),
        ],
        out_specs=(pl.BlockSpec((TM, LANES), lambda i: (i, 0)),
                   pl.BlockSpec((SUBLANES, LANES), lambda i: (0, 0)),
                   pl.BlockSpec((TM * SUBLANES, LANES), lambda i: (i, 0))),
        scratch_shapes=[pltpu.VMEM((SUBLANES, LANES), F32)],
        compiler_params=_params("arbitrary"),
        name="moe_route",
    )(h0, h1, g, jnp.concatenate([w_hi, w_lo], axis=1), b_r, tri)


def _tile_copy(src, src_row, dst, dst_row, sem):
    return pltpu.make_async_copy(src.at[pl.ds(pl.multiple_of(src_row * SUBLANES, SUBLANES), SUBLANES), :],
                                 dst.at[pl.ds(pl.multiple_of(dst_row * SUBLANES, SUBLANES), SUBLANES), :], sem)


def _dispatch_kernel(dest_ref, fill_ref, a_ref, xb_hbm, zero_ref, sem, *, rows, n_blocks):
    i = pl.program_id(0)
    base = i * rows

    def start(r, c):
        for k in range(TOP_K):
            _tile_copy(a_ref, r, xb_hbm, dest_ref[TOP_K * (base + r) + k], sem).start(priority=k)
        return c

    lax.fori_loop(0, rows, start, 0, unroll=8)
    for k in range(TOP_K):
        pltpu.make_async_copy(a_ref, xb_hbm.at[pl.ds(0, rows * SUBLANES), :], sem).wait()

    @pl.when(i == pl.num_programs(0) - 1)
    def _():
        zero_ref[...] = jnp.zeros_like(zero_ref)

        def per_expert(e, c):
            lo = fill_ref[2 * e]
            hi = fill_ref[2 * e + 1]

            def zstart(r, cc):
                _tile_copy(zero_ref, 0, xb_hbm, r, sem).start()
                return cc

            def zwait(r, cc):
                _tile_copy(zero_ref, 0, xb_hbm, r, sem).wait()
                return cc

            lax.fori_loop(lo, hi, zstart, 0)
            lax.fori_loop(lo, hi, zwait, 0)
            return c

        lax.fori_loop(0, N_EXPERTS, per_expert, 0)

        blk = zero_ref.shape[0]

        def block_copy(b):
            return pltpu.make_async_copy(zero_ref, xb_hbm.at[pl.ds(pl.multiple_of(b * blk, blk), blk), :], sem)

        def bstart(b, c):
            block_copy(b).start()
            return c

        def bwait(b, c):
            block_copy(b).wait()
            return c

        first_unused = fill_ref[2 * N_EXPERTS]
        lax.fori_loop(first_unused, n_blocks, bstart, 0)
        lax.fori_loop(first_unused, n_blocks, bwait, 0)


def _dispatch(a_slab, dest, fill, n_blocks, bm):
    T = a_slab.shape[0] // SUBLANES
    rows = MOVE_ROWS
    kern = functools.partial(_dispatch_kernel, rows=rows, n_blocks=n_blocks)
    return pl.pallas_call(
        kern,
        out_shape=jax.ShapeDtypeStruct((n_blocks * bm * SUBLANES, LANES), U32),
        grid_spec=pltpu.PrefetchScalarGridSpec(
            num_scalar_prefetch=2,
            grid=(T // rows,),
            in_specs=[pl.BlockSpec((rows * SUBLANES, LANES), lambda i, d, f: (i, 0))],
            out_specs=pl.BlockSpec(memory_space=pl.ANY),
            scratch_shapes=[pltpu.VMEM((bm * SUBLANES, LANES), U32), pltpu.SemaphoreType.DMA(())],
        ),
        compiler_params=_params("arbitrary"),
        name="moe_dispatch",
    )(dest, fill, a_slab)


M_USED, M_RUNS, M_EXPERT0 = 0, 1, 2


def _expert_kernel(be_ref, run_ref, meta_ref, x_ref, wg_hbm, wu_hbm, wd_hbm, o_ref,
                   x_s, wg_s, wu_s, wd_s, wg_f, wu_f, wd_f, sem, *, bm, layer):
    b = pl.program_id(0)
    n_used = meta_ref[M_USED]
    r = run_ref[b]

    def weight_copies(run, slot):
        e = meta_ref[M_EXPERT0 + run]
        return [pltpu.make_async_copy(w.at[layer, e], f.at[slot], sem.at[slot, j])
                for j, (w, f) in enumerate(((wg_hbm, wg_f), (wu_hbm, wu_f), (wd_hbm, wd_f)))]

    @pl.when((b < n_used) & ((b == 0) | (be_ref[b] != be_ref[jnp.maximum(b - 1, 0)])))
    def _():
        slot = r % 2

        @pl.when(r == 0)
        def _():
            for cp in weight_copies(0, 0):
                cp.start()

        for cp in weight_copies(r, slot):
            cp.wait()
        wg_s[...] = wg_f[slot].astype(BF16)
        wu_s[...] = wu_f[slot].astype(BF16)
        wd_s[...] = wd_f[slot].astype(BF16)

        @pl.when(r + 1 < meta_ref[M_RUNS])
        def _():
            for cp in weight_copies(r + 1, 1 - slot):
                cp.start()

    @pl.when(b < n_used)
    def _():
        half = SUBLANES * LANES
        for c in range(SUBLANES):
            lo, hi = _unpack_rows(x_ref, bm, c)
            x_s[:, c * LANES:(c + 1) * LANES] = lo.astype(BF16)
            x_s[:, half + c * LANES:half + (c + 1) * LANES] = hi.astype(BF16)
        a = x_s[...]
        hg = jnp.dot(a, wg_s[...], preferred_element_type=F32)
        hu = jnp.dot(a, wu_s[...], preferred_element_type=F32)
        h = hg * jax.nn.sigmoid(hg) * hu
        y = jnp.dot(h.astype(BF16), wd_s[...], preferred_element_type=F32)
        _pack_rows(y, o_ref, bm)

    @pl.when(b >= n_used)
    def _():
        o_ref[...] = jnp.zeros_like(o_ref)


def _experts(xb, w_gate, w_up, w_down, layer, blk_e, blk_run, meta):
    bm = EXPERT_BLOCK
    n_blocks = xb.shape[0] // (bm * SUBLANES)
    _, _, D, De = w_gate.shape
    kern = functools.partial(_expert_kernel, bm=bm, layer=layer)
    hbm = pl.BlockSpec(memory_space=pl.ANY)
    return pl.pallas_call(
        kern,
        out_shape=jax.ShapeDtypeStruct(xb.shape, U32),
        grid_spec=pltpu.PrefetchScalarGridSpec(
            num_scalar_prefetch=3,
            grid=(n_blocks,),
            in_specs=[
                pl.BlockSpec((bm * SUBLANES, LANES), lambda b, be, run, meta: (jnp.minimum(b, meta[M_USED] - 1), 0)),
                hbm, hbm, hbm,
            ],
            out_specs=pl.BlockSpec((bm * SUBLANES, LANES), lambda b, be, run, meta: (b, 0)),
            scratch_shapes=[pltpu.VMEM((bm, D), BF16), pltpu.VMEM((D, De), BF16),
                            pltpu.VMEM((D, De), BF16), pltpu.VMEM((De, D), BF16),
                            pltpu.VMEM((2, D, De), F32), pltpu.VMEM((2, D, De), F32), pltpu.VMEM((2, De, D), F32),
                            pltpu.SemaphoreType.DMA((2, 3))],
        ),
        compiler_params=_params("arbitrary"),
        name="moe_experts",
    )(blk_e, blk_run, meta, xb, w_gate, w_up, w_down)


def _hier_moe(hs, g, w_rg, b_rg, w_re, b_re, w_gate, w_up, w_down, layer):
    D = hs[0].shape[1]
    T = hs[0].shape[0] + hs[1].shape[0]
    pad = LANES - N_EXPERT_GROUPS - N_EXPERTS
    w_r = jnp.concatenate([w_rg, w_re, jnp.zeros((D, pad), F32)], axis=1)
    b_r = jnp.concatenate([b_rg, b_re, jnp.zeros((pad,), F32)])[None, :]
    rec, cnt, a_slab = _route(hs, g, w_r, b_r)

    bm = EXPERT_BLOCK
    e_tok = rec[:, R_E1:R_E2 + 1].astype(jnp.int32)
    rank = rec[:, R_RANK1:R_RANK2 + 1].astype(jnp.int32)
    counts = cnt[0, :N_EXPERTS].astype(jnp.int32)
    pad_counts = (counts + bm - 1) // bm * bm
    pad_end = jnp.cumsum(pad_counts)
    pad_start = pad_end - pad_counts
    start_tok = jnp.sum(jnp.where(e_tok[..., None] == jnp.arange(N_EXPERTS, dtype=jnp.int32), pad_start, 0), axis=-1)
    dest = (start_tok + rank).reshape(-1)
    n_blocks = (T * TOP_K) // bm + N_EXPERTS
    blk_start = jnp.arange(n_blocks, dtype=jnp.int32) * bm
    blk_e = jnp.minimum(jnp.sum(blk_start[:, None] >= pad_end[None, :], axis=1), N_EXPERTS - 1).astype(jnp.int32)
    n_used = (pad_end[-1:] // bm).astype(jnp.int32)
    fill = jnp.concatenate([jnp.stack([pad_start + counts, pad_end], axis=1).reshape(-1), n_used]).astype(jnp.int32)
    blk = jnp.arange(n_blocks, dtype=jnp.int32)
    run_starts = (blk < n_used[0]) & ((blk == 0) | (blk_e != jnp.roll(blk_e, 1)))
    blk_run = (jnp.cumsum(run_starts) - 1).astype(jnp.int32)
    e_ids = jnp.arange(N_EXPERTS, dtype=jnp.int32)
    run_experts = jnp.sort(jnp.where(counts > 0, e_ids, e_ids + N_EXPERTS)) % N_EXPERTS
    meta = jnp.concatenate([n_used, jnp.sum(run_starts, keepdims=True), run_experts]).astype(jnp.int32)

    xb = _dispatch(a_slab, dest, fill, n_blocks, bm)
    yb = _experts(xb, w_gate, w_up, w_down, layer, blk_e, blk_run, meta)
    return (_combine(hs[0], rec, yb, dest, 0), _combine(hs[1], rec, yb, dest, hs[0].shape[0]))


def _combine_kernel(dest_ref, h_ref, r_ref, yb_hbm, o_ref, y_ref, sem, *, rows, tok0):
    i = pl.program_id(0)

    def issue(step, slot):
        base = tok0 + step * rows

        def start(r, c):
            for k in range(TOP_K):
                _tile_copy(yb_hbm, dest_ref[TOP_K * (base + r) + k], y_ref.at[slot, k], r,
                           sem.at[slot]).start(priority=k)
            return c

        lax.fori_loop(0, rows, start, 0, unroll=8)

    def finish(slot):
        for k in range(TOP_K):
            pltpu.make_async_copy(yb_hbm.at[pl.ds(0, rows * SUBLANES), :], y_ref.at[slot, k], sem.at[slot]).wait()
        rec = r_ref[...]
        w1 = rec[:, R_W1:R_W1 + 1]
        w2 = rec[:, R_W2:R_W2 + 1]
        half = SUBLANES * LANES
        for c in range(SUBLANES):
            lo1, hi1 = _unpack_rows(y_ref.at[slot, 0], rows, c)
            lo2, hi2 = _unpack_rows(y_ref.at[slot, 1], rows, c)
            sl = slice(c * LANES, (c + 1) * LANES)
            sh = slice(half + c * LANES, half + (c + 1) * LANES)
            o_ref[:, sl] = h_ref[:, sl] + (w1 * lo1 + w2 * lo2)
            o_ref[:, sh] = h_ref[:, sh] + (w1 * hi1 + w2 * hi2)

    @pl.when(i == 0)
    def _():
        issue(0, 0)

    for slot in range(2):
        @pl.when(i % 2 == slot)
        def _():
            @pl.when(i + 1 < pl.num_programs(0))
            def _():
                issue(i + 1, 1 - slot)

            finish(slot)


def _combine(h, rec, yb, dest, tok0):
    Tg, D = h.shape
    rows = COMBINE_ROWS
    assert tok0 % rows == 0 and Tg % rows == 0
    blk0 = tok0 // rows
    kern = functools.partial(_combine_kernel, rows=rows, tok0=tok0)
    return pl.pallas_call(
        kern,
        out_shape=jax.ShapeDtypeStruct((Tg, D), F32),
        grid_spec=pltpu.PrefetchScalarGridSpec(
            num_scalar_prefetch=1,
            grid=(Tg // rows,),
            in_specs=[
                pl.BlockSpec((rows, D), lambda i, d: (i, 0)),
                pl.BlockSpec((rows, LANES), lambda i, d: (blk0 + i, 0)),
                pl.BlockSpec(memory_space=pl.ANY),
            ],
            out_specs=pl.BlockSpec((rows, D), lambda i, d: (i, 0)),
            scratch_shapes=[pltpu.VMEM((2, TOP_K, rows * SUBLANES, LANES), U32), pltpu.SemaphoreType.DMA((2,))],
        ),
        compiler_params=_params("arbitrary"),
        name="moe_combine",
    )(dest, h, rec, yb)


def _ple_math(h_ref, g_ref, wg_ref, p_ref, wp_ref):
    h = h_ref[...]
    a = _rms(h, g_ref[...]).astype(BF16)
    gate = jax.nn.sigmoid(jnp.dot(a, wg_ref[...], preferred_element_type=F32))
    emb = jnp.dot(p_ref[...].astype(BF16), wp_ref[...], preferred_element_type=F32)
    return h + gate * emb


def _ple_kernel(h_ref, g_ref, wg_ref, p_ref, wp_ref, o_ref):
    o_ref[...] = _ple_math(h_ref, g_ref, wg_ref, p_ref, wp_ref)


def _ple_final_kernel(h_ref, g_ref, wg_ref, p_ref, wp_ref, gf_ref, o_ref):
    o_ref[...] = _rms(_ple_math(h_ref, g_ref, wg_ref, p_ref, wp_ref), gf_ref[...])


def _ple(h, g, w_gate, p, layer, w_proj, g_final=None):
    T, D = h.shape
    P = p.shape[2]
    in_specs = [
        pl.BlockSpec((TM, D), lambda i: (i, 0)),
        _resident((1, D)),
        _resident((D, D)),
        pl.BlockSpec((None, TM, P), lambda i: (layer, i, 0)),
        _resident((P, D)),
    ]
    args = (h, g, w_gate, p, w_proj)
    if g_final is not None:
        in_specs.append(_resident((1, D)))
        args += (g_final,)
    return pl.pallas_call(
        _ple_kernel if g_final is None else _ple_final_kernel,
        out_shape=jax.ShapeDtypeStruct((T, D), F32),
        grid=(T // TM,),
        in_specs=in_specs,
        out_specs=pl.BlockSpec((TM, D), lambda i: (i, 0)),
        compiler_params=_params("parallel"),
        name="ple" if g_final is None else "ple_final",
    )(*args)


def _dft_tables(n, scale, rows=None):
    blk = 64
    assert n % blk == 0
    rows = n if rows is None else rows
    j = jnp.arange(rows, dtype=jnp.int32)

    def cos_sin(k):
        ang = ((j[:, None] * k[None, :]) % n).astype(F32) * (2.0 * np.pi / n)
        return jnp.cos(ang), jnp.sin(ang)

    ca, sa = cos_sin(jnp.arange(n // blk, dtype=jnp.int32) * blk)
    cb, sb = cos_sin(jnp.arange(blk, dtype=jnp.int32))
    ca, sa, cb, sb = ca[:, :, None], sa[:, :, None], cb[:, None, :], sb[:, None, :]
    c = (ca * cb - sa * sb) * scale
    s = (sa * cb + ca * sb) * scale
    return c.reshape(rows, n).astype(BF16), s.reshape(rows, n).astype(BF16)


def _norm_nyq_kernel(x_ref, g_ref, a_ref, nyq_ref):
    a = _rms(x_ref[...], g_ref[...]).astype(BF16)
    a_ref[...] = a
    row = lax.broadcasted_iota(jnp.int32, (a.shape[0], 1), 0)
    sign = jnp.where(row % 2 == 0, 1.0, -1.0)
    part = jnp.sum(a.astype(F32) * sign, axis=0, keepdims=True)

    @pl.when(pl.program_id(1) == 0)
    def _():
        nyq_ref[...] = jnp.zeros_like(nyq_ref)

    nyq_ref[...] += jnp.broadcast_to(part, nyq_ref.shape[1:])[None]


def _norm_nyq(h, g, batch, seq):
    T, D = h.shape
    nt = seq // TM
    return pl.pallas_call(
        _norm_nyq_kernel,
        out_shape=(jax.ShapeDtypeStruct((T, D), BF16), jax.ShapeDtypeStruct((batch, BF16_ROWS, D), F32)),
        grid=(batch, nt),
        in_specs=[pl.BlockSpec((TM, D), lambda b, i: (b * nt + i, 0)), _resident((1, D))],
        out_specs=(pl.BlockSpec((TM, D), lambda b, i: (b * nt + i, 0)),
                   pl.BlockSpec((1, BF16_ROWS, D), lambda b, i: (b, 0, 0))),
        compiler_params=_params("parallel", "arbitrary"),
        name="fourier_norm",
    )(h, g)


def _half_dft_kernel(c_ref, s_ref, a_ref, p_ref, q_ref, accp_ref, accq_ref):
    k = pl.program_id(3)

    @pl.when(k == 0)
    def _():
        accp_ref[...] = jnp.zeros_like(accp_ref)
        accq_ref[...] = jnp.zeros_like(accq_ref)

    a = a_ref[...]
    accp_ref[...] += jnp.dot(c_ref[...], a, preferred_element_type=F32)
    accq_ref[...] += jnp.dot(s_ref[...], a, preferred_element_type=F32)

    @pl.when(k == pl.num_programs(3) - 1)
    def _():
        p_ref[...] = accp_ref[...].astype(BF16)
        q_ref[...] = accq_ref[...].astype(BF16)


def _half_dft(a, c, s, batch, seq):
    D = a.shape[1]
    half = seq // 2
    tm, tn, tk = min(DFT_TM, half), DFT_TN, min(DFT_TK, seq)
    out = jax.ShapeDtypeStruct((batch * half, D), BF16)
    ospec = pl.BlockSpec((tm, tn), lambda b, i, j, k: (b * (half // tm) + i, j))
    return pl.pallas_call(
        _half_dft_kernel,
        out_shape=(out, out),
        grid=(batch, half // tm, D // tn, seq // tk),
        in_specs=[
            pl.BlockSpec((tm, tk), lambda b, i, j, k: (i, k)),
            pl.BlockSpec((tm, tk), lambda b, i, j, k: (i, k)),
            pl.BlockSpec((tk, tn), lambda b, i, j, k: (b * (seq // tk) + k, j)),
        ],
        out_specs=(ospec, ospec),
        scratch_shapes=[pltpu.VMEM((tm, tn), F32), pltpu.VMEM((tm, tn), F32)],
        compiler_params=_params("parallel", "parallel", "parallel", "arbitrary"),
        name="fourier_half_dft",
    )(c, s, a)


def _chan_dft(p, q, c_ref, s_ref, sign, gd):
    outs = []
    for grp in range(p.shape[1] // gd):
        sl = slice(grp * gd, (grp + 1) * gd)
        pc = jnp.dot(p[:, sl], c_ref[...], preferred_element_type=F32)
        qs = jnp.dot(q[:, sl], s_ref[...], preferred_element_type=F32)
        outs.append(pc + sign * qs)
    return jnp.concatenate(outs, axis=1)


def _fourier_out_kernel(pa_ref, qa_ref, pb_ref, qb_ref, nyq_ref, c_ref, s_ref, rev_ref, w_ref, h_ref, o_ref,
                        *, n_lo, gd):
    i = pl.program_id(1)

    @pl.when(i < n_lo)
    def _():
        f = _chan_dft(pa_ref[...], qa_ref[...], c_ref, s_ref, -1.0, gd)
        o_ref[...] = h_ref[...] + jnp.dot(f.astype(BF16), w_ref[...], preferred_element_type=F32)

    @pl.when(i >= n_lo)
    def _():
        m = _chan_dft(pa_ref[...], qa_ref[...], c_ref, s_ref, 1.0, gd).astype(BF16)
        rev = jnp.dot(rev_ref[...], m, preferred_element_type=F32)
        mb = _chan_dft(pb_ref[...], qb_ref[...], c_ref, s_ref, 1.0, gd)[0:1, :]
        nyq = nyq_ref[0].astype(BF16)
        mn = _chan_dft(nyq, jnp.zeros_like(nyq), c_ref, s_ref, 1.0, gd)[0:1, :]
        first = jnp.where(i == n_lo, mn, mb)
        row = lax.broadcasted_iota(jnp.int32, rev.shape, 0)
        f = jnp.where(row == 0, first, rev)
        o_ref[...] = h_ref[...] + jnp.dot(f.astype(BF16), w_ref[...], preferred_element_type=F32)


def _fourier_out(p, q, nyq, cc, sc, w, h, batch, seq):
    D = h.shape[1]
    gd = cc.shape[0]
    tm = TM
    half = seq // 2
    assert half % tm == 0
    n_lo = half // tm
    nt = seq // tm
    r = jnp.arange(tm)
    rev = ((r[:, None] + r[None, :]) == tm).astype(BF16)

    def a_map(b, i):
        return (b * n_lo + jnp.where(i < n_lo, i, nt - 1 - i), 0)

    def b_map(b, i):
        t = jnp.where(i < n_lo, i, jnp.minimum(nt - i, n_lo - 1))
        return ((b * n_lo + t) * (tm // BF16_ROWS), 0)

    kern = functools.partial(_fourier_out_kernel, n_lo=n_lo, gd=gd)
    return pl.pallas_call(
        kern,
        out_shape=jax.ShapeDtypeStruct(h.shape, F32),
        grid=(batch, nt),
        in_specs=[
            pl.BlockSpec((tm, D), a_map),
            pl.BlockSpec((tm, D), a_map),
            pl.BlockSpec((BF16_ROWS, D), b_map),
            pl.BlockSpec((BF16_ROWS, D), b_map),
            pl.BlockSpec((1, BF16_ROWS, D), lambda b, i: (b, 0, 0)),
            _resident((gd, gd)),
            _resident((gd, gd)),
            _resident((tm, tm)),
            _resident((D, D)),
            pl.BlockSpec((tm, D), lambda b, i: (b * nt + i, 0)),
        ],
        out_specs=pl.BlockSpec((tm, D), lambda b, i: (b * nt + i, 0)),
        compiler_params=_params("parallel", "parallel"),
        name="fourier_out",
    )(p, q, p, q, nyq, cc, sc, rev, w, h)


def _fourier_mix(h, g, w_out, cc, sc, batch, seq):
    scale = 1.0 / float(np.sqrt(seq))
    cs, sn = _dft_tables(seq, scale, rows=seq // 2)
    a, nyq = _norm_nyq(h, g, batch, seq)
    p, q = _half_dft(a, cs, sn, batch, seq)
    return _fourier_out(p, q, nyq * scale, cc, sc, w_out, h, batch, seq)


def kernel(x_prompt, x_sample, p_prompt, p_sample, g_mix, w_qkv, g_q, g_k, w_attn_out, w_fourier_out, g_ffn, w_route_group, b_route_group, w_route_expert, b_route_expert, w_exp_gate, w_exp_up, w_exp_down, g_ple, w_ple_gate, w_ple_proj, g_final):
    depth = g_mix.shape[0]
    bp, sp, D = x_prompt.shape
    bs, ss, _ = x_sample.shape
    tp, ts = bp * sp, bs * ss
    gd = D // N_FOURIER_GROUPS
    geom = ((bs, ss), (bp, sp))
    hs = (x_sample.reshape(ts, D), x_prompt.reshape(tp, D))
    ps = (p_sample, p_prompt)

    n_mixers = 2
    for i in range(depth):
        jm = i // n_mixers
        g_i = g_mix[i][None, :]
        if i % n_mixers == 0:
            w, gq2, gk2 = _qkv_weights(w_qkv[jm], g_q[jm], g_k[jm])
            cos, sin = _rope_tables(max(sp, ss))
            w_o = w_attn_out[jm].astype(BF16)
            mixed = []
            for h, (batch, seq) in zip(hs, geom):
                qkv = _qkv_proj(h, g_i, w, gq2, gk2, cos, sin, seq)
                mixed.append(_matmul_residual(_attention(qkv, 0, batch, seq), w_o, h))
        else:
            cc, sc = _dft_tables(gd, 1.0 / float(np.sqrt(gd)))
            w_o = w_fourier_out[jm].astype(BF16)
            mixed = [_fourier_mix(h, g_i, w_o, cc, sc, batch, seq) for h, (batch, seq) in zip(hs, geom)]
        hs = _hier_moe(tuple(mixed), g_ffn[i][None, :], w_route_group[i], b_route_group[i], w_route_expert[i],
                       b_route_expert[i], w_exp_gate, w_exp_up, w_exp_down, i)
        w_pg = w_ple_gate[i].astype(BF16)
        w_pp = w_ple_proj[i].astype(BF16)
        g_f = g_final[None, :] if i == depth - 1 else None
        hs = tuple(_ple(h, g_ple[i][None, :], w_pg, p.reshape(depth, h.shape[0], -1), i, w_pp, g_f)
                   for h, p in zip(hs, ps))
    y_sample, y_prompt = hs
    return (y_prompt.reshape(bp, sp, D), y_sample.reshape(bs, ss, D))
```

```python
import functools

import jax
import jax.numpy as jnp
import numpy as np
from jax import lax
from jax.experimental import pallas as pl
from jax.experimental.pallas import tpu as pltpu

F32 = jnp.float32
BF16 = jnp.bfloat16
U32 = jnp.uint32

HEAD_DIM = 128
N_HEADS = 16
N_KV_HEADS = 8
Q_PER_KV = N_HEADS // N_KV_HEADS
ROPE_AXIS_DIM = HEAD_DIM // 2
ROPE_THETA = 10000.0
GRID_W = 64
N_FOURIER_GROUPS = 8
N_EXPERT_GROUPS = 4
EXPERTS_PER_GROUP = 8
N_EXPERTS = N_EXPERT_GROUPS * EXPERTS_PER_GROUP
TOP_K = 2
NORM_EPS = 1e-6

LANES = 128
SUBLANES = 8
BF16_ROWS = 16
MXU_DIM = 256
VMEM_LIMIT = 56 * 1024 * 1024

TM = 512
TQ = 1024
TKV = 1024
EXPERT_BLOCK = 256
MOVE_ROWS = 1024
COMBINE_ROWS = 256
DFT_TM, DFT_TN, DFT_TK = 1024, 1024, 2048


def _params(*sem):
    return pltpu.CompilerParams(dimension_semantics=sem, vmem_limit_bytes=VMEM_LIMIT)


def _resident(shape):
    return pl.BlockSpec(shape, lambda *_: (0,) * len(shape), pipeline_mode=pl.Buffered(1))


def _rms(x, g):
    ms = jnp.mean(x * x, axis=-1, keepdims=True)
    return x * lax.rsqrt(ms + NORM_EPS) * g


def _pack_rows(x, ref, rows):
    half = SUBLANES * LANES
    for c in range(SUBLANES):
        lo = x[:, c * LANES:(c + 1) * LANES].astype(BF16).astype(F32)
        hi = x[:, half + c * LANES:half + (c + 1) * LANES].astype(BF16).astype(F32)
        word = (pltpu.bitcast(lo, U32) >> 16) | (pltpu.bitcast(hi, U32) & jnp.uint32(0xFFFF0000))
        ref[pl.ds(c, rows, stride=SUBLANES), :] = word


def _unpack_rows(ref, rows, c):
    word = ref[pl.ds(c, rows, stride=SUBLANES), :]
    lo = pltpu.bitcast(word << 16, F32)
    hi = pltpu.bitcast(word & jnp.uint32(0xFFFF0000), F32)
    return lo, hi


def _qkv_kernel(x_ref, g_ref, w_ref, gq_ref, gk_ref, cos_ref, sin_ref, o_ref, *, d_q, d_qk):
    a = _rms(x_ref[...], g_ref[...]).astype(BF16)
    cos = cos_ref[...]
    sin = sin_ref[...]
    q_scale = float(np.log2(np.e) / np.sqrt(HEAD_DIM))
    tabs = {
        "q": (gq_ref[0:1, :] * cos * q_scale, gq_ref[1:2, :] * sin * q_scale),
        "k": (gk_ref[0:1, :] * cos, gk_ref[1:2, :] * sin),
    }
    n_out = o_ref.shape[1]
    for c0 in range(0, n_out, MXU_DIM):
        acc = jnp.dot(a, w_ref[:, c0:c0 + MXU_DIM], preferred_element_type=F32)
        if c0 >= d_qk:
            o_ref[:, c0:c0 + MXU_DIM] = acc.astype(BF16)
            continue
        t1, t2 = tabs["q" if c0 < d_q else "k"]
        for hh in range(MXU_DIM // HEAD_DIM):
            y = acc[:, hh * HEAD_DIM:(hh + 1) * HEAD_DIM]
            r = lax.rsqrt(jnp.mean(y * y, axis=-1, keepdims=True) + NORM_EPS)
            rot = (y * t1 + pltpu.roll(y, HEAD_DIM // 2, 1) * t2) * r
            o_ref[:, c0 + hh * HEAD_DIM:c0 + (hh + 1) * HEAD_DIM] = rot.astype(BF16)


def _permute_heads(x):
    lead = x.shape[:-1]
    x = x.reshape(lead + (-1, 2, 2, HEAD_DIM // 4))
    return jnp.swapaxes(x, -3, -2).reshape(lead + (-1,))


def _rope_tables(n_pos):
    pos = jnp.arange(n_pos, dtype=jnp.int32)
    row = (pos // GRID_W).astype(F32)
    col = (pos % GRID_W).astype(F32)
    inv_freq = ROPE_THETA ** (-jnp.arange(0, ROPE_AXIS_DIM, 2, dtype=F32) / ROPE_AXIS_DIM)
    ang_r = row[:, None] * inv_freq[None, :]
    ang_c = col[:, None] * inv_freq[None, :]
    cos = jnp.concatenate([jnp.cos(ang_r), jnp.cos(ang_c), jnp.cos(ang_r), jnp.cos(ang_c)], axis=-1)
    sin = jnp.concatenate([-jnp.sin(ang_r), -jnp.sin(ang_c), jnp.sin(ang_r), jnp.sin(ang_c)], axis=-1)
    return cos, sin


def _qkv_weights(w_qkv, g_q, g_k):
    d_qk = (N_HEADS + N_KV_HEADS) * HEAD_DIM
    w = jnp.concatenate([_permute_heads(w_qkv[:, :d_qk]).astype(BF16), w_qkv[:, d_qk:].astype(BF16)], axis=1)
    gq = _permute_heads(g_q)
    gk = _permute_heads(g_k)
    gq2 = jnp.stack([gq, jnp.roll(gq, HEAD_DIM // 2)])
    gk2 = jnp.stack([gk, jnp.roll(gk, HEAD_DIM // 2)])
    return w, gq2, gk2


def _qkv_proj(h, g, w, gq2, gk2, cos, sin, seq):
    T, D = h.shape
    N = w.shape[1]
    assert seq % TM == 0 and T % seq == 0
    d_q = N_HEADS * HEAD_DIM
    d_qk = d_q + N_KV_HEADS * HEAD_DIM

    def pos_map(i):
        return (i % (seq // TM), 0)

    kern = functools.partial(_qkv_kernel, d_q=d_q, d_qk=d_qk)
    return pl.pallas_call(
        kern,
        out_shape=jax.ShapeDtypeStruct((T, N), BF16),
        grid=(T // TM,),
        in_specs=[
            pl.BlockSpec((TM, D), lambda i: (i, 0)),
            _resident((1, D)),
            _resident((D, N)),
            _resident((2, HEAD_DIM)),
            _resident((2, HEAD_DIM)),
            pl.BlockSpec((TM, HEAD_DIM), pos_map),
            pl.BlockSpec((TM, HEAD_DIM), pos_map),
        ],
        out_specs=pl.BlockSpec((TM, N), lambda i: (i, 0)),
        compiler_params=_params("parallel"),
        name="qkv_proj",
    )(h, g, w, gq2, gk2, cos, sin)


def _attn_kernel(q_ref, k_ref, v_ref, o_ref, *, tq, tk, seq):
    q = q_ref[...]
    q2 = jnp.concatenate([q[:, :HEAD_DIM], q[:, HEAD_DIM:]], axis=0)
    m = jnp.full((2 * tq, 1), -jnp.inf, F32)
    l = jnp.zeros((2 * tq, 1), F32)
    acc = jnp.zeros((2 * tq, HEAD_DIM), F32)
    for c in range(seq // tk):
        k = k_ref[c * tk:(c + 1) * tk, :]
        v = v_ref[c * tk:(c + 1) * tk, :]
        s = lax.dot_general(q2, k, (((1,), (1,)), ((), ())), preferred_element_type=F32)
        m_new = jnp.maximum(m, jnp.max(s, axis=-1, keepdims=True))
        alpha = jnp.exp2(m - m_new)
        p = jnp.exp2(s - m_new)
        l = alpha * l + jnp.sum(p, axis=-1, keepdims=True)
        acc = alpha * acc + jnp.dot(p.astype(BF16), v, preferred_element_type=F32)
        m = m_new
    o = acc / l
    o_ref[:, :HEAD_DIM] = o[:tq].astype(BF16)
    o_ref[:, HEAD_DIM:] = o[tq:].astype(BF16)


def _attention(qkv, row0, batch, seq):
    assert row0 % seq == 0 and seq % TQ == 0 and seq % TKV == 0
    qw = Q_PER_KV * HEAD_DIM
    kern = functools.partial(_attn_kernel, tq=TQ, tk=TKV, seq=seq)
    q_blk0 = row0 // TQ
    s_blk0 = row0 // seq
    k_col0 = N_HEADS
    v_col0 = N_HEADS + N_KV_HEADS
    return pl.pallas_call(
        kern,
        out_shape=jax.ShapeDtypeStruct((batch * seq, N_HEADS * HEAD_DIM), BF16),
        grid=(batch, N_KV_HEADS, seq // TQ),
        in_specs=[
            pl.BlockSpec((TQ, qw), lambda b, h, i: (q_blk0 + b * (seq // TQ) + i, h)),
            pl.BlockSpec((seq, HEAD_DIM), lambda b, h, i: (s_blk0 + b, k_col0 + h)),
            pl.BlockSpec((seq, HEAD_DIM), lambda b, h, i: (s_blk0 + b, v_col0 + h)),
        ],
        out_specs=pl.BlockSpec((TQ, qw), lambda b, h, i: (b * (seq // TQ) + i, h)),
        compiler_params=_params("parallel", "parallel", "arbitrary"),
        name="attention",
    )(qkv, qkv, qkv)


def _mm_res_kernel(x_ref, w_ref, r_ref, o_ref):
    o_ref[...] = r_ref[...] + jnp.dot(x_ref[...], w_ref[...], preferred_element_type=F32)


def _matmul_residual(x, w, res):
    T, K = x.shape
    N = w.shape[1]
    return pl.pallas_call(
        _mm_res_kernel,
        out_shape=jax.ShapeDtypeStruct((T, N), F32),
        grid=(T // TM,),
        in_specs=[
            pl.BlockSpec((TM, K), lambda i: (i, 0)),
            _resident((K, N)),
            pl.BlockSpec((TM, N), lambda i: (i, 0)),
        ],
        out_specs=pl.BlockSpec((TM, N), lambda i: (i, 0)),
        compiler_params=_params("parallel"),
        name="matmul_residual",
    )(x, w, res)


R_E1, R_E2, R_W1, R_W2, R_RANK1, R_RANK2 = range(6)
ROUTE_LOGIT_E0 = N_EXPERT_GROUPS


def _route_kernel(x0_ref, x1_ref, g_ref, w_ref, b_ref, tri_ref, r_ref, cnt_ref, a_ref, carry_ref, *, n0):
    i = pl.program_id(0)

    @pl.when(i == 0)
    def _():
        carry_ref[...] = jnp.zeros_like(carry_ref)

    a = _rms(jnp.where(i < n0, x0_ref[...], x1_ref[...]), g_ref[...])
    _pack_rows(a, a_ref, a.shape[0])
    a_hi = a.astype(BF16)
    a_lo = (a - a_hi.astype(F32)).astype(BF16)
    hi_both = jnp.dot(a_hi, w_ref[...], preferred_element_type=F32)
    logits = (hi_both[:, :LANES]
              + (jnp.dot(a_lo, w_ref[:, :LANES], preferred_element_type=F32) + hi_both[:, LANES:])) + b_ref[...]
    lane = lax.broadcasted_iota(jnp.int32, logits.shape, 1)
    lane_f = lane.astype(F32)
    neg = -jnp.inf
    far = float(LANES)

    def first_max(vals):
        top = jnp.max(vals, axis=-1, keepdims=True)
        idx = jnp.min(jnp.where(vals == top, lane_f, far), axis=-1, keepdims=True)
        return top, idx.astype(jnp.int32)

    gmask = lane < N_EXPERT_GROUPS
    gtop, g_sel = first_max(jnp.where(gmask, logits, neg))
    pg = 1.0 / jnp.sum(jnp.where(gmask, jnp.exp(logits - gtop), 0.0), axis=-1, keepdims=True)
    lo = ROUTE_LOGIT_E0 + g_sel * EXPERTS_PER_GROUP
    le = jnp.where((lane >= lo) & (lane < lo + EXPERTS_PER_GROUP), logits, neg)
    t1, i1 = first_max(le)
    t2, i2 = first_max(jnp.where(lane == i1, neg, le))
    r21 = jnp.exp(t2 - t1)
    w1 = pg / (1.0 + r21)
    w2 = pg * r21 / (1.0 + r21)
    e1 = i1 - ROUTE_LOGIT_E0
    e2 = i2 - ROUTE_LOGIT_E0

    hit1 = lane == e1
    hit2 = lane == e2
    onehot = (hit1 | hit2).astype(F32)
    before = jnp.dot(tri_ref[...], onehot.astype(BF16), preferred_element_type=F32) + carry_ref[0:1, :]
    rank1 = jnp.sum(jnp.where(hit1, before, 0.0), axis=-1, keepdims=True)
    rank2 = jnp.sum(jnp.where(hit2, before, 0.0), axis=-1, keepdims=True)
    carry_ref[...] = carry_ref[...] + jnp.sum(onehot, axis=0, keepdims=True)
    cnt_ref[...] = carry_ref[...]

    rec = jnp.zeros(logits.shape, F32)
    for pos, val in ((R_E1, e1.astype(F32)), (R_E2, e2.astype(F32)), (R_W1, w1), (R_W2, w2),
                     (R_RANK1, rank1), (R_RANK2, rank2)):
        rec = jnp.where(lane == pos, val, rec)
    r_ref[...] = rec


def _route(hs, g, w_r, b_r):
    h0, h1 = hs
    D = h0.shape[1]
    T = h0.shape[0] + h1.shape[0]
    assert D == 2 * SUBLANES * LANES and h0.shape[0] % TM == 0 and h1.shape[0] % TM == 0
    n0 = h0.shape[0] // TM
    tri = (jnp.arange(TM)[:, None] > jnp.arange(TM)[None, :]).astype(BF16)
    w_hi = w_r.astype(BF16)
    w_lo = (w_r - w_hi.astype(F32)).astype(BF16)
    kern = functools.partial(_route_kernel, n0=n0)
    return pl.pallas_call(
        kern,
        out_shape=(jax.ShapeDtypeStruct((T, LANES), F32), jax.ShapeDtypeStruct((SUBLANES, LANES), F32),
                   jax.ShapeDtypeStruct((T * SUBLANES, LANES), U32)),
        grid=(T // TM,),
        in_specs=[
            pl.BlockSpec((TM, D), lambda i: (jnp.minimum(i, n0 - 1), 0)),
            pl.BlockSpec((TM, D), lambda i: (jnp.maximum(i - n0, 0), 0)),
            _resident((1, D)),
            _resident((D, 2 * LANES)),
            _resident((1, LANES)),
            _resident((TM, TM)),
        ],
        out_specs=(pl.BlockSpec((TM, LANES), lambda i: (i, 0)),
                   pl.BlockSpec((SUBLANES, LANES), lambda i: (0, 0)),
                   pl.BlockSpec((TM * SUBLANES, LANES), lambda i: (i, 0))),
        scratch_shapes=[pltpu.VMEM((SUBLANES, LANES), F32)],
        compiler_params=_params("arbitrary"),
        name="moe_route",
    )(h0, h1, g, jnp.concatenate([w_hi, w_lo], axis=1), b_r, tri)


def _tile_copy(src, src_row, dst, dst_row, sem):
    return pltpu.make_async_copy(src.at[pl.ds(pl.multiple_of(src_row * SUBLANES, SUBLANES), SUBLANES), :],
                                 dst.at[pl.ds(pl.multiple_of(dst_row * SUBLANES, SUBLANES), SUBLANES), :], sem)


def _dispatch_kernel(dest_ref, fill_ref, a_ref, xb_hbm, zero_ref, sem, *, rows, n_blocks):
    i = pl.program_id(0)
    base = i * rows

    def start(r, c):
        for k in range(TOP_K):
            _tile_copy(a_ref, r, xb_hbm, dest_ref[TOP_K * (base + r) + k], sem).start(priority=k)
        return c

    lax.fori_loop(0, rows, start, 0, unroll=8)
    for k in range(TOP_K):
        pltpu.make_async_copy(a_ref, xb_hbm.at[pl.ds(0, rows * SUBLANES), :], sem).wait()

    @pl.when(i == pl.num_programs(0) - 1)
    def _():
        zero_ref[...] = jnp.zeros_like(zero_ref)

        def per_expert(e, c):
            lo = fill_ref[2 * e]
            hi = fill_ref[2 * e + 1]

            def zstart(r, cc):
                _tile_copy(zero_ref, 0, xb_hbm, r, sem).start()
                return cc

            def zwait(r, cc):
                _tile_copy(zero_ref, 0, xb_hbm, r, sem).wait()
                return cc

            lax.fori_loop(lo, hi, zstart, 0)
            lax.fori_loop(lo, hi, zwait, 0)
            return c

        lax.fori_loop(0, N_EXPERTS, per_expert, 0)

        blk = zero_ref.shape[0]

        def block_copy(b):
            return pltpu.make_async_copy(zero_ref, xb_hbm.at[pl.ds(pl.multiple_of(b * blk, blk), blk), :], sem)

        def bstart(b, c):
            block_copy(b).start()
            return c

        def bwait(b, c):
            block_copy(b).wait()
            return c

        first_unused = fill_ref[2 * N_EXPERTS]
        lax.fori_loop(first_unused, n_blocks, bstart, 0)
        lax.fori_loop(first_unused, n_blocks, bwait, 0)


def _dispatch(a_slab, dest, fill, n_blocks, bm):
    T = a_slab.shape[0] // SUBLANES
    rows = MOVE_ROWS
    kern = functools.partial(_dispatch_kernel, rows=rows, n_blocks=n_blocks)
    return pl.pallas_call(
        kern,
        out_shape=jax.ShapeDtypeStruct((n_blocks * bm * SUBLANES, LANES), U32),
        grid_spec=pltpu.PrefetchScalarGridSpec(
            num_scalar_prefetch=2,
            grid=(T // rows,),
            in_specs=[pl.BlockSpec((rows * SUBLANES, LANES), lambda i, d, f: (i, 0))],
            out_specs=pl.BlockSpec(memory_space=pl.ANY),
            scratch_shapes=[pltpu.VMEM((bm * SUBLANES, LANES), U32), pltpu.SemaphoreType.DMA(())],
        ),
        compiler_params=_params("arbitrary"),
        name="moe_dispatch",
    )(dest, fill, a_slab)


M_USED, M_RUNS, M_EXPERT0 = 0, 1, 2


def _expert_kernel(be_ref, run_ref, meta_ref, x_ref, wg_hbm, wu_hbm, wd_hbm, o_ref,
                   x_s, wg_s, wu_s, wd_s, wg_f, wu_f, wd_f, sem, *, bm, layer):
    b = pl.program_id(0)
    n_used = meta_ref[M_USED]
    r = run_ref[b]

    def weight_copies(run, slot):
        e = meta_ref[M_EXPERT0 + run]
        return [pltpu.make_async_copy(w.at[layer, e], f.at[slot], sem.at[slot, j])
                for j, (w, f) in enumerate(((wg_hbm, wg_f), (wu_hbm, wu_f), (wd_hbm, wd_f)))]

    @pl.when((b < n_used) & ((b == 0) | (be_ref[b] != be_ref[jnp.maximum(b - 1, 0)])))
    def _():
        slot = r % 2

        @pl.when(r == 0)
        def _():
            for cp in weight_copies(0, 0):
                cp.start()

        for cp in weight_copies(r, slot):
            cp.wait()
        wg_s[...] = wg_f[slot].astype(BF16)
        wu_s[...] = wu_f[slot].astype(BF16)
        wd_s[...] = wd_f[slot].astype(BF16)

        @pl.when(r + 1 < meta_ref[M_RUNS])
        def _():
            for cp in weight_copies(r + 1, 1 - slot):
                cp.start()

    @pl.when(b < n_used)
    def _():
        half = SUBLANES * LANES
        for c in range(SUBLANES):
            lo, hi = _unpack_rows(x_ref, bm, c)
            x_s[:, c * LANES:(c + 1) * LANES] = lo.astype(BF16)
            x_s[:, half + c * LANES:half + (c + 1) * LANES] = hi.astype(BF16)
        a = x_s[...]
        hg = jnp.dot(a, wg_s[...], preferred_element_type=F32)
        hu = jnp.dot(a, wu_s[...], preferred_element_type=F32)
        h = hg * jax.nn.sigmoid(hg) * hu
        y = jnp.dot(h.astype(BF16), wd_s[...], preferred_element_type=F32)
        _pack_rows(y, o_ref, bm)

    @pl.when(b >= n_used)
    def _():
        o_ref[...] = jnp.zeros_like(o_ref)


def _experts(xb, w_gate, w_up, w_down, layer, blk_e, blk_run, meta):
    bm = EXPERT_BLOCK
    n_blocks = xb.shape[0] // (bm * SUBLANES)
    _, _, D, De = w_gate.shape
    kern = functools.partial(_expert_kernel, bm=bm, layer=layer)
    hbm = pl.BlockSpec(memory_space=pl.ANY)
    return pl.pallas_call(
        kern,
        out_shape=jax.ShapeDtypeStruct(xb.shape, U32),
        grid_spec=pltpu.PrefetchScalarGridSpec(
            num_scalar_prefetch=3,
            grid=(n_blocks,),
            in_specs=[
                pl.BlockSpec((bm * SUBLANES, LANES), lambda b, be, run, meta: (jnp.minimum(b, meta[M_USED] - 1), 0)),
                hbm, hbm, hbm,
            ],
            out_specs=pl.BlockSpec((bm * SUBLANES, LANES), lambda b, be, run, meta: (b, 0)),
            scratch_shapes=[pltpu.VMEM((bm, D), BF16), pltpu.VMEM((D, De), BF16),
                            pltpu.VMEM((D, De), BF16), pltpu.VMEM((De, D), BF16),
                            pltpu.VMEM((2, D, De), F32), pltpu.VMEM((2, D, De), F32), pltpu.VMEM((2, De, D), F32),
                            pltpu.SemaphoreType.DMA((2, 3))],
        ),
        compiler_params=_params("arbitrary"),
        name="moe_experts",
    )(blk_e, blk_run, meta, xb, w_gate, w_up, w_down)


def _hier_moe(hs, g, w_rg, b_rg, w_re, b_re, w_gate, w_up, w_down, layer):
    D = hs[0].shape[1]
    T = hs[0].shape[0] + hs[1].shape[0]
    pad = LANES - N_EXPERT_GROUPS - N_EXPERTS
    w_r = jnp.concatenate([w_rg, w_re, jnp.zeros((D, pad), F32)], axis=1)
    b_r = jnp.concatenate([b_rg, b_re, jnp.zeros((pad,), F32)])[None, :]
    rec, cnt, a_slab = _route(hs, g, w_r, b_r)

    bm = EXPERT_BLOCK
    e_tok = rec[:, R_E1:R_E2 + 1].astype(jnp.int32)
    rank = rec[:, R_RANK1:R_RANK2 + 1].astype(jnp.int32)
    counts = cnt[0, :N_EXPERTS].astype(jnp.int32)
    pad_counts = (counts + bm - 1) // bm * bm
    pad_end = jnp.cumsum(pad_counts)
    pad_start = pad_end - pad_counts
    start_tok = jnp.sum(jnp.where(e_tok[..., None] == jnp.arange(N_EXPERTS, dtype=jnp.int32), pad_start, 0), axis=-1)
    dest = (start_tok + rank).reshape(-1)
    n_blocks = (T * TOP_K) // bm + N_EXPERTS
    blk_start = jnp.arange(n_blocks, dtype=jnp.int32) * bm
    blk_e = jnp.minimum(jnp.sum(blk_start[:, None] >= pad_end[None, :], axis=1), N_EXPERTS - 1).astype(jnp.int32)
    n_used = (pad_end[-1:] // bm).astype(jnp.int32)
    fill = jnp.concatenate([jnp.stack([pad_start + counts, pad_end], axis=1).reshape(-1), n_used]).astype(jnp.int32)
    blk = jnp.arange(n_blocks, dtype=jnp.int32)
    run_starts = (blk < n_used[0]) & ((blk == 0) | (blk_e != jnp.roll(blk_e, 1)))
    blk_run = (jnp.cumsum(run_starts) - 1).astype(jnp.int32)
    e_ids = jnp.arange(N_EXPERTS, dtype=jnp.int32)
    run_experts = jnp.sort(jnp.where(counts > 0, e_ids, e_ids + N_EXPERTS)) % N_EXPERTS
    meta = jnp.concatenate([n_used, jnp.sum(run_starts, keepdims=True), run_experts]).astype(jnp.int32)

    xb = _dispatch(a_slab, dest, fill, n_blocks, bm)
    yb = _experts(xb, w_gate, w_up, w_down, layer, blk_e, blk_run, meta)
    return (_combine(hs[0], rec, yb, dest, 0), _combine(hs[1], rec, yb, dest, hs[0].shape[0]))


def _combine_kernel(dest_ref, h_ref, r_ref, yb_hbm, o_ref, y_ref, sem, *, rows, tok0):
    i = pl.program_id(0)

    def issue(step, slot):
        base = tok0 + step * rows

        def start(r, c):
            for k in range(TOP_K):
                _tile_copy(yb_hbm, dest_ref[TOP_K * (base + r) + k], y_ref.at[slot, k], r,
                           sem.at[slot]).start(priority=k)
            return c

        lax.fori_loop(0, rows, start, 0, unroll=8)

    def finish(slot):
        for k in range(TOP_K):
            pltpu.make_async_copy(yb_hbm.at[pl.ds(0, rows * SUBLANES), :], y_ref.at[slot, k], sem.at[slot]).wait()
        rec = r_ref[...]
        w1 = rec[:, R_W1:R_W1 + 1]
        w2 = rec[:, R_W2:R_W2 + 1]
        half = SUBLANES * LANES
        for c in range(SUBLANES):
            lo1, hi1 = _unpack_rows(y_ref.at[slot, 0], rows, c)
            lo2, hi2 = _unpack_rows(y_ref.at[slot, 1], rows, c)
            sl = slice(c * LANES, (c + 1) * LANES)
            sh = slice(half + c * LANES, half + (c + 1) * LANES)
            o_ref[:, sl] = h_ref[:, sl] + (w1 * lo1 + w2 * lo2)
            o_ref[:, sh] = h_ref[:, sh] + (w1 * hi1 + w2 * hi2)

    @pl.when(i == 0)
    def _():
        issue(0, 0)

    for slot in range(2):
        @pl.when(i % 2 == slot)
        def _():
            @pl.when(i + 1 < pl.num_programs(0))
            def _():
                issue(i + 1, 1 - slot)

            finish(slot)


def _combine(h, rec, yb, dest, tok0):
    Tg, D = h.shape
    rows = COMBINE_ROWS
    assert tok0 % rows == 0 and Tg % rows == 0
    blk0 = tok0 // rows
    kern = functools.partial(_combine_kernel, rows=rows, tok0=tok0)
    return pl.pallas_call(
        kern,
        out_shape=jax.ShapeDtypeStruct((Tg, D), F32),
        grid_spec=pltpu.PrefetchScalarGridSpec(
            num_scalar_prefetch=1,
            grid=(Tg // rows,),
            in_specs=[
                pl.BlockSpec((rows, D), lambda i, d: (i, 0)),
                pl.BlockSpec((rows, LANES), lambda i, d: (blk0 + i, 0)),
                pl.BlockSpec(memory_space=pl.ANY),
            ],
            out_specs=pl.BlockSpec((rows, D), lambda i, d: (i, 0)),
            scratch_shapes=[pltpu.VMEM((2, TOP_K, rows * SUBLANES, LANES), U32), pltpu.SemaphoreType.DMA((2,))],
        ),
        compiler_params=_params("arbitrary"),
        name="moe_combine",
    )(dest, h, rec, yb)


def _ple_math(h_ref, g_ref, wg_ref, p_ref, wp_ref):
    h = h_ref[...]
    a = _rms(h, g_ref[...]).astype(BF16)
    gate = jax.nn.sigmoid(jnp.dot(a, wg_ref[...], preferred_element_type=F32))
    emb = jnp.dot(p_ref[...].astype(BF16), wp_ref[...], preferred_element_type=F32)
    return h + gate * emb


def _ple_kernel(h_ref, g_ref, wg_ref, p_ref, wp_ref, o_ref):
    o_ref[...] = _ple_math(h_ref, g_ref, wg_ref, p_ref, wp_ref)


def _ple_final_kernel(h_ref, g_ref, wg_ref, p_ref, wp_ref, gf_ref, o_ref):
    o_ref[...] = _rms(_ple_math(h_ref, g_ref, wg_ref, p_ref, wp_ref), gf_ref[...])


def _ple(h, g, w_gate, p, layer, w_proj, g_final=None):
    T, D = h.shape
    P = p.shape[2]
    in_specs = [
        pl.BlockSpec((TM, D), lambda i: (i, 0)),
        _resident((1, D)),
        _resident((D, D)),
        pl.BlockSpec((None, TM, P), lambda i: (layer, i, 0)),
        _resident((P, D)),
    ]
    args = (h, g, w_gate, p, w_proj)
    if g_final is not None:
        in_specs.append(_resident((1, D)))
        args += (g_final,)
    return pl.pallas_call(
        _ple_kernel if g_final is None else _ple_final_kernel,
        out_shape=jax.ShapeDtypeStruct((T, D), F32),
        grid=(T // TM,),
        in_specs=in_specs,
        out_specs=pl.BlockSpec((TM, D), lambda i: (i, 0)),
        compiler_params=_params("parallel"),
        name="ple" if g_final is None else "ple_final",
    )(*args)


def _dft_tables(n, scale, rows=None):
    blk = 64
    assert n % blk == 0
    rows = n if rows is None else rows
    j = jnp.arange(rows, dtype=jnp.int32)

    def cos_sin(k):
        ang = ((j[:, None] * k[None, :]) % n).astype(F32) * (2.0 * np.pi / n)
        return jnp.cos(ang), jnp.sin(ang)

    ca, sa = cos_sin(jnp.arange(n // blk, dtype=jnp.int32) * blk)
    cb, sb = cos_sin(jnp.arange(blk, dtype=jnp.int32))
    ca, sa, cb, sb = ca[:, :, None], sa[:, :, None], cb[:, None, :], sb[:, None, :]
    c = (ca * cb - sa * sb) * scale
    s = (sa * cb + ca * sb) * scale
    return c.reshape(rows, n).astype(BF16), s.reshape(rows, n).astype(BF16)


def _norm_nyq_kernel(x_ref, g_ref, a_ref, nyq_ref):
    a = _rms(x_ref[...], g_ref[...]).astype(BF16)
    a_ref[...] = a
    row = lax.broadcasted_iota(jnp.int32, (a.shape[0], 1), 0)
    sign = jnp.where(row % 2 == 0, 1.0, -1.0)
    part = jnp.sum(a.astype(F32) * sign, axis=0, keepdims=True)

    @pl.when(pl.program_id(1) == 0)
    def _():
        nyq_ref[...] = jnp.zeros_like(nyq_ref)

    nyq_ref[...] += jnp.broadcast_to(part, nyq_ref.shape[1:])[None]


def _norm_nyq(h, g, batch, seq):
    T, D = h.shape
    nt = seq // TM
    return pl.pallas_call(
        _norm_nyq_kernel,
        out_shape=(jax.ShapeDtypeStruct((T, D), BF16), jax.ShapeDtypeStruct((batch, BF16_ROWS, D), F32)),
        grid=(batch, nt),
        in_specs=[pl.BlockSpec((TM, D), lambda b, i: (b * nt + i, 0)), _resident((1, D))],
        out_specs=(pl.BlockSpec((TM, D), lambda b, i: (b * nt + i, 0)),
                   pl.BlockSpec((1, BF16_ROWS, D), lambda b, i: (b, 0, 0))),
        compiler_params=_params("parallel", "arbitrary"),
        name="fourier_norm",
    )(h, g)


def _half_dft_kernel(c_ref, s_ref, a_ref, p_ref, q_ref, accp_ref, accq_ref):
    k = pl.program_id(3)

    @pl.when(k == 0)
    def _():
        accp_ref[...] = jnp.zeros_like(accp_ref)
        accq_ref[...] = jnp.zeros_like(accq_ref)

    a = a_ref[...]
    accp_ref[...] += jnp.dot(c_ref[...], a, preferred_element_type=F32)
    accq_ref[...] += jnp.dot(s_ref[...], a, preferred_element_type=F32)

    @pl.when(k == pl.num_programs(3) - 1)
    def _():
        p_ref[...] = accp_ref[...].astype(BF16)
        q_ref[...] = accq_ref[...].astype(BF16)


def _half_dft(a, c, s, batch, seq):
    D = a.shape[1]
    half = seq // 2
    tm, tn, tk = min(DFT_TM, half), DFT_TN, min(DFT_TK, seq)
    out = jax.ShapeDtypeStruct((batch * half, D), BF16)
    ospec = pl.BlockSpec((tm, tn), lambda b, i, j, k: (b * (half // tm) + i, j))
    return pl.pallas_call(
        _half_dft_kernel,
        out_shape=(out, out),
        grid=(batch, half // tm, D // tn, seq // tk),
        in_specs=[
            pl.BlockSpec((tm, tk), lambda b, i, j, k: (i, k)),
            pl.BlockSpec((tm, tk), lambda b, i, j, k: (i, k)),
            pl.BlockSpec((tk, tn), lambda b, i, j, k: (b * (seq // tk) + k, j)),
        ],
        out_specs=(ospec, ospec),
        scratch_shapes=[pltpu.VMEM((tm, tn), F32), pltpu.VMEM((tm, tn), F32)],
        compiler_params=_params("parallel", "parallel", "parallel", "arbitrary"),
        name="fourier_half_dft",
    )(c, s, a)


def _chan_dft(p, q, c_ref, s_ref, sign, gd):
    outs = []
    for grp in range(p.shape[1] // gd):
        sl = slice(grp * gd, (grp + 1) * gd)
        pc = jnp.dot(p[:, sl], c_ref[...], preferred_element_type=F32)
        qs = jnp.dot(q[:, sl], s_ref[...], preferred_element_type=F32)
        outs.append(pc + sign * qs)
    return jnp.concatenate(outs, axis=1)


def _fourier_out_kernel(pa_ref, qa_ref, pb_ref, qb_ref, nyq_ref, c_ref, s_ref, rev_ref, w_ref, h_ref, o_ref,
                        *, n_lo, gd):
    i = pl.program_id(1)

    @pl.when(i < n_lo)
    def _():
        f = _chan_dft(pa_ref[...], qa_ref[...], c_ref, s_ref, -1.0, gd)
        o_ref[...] = h_ref[...] + jnp.dot(f.astype(BF16), w_ref[...], preferred_element_type=F32)

    @pl.when(i >= n_lo)
    def _():
        m = _chan_dft(pa_ref[...], qa_ref[...], c_ref, s_ref, 1.0, gd).astype(BF16)
        rev = jnp.dot(rev_ref[...], m, preferred_element_type=F32)
        mb = _chan_dft(pb_ref[...], qb_ref[...], c_ref, s_ref, 1.0, gd)[0:1, :]
        nyq = nyq_ref[0].astype(BF16)
        mn = _chan_dft(nyq, jnp.zeros_like(nyq), c_ref, s_ref, 1.0, gd)[0:1, :]
        first = jnp.where(i == n_lo, mn, mb)
        row = lax.broadcasted_iota(jnp.int32, rev.shape, 0)
        f = jnp.where(row == 0, first, rev)
        o_ref[...] = h_ref[...] + jnp.dot(f.astype(BF16), w_ref[...], preferred_element_type=F32)


def _fourier_out(p, q, nyq, cc, sc, w, h, batch, seq):
    D = h.shape[1]
    gd = cc.shape[0]
    tm = TM
    half = seq // 2
    assert half % tm == 0
    n_lo = half // tm
    nt = seq // tm
    r = jnp.arange(tm)
    rev = ((r[:, None] + r[None, :]) == tm).astype(BF16)

    def a_map(b, i):
        return (b * n_lo + jnp.where(i < n_lo, i, nt - 1 - i), 0)

    def b_map(b, i):
        t = jnp.where(i < n_lo, i, jnp.minimum(nt - i, n_lo - 1))
        return ((b * n_lo + t) * (tm // BF16_ROWS), 0)

    kern = functools.partial(_fourier_out_kernel, n_lo=n_lo, gd=gd)
    return pl.pallas_call(
        kern,
        out_shape=jax.ShapeDtypeStruct(h.shape, F32),
        grid=(batch, nt),
        in_specs=[
            pl.BlockSpec((tm, D), a_map),
            pl.BlockSpec((tm, D), a_map),
            pl.BlockSpec((BF16_ROWS, D), b_map),
            pl.BlockSpec((BF16_ROWS, D), b_map),
            pl.BlockSpec((1, BF16_ROWS, D), lambda b, i: (b, 0, 0)),
            _resident((gd, gd)),
            _resident((gd, gd)),
            _resident((tm, tm)),
            _resident((D, D)),
            pl.BlockSpec((tm, D), lambda b, i: (b * nt + i, 0)),
        ],
        out_specs=pl.BlockSpec((tm, D), lambda b, i: (b * nt + i, 0)),
        compiler_params=_params("parallel", "parallel"),
        name="fourier_out",
    )(p, q, p, q, nyq, cc, sc, rev, w, h)


def _fourier_mix(h, g, w_out, cc, sc, batch, seq):
    scale = 1.0 / float(np.sqrt(seq))
    cs, sn = _dft_tables(seq, scale, rows=seq // 2)
    a, nyq = _norm_nyq(h, g, batch, seq)
    p, q = _half_dft(a, cs, sn, batch, seq)
    return _fourier_out(p, q, nyq * scale, cc, sc, w_out, h, batch, seq)


def kernel(x_prompt, x_sample, p_prompt, p_sample, g_mix, w_qkv, g_q, g_k, w_attn_out, w_fourier_out, g_ffn, w_route_group, b_route_group, w_route_expert, b_route_expert, w_exp_gate, w_exp_up, w_exp_down, g_ple, w_ple_gate, w_ple_proj, g_final):
    depth = g_mix.shape[0]
    bp, sp, D = x_prompt.shape
    bs, ss, _ = x_sample.shape
    tp, ts = bp * sp, bs * ss
    gd = D // N_FOURIER_GROUPS
    geom = ((bs, ss), (bp, sp))
    hs = (x_sample.reshape(ts, D), x_prompt.reshape(tp, D))
    ps = (p_sample, p_prompt)

    n_mixers = 2
    for i in range(depth):
        jm = i // n_mixers
        g_i = g_mix[i][None, :]
        if i % n_mixers == 0:
            w, gq2, gk2 = _qkv_weights(w_qkv[jm], g_q[jm], g_k[jm])
            cos, sin = _rope_tables(max(sp, ss))
            w_o = w_attn_out[jm].astype(BF16)
            mixed = []
            for h, (batch, seq) in zip(hs, geom):
                qkv = _qkv_proj(h, g_i, w, gq2, gk2, cos, sin, seq)
                mixed.append(_matmul_residual(_attention(qkv, 0, batch, seq), w_o, h))
        else:
            cc, sc = _dft_tables(gd, 1.0 / float(np.sqrt(gd)))
            w_o = w_fourier_out[jm].astype(BF16)
            mixed = [_fourier_mix(h, g_i, w_o, cc, sc, batch, seq) for h, (batch, seq) in zip(hs, geom)]
        hs = _hier_moe(tuple(mixed), g_ffn[i][None, :], w_route_group[i], b_route_group[i], w_route_expert[i],
                       b_route_expert[i], w_exp_gate, w_exp_up, w_exp_down, i)
        w_pg = w_ple_gate[i].astype(BF16)
        w_pp = w_ple_proj[i].astype(BF16)
        g_f = g_final[None, :] if i == depth - 1 else None
        hs = tuple(_ple(h, g_ple[i][None, :], w_pg, p.reshape(depth, h.shape[0], -1), i, w_pp, g_f)
                   for h, p in zip(hs, ps))
    y_sample, y_prompt = hs
    return (y_prompt.reshape(bp, sp, D), y_sample.reshape(bs, ss, D))
```

```python
import functools

import jax
import jax.numpy as jnp
import numpy as np
from jax import lax
from jax.experimental import pallas as pl
from jax.experimental.pallas import tpu as pltpu

F32 = jnp.float32
BF16 = jnp.bfloat16
U32 = jnp.uint32

HEAD_DIM = 128
N_HEADS = 16
N_KV_HEADS = 8
Q_PER_KV = N_HEADS // N_KV_HEADS
ROPE_AXIS_DIM = HEAD_DIM // 2
ROPE_THETA = 10000.0
GRID_W = 64
N_FOURIER_GROUPS = 8
N_EXPERT_GROUPS = 4
EXPERTS_PER_GROUP = 8
N_EXPERTS = N_EXPERT_GROUPS * EXPERTS_PER_GROUP
TOP_K = 2
NORM_EPS = 1e-6

LANES = 128
SUBLANES = 8
BF16_ROWS = 16
MXU_DIM = 256
VMEM_LIMIT = 56 * 1024 * 1024

TM = 512
TQ = 1024
TKV = 1024
EXPERT_BLOCK = 512
EXPERT_STREAM = 256
MOVE_ROWS = 1024
COMBINE_ROWS = 256
DFT_TM, DFT_TN, DFT_TK = 1024, 1024, 2048


def _params(*sem):
    return pltpu.CompilerParams(dimension_semantics=sem, vmem_limit_bytes=VMEM_LIMIT)


def _resident(shape):
    return pl.BlockSpec(shape, lambda *_: (0,) * len(shape), pipeline_mode=pl.Buffered(1))


def _rms(x, g):
    ms = jnp.mean(x * x, axis=-1, keepdims=True)
    return x * lax.rsqrt(ms + NORM_EPS) * g


def _pack_rows(x, ref, rows, row0=0):
    half = SUBLANES * LANES
    for c in range(SUBLANES):
        lo = x[:, c * LANES:(c + 1) * LANES].astype(BF16).astype(F32)
        hi = x[:, half + c * LANES:half + (c + 1) * LANES].astype(BF16).astype(F32)
        word = (pltpu.bitcast(lo, U32) >> 16) | (pltpu.bitcast(hi, U32) & jnp.uint32(0xFFFF0000))
        ref[pl.ds(row0 * SUBLANES + c, rows, stride=SUBLANES), :] = word


def _unpack_rows(ref, rows, c, row0=0):
    word = ref[pl.ds(row0 * SUBLANES + c, rows, stride=SUBLANES), :]
    lo = pltpu.bitcast(word << 16, F32)
    hi = pltpu.bitcast(word & jnp.uint32(0xFFFF0000), F32)
    return lo, hi


def _qkv_kernel(x_ref, g_ref, w_ref, gq_ref, gk_ref, cos_ref, sin_ref, o_ref, *, d_q, d_qk):
    a = _rms(x_ref[...], g_ref[...]).astype(BF16)
    cos = cos_ref[...]
    sin = sin_ref[...]
    q_scale = float(np.log2(np.e) / np.sqrt(HEAD_DIM))
    tabs = {
        "q": (gq_ref[0:1, :] * cos * q_scale, gq_ref[1:2, :] * sin * q_scale),
        "k": (gk_ref[0:1, :] * cos, gk_ref[1:2, :] * sin),
    }
    n_out = o_ref.shape[1]
    for c0 in range(0, n_out, MXU_DIM):
        acc = jnp.dot(a, w_ref[:, c0:c0 + MXU_DIM], preferred_element_type=F32)
        if c0 >= d_qk:
            o_ref[:, c0:c0 + MXU_DIM] = acc.astype(BF16)
            continue
        t1, t2 = tabs["q" if c0 < d_q else "k"]
        for hh in range(MXU_DIM // HEAD_DIM):
            y = acc[:, hh * HEAD_DIM:(hh + 1) * HEAD_DIM]
            r = lax.rsqrt(jnp.mean(y * y, axis=-1, keepdims=True) + NORM_EPS)
            rot = (y * t1 + pltpu.roll(y, HEAD_DIM // 2, 1) * t2) * r
            o_ref[:, c0 + hh * HEAD_DIM:c0 + (hh + 1) * HEAD_DIM] = rot.astype(BF16)


def _permute_heads(x):
    lead = x.shape[:-1]
    x = x.reshape(lead + (-1, 2, 2, HEAD_DIM // 4))
    return jnp.swapaxes(x, -3, -2).reshape(lead + (-1,))


def _rope_tables(n_pos):
    pos = jnp.arange(n_pos, dtype=jnp.int32)
    row = (pos // GRID_W).astype(F32)
    col = (pos % GRID_W).astype(F32)
    inv_freq = ROPE_THETA ** (-jnp.arange(0, ROPE_AXIS_DIM, 2, dtype=F32) / ROPE_AXIS_DIM)
    ang_r = row[:, None] * inv_freq[None, :]
    ang_c = col[:, None] * inv_freq[None, :]
    cos = jnp.concatenate([jnp.cos(ang_r), jnp.cos(ang_c), jnp.cos(ang_r), jnp.cos(ang_c)], axis=-1)
    sin = jnp.concatenate([-jnp.sin(ang_r), -jnp.sin(ang_c), jnp.sin(ang_r), jnp.sin(ang_c)], axis=-1)
    return cos, sin


def _qkv_weights(w_qkv, g_q, g_k):
    d_qk = (N_HEADS + N_KV_HEADS) * HEAD_DIM
    w = jnp.concatenate([_permute_heads(w_qkv[:, :d_qk]).astype(BF16), w_qkv[:, d_qk:].astype(BF16)], axis=1)
    gq = _permute_heads(g_q)
    gk = _permute_heads(g_k)
    gq2 = jnp.stack([gq, jnp.roll(gq, HEAD_DIM // 2)])
    gk2 = jnp.stack([gk, jnp.roll(gk, HEAD_DIM // 2)])
    return w, gq2, gk2


def _qkv_proj(h, g, w, gq2, gk2, cos, sin, seq):
    T, D = h.shape
    N = w.shape[1]
    assert seq % TM == 0 and T % seq == 0
    d_q = N_HEADS * HEAD_DIM
    d_qk = d_q + N_KV_HEADS * HEAD_DIM

    def pos_map(i):
        return (i % (seq // TM), 0)

    kern = functools.partial(_qkv_kernel, d_q=d_q, d_qk=d_qk)
    return pl.pallas_call(
        kern,
        out_shape=jax.ShapeDtypeStruct((T, N), BF16),
        grid=(T // TM,),
        in_specs=[
            pl.BlockSpec((TM, D), lambda i: (i, 0)),
            _resident((1, D)),
            _resident((D, N)),
            _resident((2, HEAD_DIM)),
            _resident((2, HEAD_DIM)),
            pl.BlockSpec((TM, HEAD_DIM), pos_map),
            pl.BlockSpec((TM, HEAD_DIM), pos_map),
        ],
        out_specs=pl.BlockSpec((TM, N), lambda i: (i, 0)),
        compiler_params=_params("parallel"),
        name="qkv_proj",
    )(h, g, w, gq2, gk2, cos, sin)


def _attn_kernel(q_ref, k_ref, v_ref, o_ref, *, tq, tk, seq):
    q = q_ref[...]
    q2 = jnp.concatenate([q[:, :HEAD_DIM], q[:, HEAD_DIM:]], axis=0)
    m = jnp.full((2 * tq, 1), -jnp.inf, F32)
    l = jnp.zeros((2 * tq, 1), F32)
    acc = jnp.zeros((2 * tq, HEAD_DIM), F32)
    for c in range(seq // tk):
        k = k_ref[c * tk:(c + 1) * tk, :]
        v = v_ref[c * tk:(c + 1) * tk, :]
        s = lax.dot_general(q2, k, (((1,), (1,)), ((), ())), preferred_element_type=F32)
        m_new = jnp.maximum(m, jnp.max(s, axis=-1, keepdims=True))
        alpha = jnp.exp2(m - m_new)
        p = jnp.exp2(s - m_new)
        l = alpha * l + jnp.sum(p, axis=-1, keepdims=True)
        acc = alpha * acc + jnp.dot(p.astype(BF16), v, preferred_element_type=F32)
        m = m_new
    o = acc / l
    o_ref[:, :HEAD_DIM] = o[:tq].astype(BF16)
    o_ref[:, HEAD_DIM:] = o[tq:].astype(BF16)


def _attention(qkv, row0, batch, seq):
    assert row0 % seq == 0 and seq % TQ == 0 and seq % TKV == 0
    qw = Q_PER_KV * HEAD_DIM
    kern = functools.partial(_attn_kernel, tq=TQ, tk=TKV, seq=seq)
    q_blk0 = row0 // TQ
    s_blk0 = row0 // seq
    k_col0 = N_HEADS
    v_col0 = N_HEADS + N_KV_HEADS
    return pl.pallas_call(
        kern,
        out_shape=jax.ShapeDtypeStruct((batch * seq, N_HEADS * HEAD_DIM), BF16),
        grid=(batch, N_KV_HEADS, seq // TQ),
        in_specs=[
            pl.BlockSpec((TQ, qw), lambda b, h, i: (q_blk0 + b * (seq // TQ) + i, h)),
            pl.BlockSpec((seq, HEAD_DIM), lambda b, h, i: (s_blk0 + b, k_col0 + h)),
            pl.BlockSpec((seq, HEAD_DIM), lambda b, h, i: (s_blk0 + b, v_col0 + h)),
        ],
        out_specs=pl.BlockSpec((TQ, qw), lambda b, h, i: (b * (seq // TQ) + i, h)),
        compiler_params=_params("parallel", "parallel", "arbitrary"),
        name="attention",
    )(qkv, qkv, qkv)


def _mm_res_kernel(x_ref, w_ref, r_ref, o_ref):
    o_ref[...] = r_ref[...] + jnp.dot(x_ref[...], w_ref[...], preferred_element_type=F32)


def _matmul_residual(x, w, res):
    T, K = x.shape
    N = w.shape[1]
    return pl.pallas_call(
        _mm_res_kernel,
        out_shape=jax.ShapeDtypeStruct((T, N), F32),
        grid=(T // TM,),
        in_specs=[
            pl.BlockSpec((TM, K), lambda i: (i, 0)),
            _resident((K, N)),
            pl.BlockSpec((TM, N), lambda i: (i, 0)),
        ],
        out_specs=pl.BlockSpec((TM, N), lambda i: (i, 0)),
        compiler_params=_params("parallel"),
        name="matmul_residual",
    )(x, w, res)


R_E1, R_E2, R_W1, R_W2, R_RANK1, R_RANK2 = range(6)
ROUTE_LOGIT_E0 = N_EXPERT_GROUPS


def _route_kernel(x0_ref, x1_ref, g_ref, w_ref, b_ref, tri_ref, r_ref, cnt_ref, a_ref, carry_ref, *, n0):
    i = pl.program_id(0)

    @pl.when(i == 0)
    def _():
        carry_ref[...] = jnp.zeros_like(carry_ref)

    a = _rms(jnp.where(i < n0, x0_ref[...], x1_ref[...]), g_ref[...])
    _pack_rows(a, a_ref, a.shape[0])
    a_hi = a.astype(BF16)
    a_lo = (a - a_hi.astype(F32)).astype(BF16)
    hi_both = jnp.dot(a_hi, w_ref[...], preferred_element_type=F32)
    logits = (hi_both[:, :LANES]
              + (jnp.dot(a_lo, w_ref[:, :LANES], preferred_element_type=F32) + hi_both[:, LANES:])) + b_ref[...]
    lane = lax.broadcasted_iota(jnp.int32, logits.shape, 1)
    lane_f = lane.astype(F32)
    neg = -jnp.inf
    far = float(LANES)

    def first_max(vals):
        top = jnp.max(vals, axis=-1, keepdims=True)
        idx = jnp.min(jnp.where(vals == top, lane_f, far), axis=-1, keepdims=True)
        return top, idx.astype(jnp.int32)

    gmask = lane < N_EXPERT_GROUPS
    gtop, g_sel = first_max(jnp.where(gmask, logits, neg))
    pg = 1.0 / jnp.sum(jnp.where(gmask, jnp.exp(logits - gtop), 0.0), axis=-1, keepdims=True)
    lo = ROUTE_LOGIT_E0 + g_sel * EXPERTS_PER_GROUP
    le = jnp.where((lane >= lo) & (lane < lo + EXPERTS_PER_GROUP), logits, neg)
    t1, i1 = first_max(le)
    t2, i2 = first_max(jnp.where(lane == i1, neg, le))
    r21 = jnp.exp(t2 - t1)
    w1 = pg / (1.0 + r21)
    w2 = pg * r21 / (1.0 + r21)
    e1 = i1 - ROUTE_LOGIT_E0
    e2 = i2 - ROUTE_LOGIT_E0

    hit1 = lane == e1
    hit2 = lane == e2
    onehot = (hit1 | hit2).astype(F32)
    before = jnp.dot(tri_ref[...], onehot.astype(BF16), preferred_element_type=F32) + carry_ref[0:1, :]
    rank1 = jnp.sum(jnp.where(hit1, before, 0.0), axis=-1, keepdims=True)
    rank2 = jnp.sum(jnp.where(hit2, before, 0.0), axis=-1, keepdims=True)
    carry_ref[...] = carry_ref[...] + jnp.sum(onehot, axis=0, keepdims=True)
    cnt_ref[...] = carry_ref[...]

    rec = jnp.zeros(logits.shape, F32)
    for pos, val in ((R_E1, e1.astype(F32)), (R_E2, e2.astype(F32)), (R_W1, w1), (R_W2, w2),
                     (R_RANK1, rank1), (R_RANK2, rank2)):
        rec = jnp.where(lane == pos, val, rec)
    r_ref[...] = rec


def _route(hs, g, w_r, b_r):
    h0, h1 = hs
    D = h0.shape[1]
    T = h0.shape[0] + h1.shape[0]
    assert D == 2 * SUBLANES * LANES and h0.shape[0] % TM == 0 and h1.shape[0] % TM == 0
    n0 = h0.shape[0] // TM
    tri = (jnp.arange(TM)[:, None] > jnp.arange(TM)[None, :]).astype(BF16)
    w_hi = w_r.astype(BF16)
    w_lo = (w_r - w_hi.astype(F32)).astype(BF16)
    kern = functools.partial(_route_kernel, n0=n0)
    return pl.pallas_call(
        kern,
        out_shape=(jax.ShapeDtypeStruct((T, LANES), F32), jax.ShapeDtypeStruct((SUBLANES, LANES), F32),
                   jax.ShapeDtypeStruct((T * SUBLANES, LANES), U32)),
        grid=(T // TM,),
        in_specs=[
            pl.BlockSpec((TM, D), lambda i: (jnp.minimum(i, n0 - 1), 0)),
            pl.BlockSpec((TM, D), lambda i: (jnp.maximum(i - n0, 0), 0)),
            _resident((1, D)),
            _resident((D, 2 * LANES)),
            _resident((1, LANES)),
            _resident((TM, TM)),
        ],
        out_specs=(pl.BlockSpec((TM, LANES), lambda i: (i, 0)),
                   pl.BlockSpec((SUBLANES, LANES), lambda i: (0, 0)),
                   pl.BlockSpec((TM * SUBLANES, LANES), lambda i: (i, 0))),
        scratch_shapes=[pltpu.VMEM((SUBLANES, LANES), F32)],
        compiler_params=_params("arbitrary"),
        name="moe_route",
    )(h0, h1, g, jnp.concatenate([w_hi, w_lo], axis=1), b_r, tri)


def _tile_copy(src, src_row, dst, dst_row, sem):
    return pltpu.make_async_copy(src.at[pl.ds(pl.multiple_of(src_row * SUBLANES, SUBLANES), SUBLANES), :],
                                 dst.at[pl.ds(pl.multiple_of(dst_row * SUBLANES, SUBLANES), SUBLANES), :], sem)


def _dispatch_kernel(dest_ref, fill_ref, a_ref, xb_hbm, zero_ref, sem, *, rows, n_blocks):
    i = pl.program_id(0)
    base = i * rows

    def start(r, c):
        for k in range(TOP_K):
            _tile_copy(a_ref, r, xb_hbm, dest_ref[TOP_K * (base + r) + k], sem).start(priority=k)
        return c

    lax.fori_loop(0, rows, start, 0, unroll=8)
    for k in range(TOP_K):
        pltpu.make_async_copy(a_ref, xb_hbm.at[pl.ds(0, rows * SUBLANES), :], sem).wait()

    @pl.when(i == pl.num_programs(0) - 1)
    def _():
        zero_ref[...] = jnp.zeros_like(zero_ref)

        def per_expert(e, c):
            lo = fill_ref[2 * e]
            hi = fill_ref[2 * e + 1]

            def zstart(r, cc):
                _tile_copy(zero_ref, 0, xb_hbm, r, sem).start()
                return cc

            def zwait(r, cc):
                _tile_copy(zero_ref, 0, xb_hbm, r, sem).wait()
                return cc

            lax.fori_loop(lo, hi, zstart, 0)
            lax.fori_loop(lo, hi, zwait, 0)
            return c

        lax.fori_loop(0, N_EXPERTS, per_expert, 0)

        blk = zero_ref.shape[0]

        def block_copy(b):
            return pltpu.make_async_copy(zero_ref, xb_hbm.at[pl.ds(pl.multiple_of(b * blk, blk), blk), :], sem)

        def bstart(b, c):
            block_copy(b).start()
            return c

        def bwait(b, c):
            block_copy(b).wait()
            return c

        first_unused = fill_ref[2 * N_EXPERTS]
        lax.fori_loop(first_unused, n_blocks, bstart, 0)
        lax.fori_loop(first_unused, n_blocks, bwait, 0)


def _dispatch(a_slab, dest, fill, n_blocks, bm):
    T = a_slab.shape[0] // SUBLANES
    rows = MOVE_ROWS
    kern = functools.partial(_dispatch_kernel, rows=rows, n_blocks=n_blocks)
    return pl.pallas_call(
        kern,
        out_shape=jax.ShapeDtypeStruct((n_blocks * bm * SUBLANES, LANES), U32),
        grid_spec=pltpu.PrefetchScalarGridSpec(
            num_scalar_prefetch=2,
            grid=(T // rows,),
            in_specs=[pl.BlockSpec((rows * SUBLANES, LANES), lambda i, d, f: (i, 0))],
            out_specs=pl.BlockSpec(memory_space=pl.ANY),
            scratch_shapes=[pltpu.VMEM((bm * SUBLANES, LANES), U32), pltpu.SemaphoreType.DMA(())],
        ),
        compiler_params=_params("arbitrary"),
        name="moe_dispatch",
    )(dest, fill, a_slab)


M_USED, M_RUNS, M_EXPERT0 = 0, 1, 2


def _expert_kernel(be_ref, run_ref, meta_ref, x_ref, wg_hbm, wu_hbm, wd_hbm, o_ref,
                   x_s, wg_s, wu_s, wd_s, wg_f, wu_f, wd_f, sem, *, bm, layer):
    b = pl.program_id(0)
    n_used = meta_ref[M_USED]
    r = run_ref[b]

    def weight_copies(run, slot):
        e = meta_ref[M_EXPERT0 + run]
        return [pltpu.make_async_copy(w.at[layer, e], f.at[slot], sem.at[slot, j])
                for j, (w, f) in enumerate(((wg_hbm, wg_f), (wu_hbm, wu_f), (wd_hbm, wd_f)))]

    @pl.when((b < n_used) & ((b == 0) | (be_ref[b] != be_ref[jnp.maximum(b - 1, 0)])))
    def _():
        slot = r % 2

        @pl.when(r == 0)
        def _():
            for cp in weight_copies(0, 0):
                cp.start()

        for cp in weight_copies(r, slot):
            cp.wait()
        wg_s[...] = wg_f[slot].astype(BF16)
        wu_s[...] = wu_f[slot].astype(BF16)
        wd_s[...] = wd_f[slot].astype(BF16)

        @pl.when(r + 1 < meta_ref[M_RUNS])
        def _():
            for cp in weight_copies(r + 1, 1 - slot):
                cp.start()

    @pl.when(b < n_used)
    def _():
        half = SUBLANES * LANES
        for r0 in range(0, bm, EXPERT_STREAM):
            rs = slice(r0, r0 + EXPERT_STREAM)
            for c in range(SUBLANES):
                lo, hi = _unpack_rows(x_ref, EXPERT_STREAM, c, r0)
                x_s[rs, c * LANES:(c + 1) * LANES] = lo.astype(BF16)
                x_s[rs, half + c * LANES:half + (c + 1) * LANES] = hi.astype(BF16)
            a = x_s[rs, :]
            hg = jnp.dot(a, wg_s[...], preferred_element_type=F32)
            hu = jnp.dot(a, wu_s[...], preferred_element_type=F32)
            h = hg * jax.nn.sigmoid(hg) * hu
            y = jnp.dot(h.astype(BF16), wd_s[...], preferred_element_type=F32)
            _pack_rows(y, o_ref, EXPERT_STREAM, r0)

    @pl.when(b >= n_used)
    def _():
        o_ref[...] = jnp.zeros_like(o_ref)


def _experts(xb, w_gate, w_up, w_down, layer, blk_e, blk_run, meta):
    bm = EXPERT_BLOCK
    n_blocks = xb.shape[0] // (bm * SUBLANES)
    _, _, D, De = w_gate.shape
    kern = functools.partial(_expert_kernel, bm=bm, layer=layer)
    hbm = pl.BlockSpec(memory_space=pl.ANY)
    return pl.pallas_call(
        kern,
        out_shape=jax.ShapeDtypeStruct(xb.shape, U32),
        grid_spec=pltpu.PrefetchScalarGridSpec(
            num_scalar_prefetch=3,
            grid=(n_blocks,),
            in_specs=[
                pl.BlockSpec((bm * SUBLANES, LANES), lambda b, be, run, meta: (jnp.minimum(b, meta[M_USED] - 1), 0)),
                hbm, hbm, hbm,
            ],
            out_specs=pl.BlockSpec((bm * SUBLANES, LANES), lambda b, be, run, meta: (b, 0)),
            scratch_shapes=[pltpu.VMEM((bm, D), BF16), pltpu.VMEM((D, De), BF16),
                            pltpu.VMEM((D, De), BF16), pltpu.VMEM((De, D), BF16),
                            pltpu.VMEM((2, D, De), F32), pltpu.VMEM((2, D, De), F32), pltpu.VMEM((2, De, D), F32),
                            pltpu.SemaphoreType.DMA((2, 3))],
        ),
        compiler_params=_params("arbitrary"),
        name="moe_experts",
    )(blk_e, blk_run, meta, xb, w_gate, w_up, w_down)


def _hier_moe(hs, g, w_rg, b_rg, w_re, b_re, w_gate, w_up, w_down, layer):
    D = hs[0].shape[1]
    T = hs[0].shape[0] + hs[1].shape[0]
    pad = LANES - N_EXPERT_GROUPS - N_EXPERTS
    w_r = jnp.concatenate([w_rg, w_re, jnp.zeros((D, pad), F32)], axis=1)
    b_r = jnp.concatenate([b_rg, b_re, jnp.zeros((pad,), F32)])[None, :]
    rec, cnt, a_slab = _route(hs, g, w_r, b_r)

    bm = EXPERT_BLOCK
    e_tok = rec[:, R_E1:R_E2 + 1].astype(jnp.int32)
    rank = rec[:, R_RANK1:R_RANK2 + 1].astype(jnp.int32)
    counts = cnt[0, :N_EXPERTS].astype(jnp.int32)
    pad_counts = (counts + bm - 1) // bm * bm
    pad_end = jnp.cumsum(pad_counts)
    pad_start = pad_end - pad_counts
    start_tok = jnp.sum(jnp.where(e_tok[..., None] == jnp.arange(N_EXPERTS, dtype=jnp.int32), pad_start, 0), axis=-1)
    dest = (start_tok + rank).reshape(-1)
    n_blocks = (T * TOP_K) // bm + N_EXPERTS
    blk_start = jnp.arange(n_blocks, dtype=jnp.int32) * bm
    blk_e = jnp.minimum(jnp.sum(blk_start[:, None] >= pad_end[None, :], axis=1), N_EXPERTS - 1).astype(jnp.int32)
    n_used = (pad_end[-1:] // bm).astype(jnp.int32)
    fill = jnp.concatenate([jnp.stack([pad_start + counts, pad_end], axis=1).reshape(-1), n_used]).astype(jnp.int32)
    blk = jnp.arange(n_blocks, dtype=jnp.int32)
    run_starts = (blk < n_used[0]) & ((blk == 0) | (blk_e != jnp.roll(blk_e, 1)))
    blk_run = (jnp.cumsum(run_starts) - 1).astype(jnp.int32)
    e_ids = jnp.arange(N_EXPERTS, dtype=jnp.int32)
    run_experts = jnp.sort(jnp.where(counts > 0, e_ids, e_ids + N_EXPERTS)) % N_EXPERTS
    meta = jnp.concatenate([n_used, jnp.sum(run_starts, keepdims=True), run_experts]).astype(jnp.int32)

    xb = _dispatch(a_slab, dest, fill, n_blocks, bm)
    yb = _experts(xb, w_gate, w_up, w_down, layer, blk_e, blk_run, meta)
    return (_combine(hs[0], rec, yb, dest, 0), _combine(hs[1], rec, yb, dest, hs[0].shape[0]))


def _combine_kernel(dest_ref, h_ref, r_ref, yb_hbm, o_ref, y_ref, sem, *, rows, tok0):
    i = pl.program_id(0)

    def issue(step, slot):
        base = tok0 + step * rows

        def start(r, c):
            for k in range(TOP_K):
                _tile_copy(yb_hbm, dest_ref[TOP_K * (base + r) + k], y_ref.at[slot, k], r,
                           sem.at[slot]).start(priority=k)
            return c

        lax.fori_loop(0, rows, start, 0, unroll=8)

    def finish(slot):
        for k in range(TOP_K):
            pltpu.make_async_copy(yb_hbm.at[pl.ds(0, rows * SUBLANES), :], y_ref.at[slot, k], sem.at[slot]).wait()
        rec = r_ref[...]
        w1 = rec[:, R_W1:R_W1 + 1]
        w2 = rec[:, R_W2:R_W2 + 1]
        half = SUBLANES * LANES
        for c in range(SUBLANES):
            lo1, hi1 = _unpack_rows(y_ref.at[slot, 0], rows, c)
            lo2, hi2 = _unpack_rows(y_ref.at[slot, 1], rows, c)
            sl = slice(c * LANES, (c + 1) * LANES)
            sh = slice(half + c * LANES, half + (c + 1) * LANES)
            o_ref[:, sl] = h_ref[:, sl] + (w1 * lo1 + w2 * lo2)
            o_ref[:, sh] = h_ref[:, sh] + (w1 * hi1 + w2 * hi2)

    @pl.when(i == 0)
    def _():
        issue(0, 0)

    for slot in range(2):
        @pl.when(i % 2 == slot)
        def _():
            @pl.when(i + 1 < pl.num_programs(0))
            def _():
                issue(i + 1, 1 - slot)

            finish(slot)


def _combine(h, rec, yb, dest, tok0):
    Tg, D = h.shape
    rows = COMBINE_ROWS
    assert tok0 % rows == 0 and Tg % rows == 0
    blk0 = tok0 // rows
    kern = functools.partial(_combine_kernel, rows=rows, tok0=tok0)
    return pl.pallas_call(
        kern,
        out_shape=jax.ShapeDtypeStruct((Tg, D), F32),
        grid_spec=pltpu.PrefetchScalarGridSpec(
            num_scalar_prefetch=1,
            grid=(Tg // rows,),
            in_specs=[
                pl.BlockSpec((rows, D), lambda i, d: (i, 0)),
                pl.BlockSpec((rows, LANES), lambda i, d: (blk0 + i, 0)),
                pl.BlockSpec(memory_space=pl.ANY),
            ],
            out_specs=pl.BlockSpec((rows, D), lambda i, d: (i, 0)),
            scratch_shapes=[pltpu.VMEM((2, TOP_K, rows * SUBLANES, LANES), U32), pltpu.SemaphoreType.DMA((2,))],
        ),
        compiler_params=_params("arbitrary"),
        name="moe_combine",
    )(dest, h, rec, yb)


def _ple_math(h_ref, g_ref, wg_ref, p_ref, wp_ref):
    h = h_ref[...]
    a = _rms(h, g_ref[...]).astype(BF16)
    gate = jax.nn.sigmoid(jnp.dot(a, wg_ref[...], preferred_element_type=F32))
    emb = jnp.dot(p_ref[...].astype(BF16), wp_ref[...], preferred_element_type=F32)
    return h + gate * emb


def _ple_kernel(h_ref, g_ref, wg_ref, p_ref, wp_ref, o_ref):
    o_ref[...] = _ple_math(h_ref, g_ref, wg_ref, p_ref, wp_ref)


def _ple_final_kernel(h_ref, g_ref, wg_ref, p_ref, wp_ref, gf_ref, o_ref):
    o_ref[...] = _rms(_ple_math(h_ref, g_ref, wg_ref, p_ref, wp_ref), gf_ref[...])


def _ple(h, g, w_gate, p, layer, w_proj, g_final=None):
    T, D = h.shape
    P = p.shape[2]
    in_specs = [
        pl.BlockSpec((TM, D), lambda i: (i, 0)),
        _resident((1, D)),
        _resident((D, D)),
        pl.BlockSpec((None, TM, P), lambda i: (layer, i, 0)),
        _resident((P, D)),
    ]
    args = (h, g, w_gate, p, w_proj)
    if g_final is not None:
        in_specs.append(_resident((1, D)))
        args += (g_final,)
    return pl.pallas_call(
        _ple_kernel if g_final is None else _ple_final_kernel,
        out_shape=jax.ShapeDtypeStruct((T, D), F32),
        grid=(T // TM,),
        in_specs=in_specs,
        out_specs=pl.BlockSpec((TM, D), lambda i: (i, 0)),
        compiler_params=_params("parallel"),
        name="ple" if g_final is None else "ple_final",
    )(*args)


def _dft_tables(n, scale, rows=None):
    blk = 64
    assert n % blk == 0
    rows = n if rows is None else rows
    j = jnp.arange(rows, dtype=jnp.int32)

    def cos_sin(k):
        ang = ((j[:, None] * k[None, :]) % n).astype(F32) * (2.0 * np.pi / n)
        return jnp.cos(ang), jnp.sin(ang)

    ca, sa = cos_sin(jnp.arange(n // blk, dtype=jnp.int32) * blk)
    cb, sb = cos_sin(jnp.arange(blk, dtype=jnp.int32))
    ca, sa, cb, sb = ca[:, :, None], sa[:, :, None], cb[:, None, :], sb[:, None, :]
    c = (ca * cb - sa * sb) * scale
    s = (sa * cb + ca * sb) * scale
    return c.reshape(rows, n).astype(BF16), s.reshape(rows, n).astype(BF16)


def _norm_nyq_kernel(x_ref, g_ref, a_ref, nyq_ref):
    a = _rms(x_ref[...], g_ref[...]).astype(BF16)
    a_ref[...] = a
    row = lax.broadcasted_iota(jnp.int32, (a.shape[0], 1), 0)
    sign = jnp.where(row % 2 == 0, 1.0, -1.0)
    part = jnp.sum(a.astype(F32) * sign, axis=0, keepdims=True)

    @pl.when(pl.program_id(1) == 0)
    def _():
        nyq_ref[...] = jnp.zeros_like(nyq_ref)

    nyq_ref[...] += jnp.broadcast_to(part, nyq_ref.shape[1:])[None]


def _norm_nyq(h, g, batch, seq):
    T, D = h.shape
    nt = seq // TM
    return pl.pallas_call(
        _norm_nyq_kernel,
        out_shape=(jax.ShapeDtypeStruct((T, D), BF16), jax.ShapeDtypeStruct((batch, BF16_ROWS, D), F32)),
        grid=(batch, nt),
        in_specs=[pl.BlockSpec((TM, D), lambda b, i: (b * nt + i, 0)), _resident((1, D))],
        out_specs=(pl.BlockSpec((TM, D), lambda b, i: (b * nt + i, 0)),
                   pl.BlockSpec((1, BF16_ROWS, D), lambda b, i: (b, 0, 0))),
        compiler_params=_params("parallel", "arbitrary"),
        name="fourier_norm",
    )(h, g)


def _half_dft_kernel(c_ref, s_ref, a_ref, p_ref, q_ref, accp_ref, accq_ref):
    k = pl.program_id(3)

    @pl.when(k == 0)
    def _():
        accp_ref[...] = jnp.zeros_like(accp_ref)
        accq_ref[...] = jnp.zeros_like(accq_ref)

    a = a_ref[...]
    accp_ref[...] += jnp.dot(c_ref[...], a, preferred_element_type=F32)
    accq_ref[...] += jnp.dot(s_ref[...], a, preferred_element_type=F32)

    @pl.when(k == pl.num_programs(3) - 1)
    def _():
        p_ref[...] = accp_ref[...].astype(BF16)
        q_ref[...] = accq_ref[...].astype(BF16)


def _half_dft(a, c, s, batch, seq):
    D = a.shape[1]
    half = seq // 2
    tm, tn, tk = min(DFT_TM, half), DFT_TN, min(DFT_TK, seq)
    out = jax.ShapeDtypeStruct((batch * half, D), BF16)
    ospec = pl.BlockSpec((tm, tn), lambda b, i, j, k: (b * (half // tm) + i, j))
    return pl.pallas_call(
        _half_dft_kernel,
        out_shape=(out, out),
        grid=(batch, half // tm, D // tn, seq // tk),
        in_specs=[
            pl.BlockSpec((tm, tk), lambda b, i, j, k: (i, k)),
            pl.BlockSpec((tm, tk), lambda b, i, j, k: (i, k)),
            pl.BlockSpec((tk, tn), lambda b, i, j, k: (b * (seq // tk) + k, j)),
        ],
        out_specs=(ospec, ospec),
        scratch_shapes=[pltpu.VMEM((tm, tn), F32), pltpu.VMEM((tm, tn), F32)],
        compiler_params=_params("parallel", "parallel", "parallel", "arbitrary"),
        name="fourier_half_dft",
    )(c, s, a)


def _chan_dft(p, q, c_ref, s_ref, sign, gd):
    outs = []
    for grp in range(p.shape[1] // gd):
        sl = slice(grp * gd, (grp + 1) * gd)
        pc = jnp.dot(p[:, sl], c_ref[...], preferred_element_type=F32)
        qs = jnp.dot(q[:, sl], s_ref[...], preferred_element_type=F32)
        outs.append(pc + sign * qs)
    return jnp.concatenate(outs, axis=1)


def _fourier_out_kernel(pa_ref, qa_ref, pb_ref, qb_ref, nyq_ref, c_ref, s_ref, rev_ref, w_ref, h_ref, o_ref,
                        *, n_lo, gd):
    i = pl.program_id(1)

    @pl.when(i < n_lo)
    def _():
        f = _chan_dft(pa_ref[...], qa_ref[...], c_ref, s_ref, -1.0, gd)
        o_ref[...] = h_ref[...] + jnp.dot(f.astype(BF16), w_ref[...], preferred_element_type=F32)

    @pl.when(i >= n_lo)
    def _():
        m = _chan_dft(pa_ref[...], qa_ref[...], c_ref, s_ref, 1.0, gd).astype(BF16)
        rev = jnp.dot(rev_ref[...], m, preferred_element_type=F32)
        mb = _chan_dft(pb_ref[...], qb_ref[...], c_ref, s_ref, 1.0, gd)[0:1, :]
        nyq = nyq_ref[0].astype(BF16)
        mn = _chan_dft(nyq, jnp.zeros_like(nyq), c_ref, s_ref, 1.0, gd)[0:1, :]
        first = jnp.where(i == n_lo, mn, mb)
        row = lax.broadcasted_iota(jnp.int32, rev.shape, 0)
        f = jnp.where(row == 0, first, rev)
        o_ref[...] = h_ref[...] + jnp.dot(f.astype(BF16), w_ref[...], preferred_element_type=F32)


def _fourier_out(p, q, nyq, cc, sc, w, h, batch, seq):
    D = h.shape[1]
    gd = cc.shape[0]
    tm = TM
    half = seq // 2
    assert half % tm == 0
    n_lo = half // tm
    nt = seq // tm
    r = jnp.arange(tm)
    rev = ((r[:, None] + r[None, :]) == tm).astype(BF16)

    def a_map(b, i):
        return (b * n_lo + jnp.where(i < n_lo, i, nt - 1 - i), 0)

    def b_map(b, i):
        t = jnp.where(i < n_lo, i, jnp.minimum(nt - i, n_lo - 1))
        return ((b * n_lo + t) * (tm // BF16_ROWS), 0)

    kern = functools.partial(_fourier_out_kernel, n_lo=n_lo, gd=gd)
    return pl.pallas_call(
        kern,
        out_shape=jax.ShapeDtypeStruct(h.shape, F32),
        grid=(batch, nt),
        in_specs=[
            pl.BlockSpec((tm, D), a_map),
            pl.BlockSpec((tm, D), a_map),
            pl.BlockSpec((BF16_ROWS, D), b_map),
            pl.BlockSpec((BF16_ROWS, D), b_map),
            pl.BlockSpec((1, BF16_ROWS, D), lambda b, i: (b, 0, 0)),
            _resident((gd, gd)),
            _resident((gd, gd)),
            _resident((tm, tm)),
            _resident((D, D)),
            pl.BlockSpec((tm, D), lambda b, i: (b * nt + i, 0)),
        ],
        out_specs=pl.BlockSpec((tm, D), lambda b, i: (b * nt + i, 0)),
        compiler_params=_params("parallel", "parallel"),
        name="fourier_out",
    )(p, q, p, q, nyq, cc, sc, rev, w, h)


def _fourier_mix(h, g, w_out, cc, sc, batch, seq):
    scale = 1.0 / float(np.sqrt(seq))
    cs, sn = _dft_tables(seq, scale, rows=seq // 2)
    a, nyq = _norm_nyq(h, g, batch, seq)
    p, q = _half_dft(a, cs, sn, batch, seq)
    return _fourier_out(p, q, nyq * scale, cc, sc, w_out, h, batch, seq)


def kernel(x_prompt, x_sample, p_prompt, p_sample, g_mix, w_qkv, g_q, g_k, w_attn_out, w_fourier_out, g_ffn, w_route_group, b_route_group, w_route_expert, b_route_expert, w_exp_gate, w_exp_up, w_exp_down, g_ple, w_ple_gate, w_ple_proj, g_final):
    depth = g_mix.shape[0]
    bp, sp, D = x_prompt.shape
    bs, ss, _ = x_sample.shape
    tp, ts = bp * sp, bs * ss
    gd = D // N_FOURIER_GROUPS
    geom = ((bs, ss), (bp, sp))
    hs = (x_sample.reshape(ts, D), x_prompt.reshape(tp, D))
    ps = (p_sample, p_prompt)

    n_mixers = 2
    for i in range(depth):
        jm = i // n_mixers
        g_i = g_mix[i][None, :]
        if i % n_mixers == 0:
            w, gq2, gk2 = _qkv_weights(w_qkv[jm], g_q[jm], g_k[jm])
            cos, sin = _rope_tables(max(sp, ss))
            w_o = w_attn_out[jm].astype(BF16)
            mixed = []
            for h, (batch, seq) in zip(hs, geom):
                qkv = _qkv_proj(h, g_i, w, gq2, gk2, cos, sin, seq)
                mixed.append(_matmul_residual(_attention(qkv, 0, batch, seq), w_o, h))
        else:
            cc, sc = _dft_tables(gd, 1.0 / float(np.sqrt(gd)))
            w_o = w_fourier_out[jm].astype(BF16)
            mixed = [_fourier_mix(h, g_i, w_o, cc, sc, batch, seq) for h, (batch, seq) in zip(hs, geom)]
        hs = _hier_moe(tuple(mixed), g_ffn[i][None, :], w_route_group[i], b_route_group[i], w_route_expert[i],
                       b_route_expert[i], w_exp_gate, w_exp_up, w_exp_down, i)
        w_pg = w_ple_gate[i].astype(BF16)
        w_pp = w_ple_proj[i].astype(BF16)
        g_f = g_final[None, :] if i == depth - 1 else None
        hs = tuple(_ple(h, g_ple[i][None, :], w_pg, p.reshape(depth, h.shape[0], -1), i, w_pp, g_f)
                   for h, p in zip(hs, ps))
    y_sample, y_prompt = hs
    return (y_prompt.reshape(bp, sp, D), y_sample.reshape(bs, ss, D))
```
